```python
import math
import jax
import jax.numpy as jnp
from jax import lax
import numpy as np

D_MODEL = 2048
BATCH = 2
SEQ = 4096
DEPTH = 4
DEC_BATCH = 8
DEC_SEQ = 1
PAST_LEN = 16384
PAGE_SIZE = 128

N_META = 16
D_FF = 5632
NORM_EPS = 1e-6
POOL_DIM = 1024
POOL_WINDOWS = (2, 4, 8, 16)
POOL_GROUPS = len(POOL_WINDOWS)
POOL_GROUP_DIM = POOL_DIM // POOL_GROUPS
POOL_HIST = max(POOL_WINDOWS) - 1
SSM_D_INNER = 1024
SSM_HEAD_DIM = 64
SSM_HEADS = SSM_D_INNER // SSM_HEAD_DIM
SSM_GROUPS = 2
SSM_HEADS_PER_GROUP = SSM_HEADS // SSM_GROUPS
SSM_STATE = 128
SSM_CONV = 4
SSM_CONV_DIM = SSM_D_INNER + 2 * SSM_GROUPS * SSM_STATE
SSM_CHUNK = 128
DT_MIN = 0.001
DT_MAX = 0.1
ATT_HEADS = 16
ATT_HEAD_DIM = 64
ATT_DIM = ATT_HEADS * ATT_HEAD_DIM
Q_BLOCK = 128
NEG_INF = -1e30
N_BRANCH = 3
OFF_POOL = 0
OFF_Z = OFF_POOL + POOL_DIM
OFF_XBC = OFF_Z + SSM_D_INNER
OFF_DT = OFF_XBC + SSM_CONV_DIM
OFF_Q = OFF_DT + SSM_HEADS
OFF_K = OFF_Q + ATT_DIM
OFF_V = OFF_K + ATT_DIM
OFF_F = OFF_V + ATT_DIM
OFF_GATE = OFF_F + ATT_HEADS
IN_DIM = OFF_GATE + N_BRANCH * D_MODEL

kernel_name = 'meta_pool_ssd_fox_gated_macaron_step'


def rms_norm(x, g):
    xf = x.astype(jnp.float32)
    y = xf * lax.rsqrt(jnp.mean(xf * xf, axis=-1, keepdims=True) + NORM_EPS)
    return (y * g.astype(jnp.float32)).astype(x.dtype)


def swiglu(x, w_gu, w_down):
    gate, up = jnp.split(x @ w_gu, 2, axis=-1)
    return (jax.nn.silu(gate) * up) @ w_down


def pool_mixer(xp, hist, pos0, w_group, scale):
    b, t, _ = xp.shape
    full = jnp.concatenate([hist, xp], axis=1)
    cs = jnp.cumsum(full.astype(jnp.float32), axis=1)
    cs = jnp.concatenate([jnp.zeros((b, 1, POOL_DIM), jnp.float32), cs], axis=1)
    pos = pos0 + jnp.arange(t)
    xf = xp.astype(jnp.float32)
    outs = []
    for g, w in enumerate(POOL_WINDOWS):
        sl = slice(g * POOL_GROUP_DIM, (g + 1) * POOL_GROUP_DIM)
        end = cs[:, POOL_HIST + 1:POOL_HIST + 1 + t, sl]
        start = cs[:, POOL_HIST + 1 - w:POOL_HIST + 1 - w + t, sl]
        cnt = jnp.minimum(w, pos + 1).astype(jnp.float32)
        outs.append((end - start) / cnt[None, :, None] - xf[:, :, sl])
    pooled = jnp.stack(outs, axis=2).astype(xp.dtype)
    mixed = jnp.einsum('btgc,gcd->btgd', pooled, w_group).reshape(b, t, POOL_DIM)
    return mixed * scale, full[:, -POOL_HIST:]


def causal_dwconv(xbc, hist, w, bias):
    t = xbc.shape[1]
    full = jnp.concatenate([hist, xbc], axis=1)
    out = bias
    for k in range(SSM_CONV):
        out = out + full[:, k:k + t] * w[k]
    return jax.nn.silu(out), full[:, -(SSM_CONV - 1):]


def ssd_scan(x, dt, a, bm, cm, h0, pad_left):
    f32 = jnp.float32
    b, t = x.shape[:2]
    pad = (-t) % SSM_CHUNK
    padw = (pad, 0) if pad_left else (0, pad)
    nc = (t + pad) // SSM_CHUNK

    def padt(z):
        return jnp.pad(z, ((0, 0), padw) + ((0, 0),) * (z.ndim - 2))

    xs = padt(x.astype(f32)).reshape(b, nc, SSM_CHUNK, SSM_GROUPS, SSM_HEADS_PER_GROUP, SSM_HEAD_DIM)
    dts = padt(dt).reshape(b, nc, SSM_CHUNK, SSM_GROUPS, SSM_HEADS_PER_GROUP)
    bs = padt(bm.astype(f32)).reshape(b, nc, SSM_CHUNK, SSM_GROUPS, SSM_STATE)
    cs = padt(cm.astype(f32)).reshape(b, nc, SSM_CHUNK, SSM_GROUPS, SSM_STATE)
    acum = jnp.cumsum(dts * a.reshape(SSM_GROUPS, SSM_HEADS_PER_GROUP), axis=2)
    causal = jnp.tril(jnp.ones((SSM_CHUNK, SSM_CHUNK), bool))[None, None, :, :, None, None]
    seg = acum[:, :, :, None] - acum[:, :, None, :]
    decay = jnp.exp(jnp.where(causal, seg, -jnp.inf))
    cb = jnp.einsum('bclgn,bcsgn->bclsg', cs, bs)
    wts = cb[..., None] * decay * dts[:, :, None]
    y = jnp.einsum('bclsgh,bcsghp->bclghp', wts, xs)
    to_end = jnp.exp(acum[:, :, -1:] - acum) * dts
    st = jnp.einsum('bclgn,bclgh,bclghp->bcghpn', bs, to_end, xs)
    chunk_decay = jnp.exp(acum[:, :, -1])

    def step(h, inp):
        dec, s = inp
        return dec[..., None, None] * h + s, h

    h_init = h0.astype(f32).reshape(b, SSM_GROUPS, SSM_HEADS_PER_GROUP, SSM_HEAD_DIM, SSM_STATE)
    h_last, h_prev = lax.scan(step, h_init, (jnp.moveaxis(chunk_decay, 1, 0), jnp.moveaxis(st, 1, 0)))
    y = y + jnp.einsum('bclgn,cbghpn->bclghp', cs, h_prev) * jnp.exp(acum)[..., None]
    y = y.reshape(b, nc * SSM_CHUNK, SSM_HEADS, SSM_HEAD_DIM)
    y = y[:, pad:] if pad_left else y[:, :t]
    return y, h_last.reshape(b, SSM_HEADS, SSM_HEAD_DIM, SSM_STATE)


def fox_attention(q, k_all, v_all, c_q, c_all, q_pos, pad_left):
    f32 = jnp.float32
    b, t = q.shape[:2]
    qb = min(Q_BLOCK, t)
    pad = (-t) % qb
    padw = (pad, 0) if pad_left else (0, pad)
    nb = (t + pad) // qb
    qp = jnp.moveaxis(jnp.pad(q, ((0, 0), padw, (0, 0), (0, 0))).reshape(b, nb, qb, ATT_HEADS, ATT_HEAD_DIM), 1, 0)
    cqp = jnp.moveaxis(jnp.pad(c_q, ((0, 0), padw, (0, 0))).reshape(b, nb, qb, ATT_HEADS), 1, 0)
    posp = jnp.pad(q_pos, padw, constant_values=-1).reshape(nb, qb)
    k_pos = jnp.arange(k_all.shape[1])
    c_k = jnp.swapaxes(c_all, 1, 2)[:, :, None, :]
    scale = ATT_HEAD_DIM ** -0.5

    def block(args):
        qblk, cblk, pblk = args
        s = jnp.einsum('bqhd,bkhd->bhqk', qblk, k_all).astype(f32) * scale
        s = s + jnp.swapaxes(cblk, 1, 2)[..., None] - c_k
        s = jnp.where(k_pos[None, :] <= pblk[:, None], s, NEG_INF)
        p = jax.nn.softmax(s, axis=-1)
        return jnp.einsum('bhqk,bkhd->bqhd', p.astype(v_all.dtype), v_all)

    o = lax.map(block, (qp, cqp, posp))
    o = jnp.moveaxis(o, 0, 1).reshape(b, nb * qb, ATT_HEADS, ATT_HEAD_DIM)
    return o[:, pad:] if pad_left else o[:, :t]


def hybrid_mixer(u, lp, pos0, pool_hist, conv_hist, h0, k_past, v_past, lf_past, pad_left):
    f32 = jnp.float32
    b, t, _ = u.shape
    proj = u @ lp['w_in']
    xp = proj[..., OFF_POOL:OFF_Z]
    z = proj[..., OFF_Z:OFF_XBC]
    xbc = proj[..., OFF_XBC:OFF_DT]
    dt_raw = proj[..., OFF_DT:OFF_Q]
    q = proj[..., OFF_Q:OFF_K].reshape(b, t, ATT_HEADS, ATT_HEAD_DIM)
    k = proj[..., OFF_K:OFF_V].reshape(b, t, ATT_HEADS, ATT_HEAD_DIM)
    v = proj[..., OFF_V:OFF_F].reshape(b, t, ATT_HEADS, ATT_HEAD_DIM)
    f_raw = proj[..., OFF_F:OFF_GATE]
    gates = jax.nn.sigmoid(proj[..., OFF_GATE:].reshape(b, t, N_BRANCH, D_MODEL))
    pool_out, new_pool = pool_mixer(xp, pool_hist, pos0, lp['pool_w'], lp['pool_scale'])
    xbc_c, new_conv = causal_dwconv(xbc, conv_hist, lp['conv_w'], lp['conv_b'])
    xs = xbc_c[..., :SSM_D_INNER].reshape(b, t, SSM_HEADS, SSM_HEAD_DIM)
    bm = xbc_c[..., SSM_D_INNER:SSM_D_INNER + SSM_GROUPS * SSM_STATE].reshape(b, t, SSM_GROUPS, SSM_STATE)
    cm = xbc_c[..., SSM_D_INNER + SSM_GROUPS * SSM_STATE:].reshape(b, t, SSM_GROUPS, SSM_STATE)
    dt = jax.nn.softplus(dt_raw.astype(f32) + lp['dt_bias'].astype(f32))
    a = -jnp.exp(lp['a_log'].astype(f32))
    y, h_new = ssd_scan(xs, dt, a, bm, cm, h0, pad_left)
    y = (y + lp['d_skip'].astype(f32)[:, None] * xs.astype(f32)).reshape(b, t, SSM_D_INNER).astype(u.dtype)
    ssm_out = rms_norm(y * jax.nn.silu(z), lp['ssm_norm'])
    logf = jax.nn.log_sigmoid(f_raw.astype(f32) + lp['forget_bias'].astype(f32))
    k_all = jnp.concatenate([k_past, k.astype(k_past.dtype)], axis=1)
    v_all = jnp.concatenate([v_past, v.astype(v_past.dtype)], axis=1)
    c_all = jnp.cumsum(jnp.concatenate([lf_past.astype(f32), logf], axis=1), axis=1)
    n_past = k_past.shape[1]
    att = fox_attention(q, k_all, v_all, c_all[:, n_past:], c_all, n_past + jnp.arange(t), pad_left)
    merged = (gates[:, :, 0] * (pool_out @ lp['w_br_pool'])
              + gates[:, :, 1] * (ssm_out @ lp['w_br_ssm'])
              + gates[:, :, 2] * (att.reshape(b, t, ATT_DIM).astype(u.dtype) @ lp['w_br_attn']))
    out = merged @ lp['w_out']
    return out, new_pool, new_conv, h_new, k, v, logf.astype(lf_past.dtype)


def decoder_layer(x, lp, pos0, states, pad_left):
    x = x + 0.5 * swiglu(rms_norm(x, lp['norm_ffn1']), lp['ffn1_w_gu'], lp['ffn1_w_down'])
    mix, *new = hybrid_mixer(rms_norm(x, lp['norm_mix']), lp, pos0, *states, pad_left)
    x = x + mix
    x = x + 0.5 * swiglu(rms_norm(x, lp['norm_ffn2']), lp['ffn2_w_gu'], lp['ffn2_w_down'])
    return x, new


def setup_inputs(seed: int = 0) -> dict:
    key = jax.random.key(seed)
    keys = iter(jax.random.split(key, 40))
    f32 = jnp.float32
    n_pages = PAST_LEN // PAGE_SIZE
    n_used = DEC_BATCH * n_pages
    n_pool = n_used + max(1, n_used // 4)

    def normal(shape, scale=1.0):
        return scale * jax.random.normal(next(keys), shape, f32)

    def gain(shape):
        return 1.0 + 0.02 * jax.random.normal(next(keys), shape, f32)

    def uniform(shape, lo, hi):
        return jax.random.uniform(next(keys), shape, f32, lo, hi)

    x_prompt = normal((BATCH, SEQ, D_MODEL))
    x_sample = normal((DEC_BATCH, DEC_SEQ, D_MODEL))
    cache_k = normal((n_pool, DEPTH, PAGE_SIZE, ATT_HEADS, ATT_HEAD_DIM))
    cache_v = normal((n_pool, DEPTH, PAGE_SIZE, ATT_HEADS, ATT_HEAD_DIM))
    cache_logf = jax.nn.log_sigmoid(normal((n_pool, DEPTH, PAGE_SIZE, ATT_HEADS)) + 3.0)
    page_table = jax.random.permutation(next(keys), n_pool)[:n_used].reshape(DEC_BATCH, n_pages).astype(jnp.int32)
    state_ssm = normal((DEC_BATCH, DEPTH, SSM_HEADS, SSM_HEAD_DIM, SSM_STATE), 0.1)
    state_conv = normal((DEC_BATCH, DEPTH, SSM_CONV - 1, SSM_CONV_DIM))
    state_pool = normal((DEC_BATCH, DEPTH, POOL_HIST, POOL_DIM))
    meta_tokens = normal((N_META, D_MODEL))
    norm_ffn1 = gain((DEPTH, D_MODEL))
    ffn1_w_gu = normal((DEPTH, D_MODEL, 2 * D_FF), D_MODEL ** -0.5)
    ffn1_w_down = normal((DEPTH, D_FF, D_MODEL), D_FF ** -0.5)
    norm_mix = gain((DEPTH, D_MODEL))
    w_in = normal((DEPTH, D_MODEL, IN_DIM), D_MODEL ** -0.5)
    pool_w = normal((DEPTH, POOL_GROUPS, POOL_GROUP_DIM, POOL_GROUP_DIM), POOL_GROUP_DIM ** -0.5)
    pool_scale = gain((DEPTH, POOL_DIM))
    conv_w = normal((DEPTH, SSM_CONV, SSM_CONV_DIM), SSM_CONV ** -0.5)
    conv_b = normal((DEPTH, SSM_CONV_DIM), 0.01)
    dt0 = jnp.exp(uniform((DEPTH, SSM_HEADS), math.log(DT_MIN), math.log(DT_MAX)))
    dt_bias = dt0 + jnp.log(-jnp.expm1(-dt0))
    a_log = jnp.log(uniform((DEPTH, SSM_HEADS), 1.0, 16.0))
    d_skip = 1.0 + normal((DEPTH, SSM_HEADS), 0.1)
    ssm_norm = gain((DEPTH, SSM_D_INNER))
    forget_bias = uniform((DEPTH, ATT_HEADS), 1.0, 5.0)
    w_br_pool = normal((DEPTH, POOL_DIM, D_MODEL), POOL_DIM ** -0.5)
    w_br_ssm = normal((DEPTH, SSM_D_INNER, D_MODEL), SSM_D_INNER ** -0.5)
    w_br_attn = normal((DEPTH, ATT_DIM, D_MODEL), ATT_DIM ** -0.5)
    w_out = normal((DEPTH, D_MODEL, D_MODEL), D_MODEL ** -0.5)
    norm_ffn2 = gain((DEPTH, D_MODEL))
    ffn2_w_gu = normal((DEPTH, D_MODEL, 2 * D_FF), D_MODEL ** -0.5)
    ffn2_w_down = normal((DEPTH, D_FF, D_MODEL), D_FF ** -0.5)
    final_norm = gain((D_MODEL,))
    return {'x_prompt': x_prompt, 'x_sample': x_sample, 'cache_k': cache_k, 'cache_v': cache_v,
            'cache_logf': cache_logf, 'page_table': page_table, 'state_ssm': state_ssm,
            'state_conv': state_conv, 'state_pool': state_pool, 'meta_tokens': meta_tokens,
            'norm_ffn1': norm_ffn1, 'ffn1_w_gu': ffn1_w_gu, 'ffn1_w_down': ffn1_w_down,
            'norm_mix': norm_mix, 'w_in': w_in, 'pool_w': pool_w, 'pool_scale': pool_scale,
            'conv_w': conv_w, 'conv_b': conv_b, 'dt_bias': dt_bias, 'a_log': a_log, 'd_skip': d_skip,
            'ssm_norm': ssm_norm, 'forget_bias': forget_bias, 'w_br_pool': w_br_pool,
            'w_br_ssm': w_br_ssm, 'w_br_attn': w_br_attn, 'w_out': w_out, 'norm_ffn2': norm_ffn2,
            'ffn2_w_gu': ffn2_w_gu, 'ffn2_w_down': ffn2_w_down, 'final_norm': final_norm}


def reference(x_prompt, x_sample, cache_k, cache_v, cache_logf, page_table, state_ssm, state_conv, state_pool,
              meta_tokens, norm_ffn1, ffn1_w_gu, ffn1_w_down, norm_mix, w_in, pool_w, pool_scale,
              conv_w, conv_b, dt_bias, a_log, d_skip, ssm_norm, forget_bias, w_br_pool, w_br_ssm,
              w_br_attn, w_out, norm_ffn2, ffn2_w_gu, ffn2_w_down, final_norm):
    n_pages = PAST_LEN // PAGE_SIZE
    n_past = n_pages * PAGE_SIZE
    dtype = x_prompt.dtype
    xp = jnp.concatenate([jnp.broadcast_to(meta_tokens.astype(dtype)[None], (BATCH, N_META, D_MODEL)), x_prompt], axis=1)
    xs = x_sample
    kp_l, vp_l, lfp_l, hp_l, cp_l, pp_l = [], [], [], [], [], []
    ks_l, vs_l, lfs_l, hs_l, cs_l, ps_l = [], [], [], [], [], []
    for i in range(DEPTH):
        lp = dict(norm_ffn1=norm_ffn1[i], ffn1_w_gu=ffn1_w_gu[i], ffn1_w_down=ffn1_w_down[i],
                  norm_mix=norm_mix[i], w_in=w_in[i], pool_w=pool_w[i], pool_scale=pool_scale[i],
                  conv_w=conv_w[i], conv_b=conv_b[i], dt_bias=dt_bias[i], a_log=a_log[i], d_skip=d_skip[i],
                  ssm_norm=ssm_norm[i], forget_bias=forget_bias[i], w_br_pool=w_br_pool[i],
                  w_br_ssm=w_br_ssm[i], w_br_attn=w_br_attn[i], w_out=w_out[i], norm_ffn2=norm_ffn2[i],
                  ffn2_w_gu=ffn2_w_gu[i], ffn2_w_down=ffn2_w_down[i])
        prompt_states = (jnp.zeros((BATCH, POOL_HIST, POOL_DIM), dtype),
                         jnp.zeros((BATCH, SSM_CONV - 1, SSM_CONV_DIM), dtype),
                         jnp.zeros((BATCH, SSM_HEADS, SSM_HEAD_DIM, SSM_STATE), jnp.float32),
                         jnp.zeros((BATCH, 0, ATT_HEADS, ATT_HEAD_DIM), dtype),
                         jnp.zeros((BATCH, 0, ATT_HEADS, ATT_HEAD_DIM), dtype),
                         jnp.zeros((BATCH, 0, ATT_HEADS), dtype))
        xp, (pool_p, conv_p, h_p, k_p, v_p, lf_p) = decoder_layer(xp, lp, 0, prompt_states, True)
        k_past = cache_k[page_table, i].reshape(DEC_BATCH, n_past, ATT_HEADS, ATT_HEAD_DIM)
        v_past = cache_v[page_table, i].reshape(DEC_BATCH, n_past, ATT_HEADS, ATT_HEAD_DIM)
        lf_past = cache_logf[page_table, i].reshape(DEC_BATCH, n_past, ATT_HEADS)
        sample_states = (state_pool[:, i], state_conv[:, i], state_ssm[:, i], k_past, v_past, lf_past)
        xs, (pool_s, conv_s, h_s, k_s, v_s, lf_s) = decoder_layer(xs, lp, PAST_LEN, sample_states, False)
        kp_l.append(k_p); vp_l.append(v_p); lfp_l.append(lf_p); hp_l.append(h_p); cp_l.append(conv_p); pp_l.append(pool_p)
        ks_l.append(k_s); vs_l.append(v_s); lfs_l.append(lf_s); hs_l.append(h_s); cs_l.append(conv_s); ps_l.append(pool_s)
    y_prompt = rms_norm(xp, final_norm)[:, N_META:]
    y_sample = rms_norm(xs, final_norm)
    return (y_prompt, y_sample,
            jnp.stack(kp_l, axis=1), jnp.stack(vp_l, axis=1), jnp.stack(lfp_l, axis=1),
            jnp.stack(ks_l, axis=1), jnp.stack(vs_l, axis=1), jnp.stack(lfs_l, axis=1),
            jnp.stack(hp_l, axis=1), jnp.stack(hs_l, axis=1),
            jnp.stack(cp_l, axis=1), jnp.stack(cs_l, axis=1),
            jnp.stack(pp_l, axis=1), jnp.stack(ps_l, axis=1))
```

```python
import functools
import math

import jax
import jax.numpy as jnp
from jax import lax
from jax.experimental import pallas as pl
from jax.experimental.pallas import tpu as pltpu

F32 = jnp.float32
BF16 = jnp.bfloat16
HIGHEST = lax.Precision.HIGHEST

NORM_EPS = 1e-6
NEG_INF = -1e30
POOL_WINDOWS = (2, 4, 8, 16)
POOL_HIST_ROWS = 16
CONV_TAPS = 4
CONV_HIST_ROWS = 8
SEQ_TILE = 128
DEC_ROWS = 16
LANES = 128
SMALL_W = 512
DT_LANE = 0
F_LANE = 128
VMEM_LIMIT_BYTES = 56 * 1024 * 1024


def _cparams(n_axes):
    return pltpu.CompilerParams(dimension_semantics=("arbitrary",) * n_axes,
                                vmem_limit_bytes=VMEM_LIMIT_BYTES)


def _pick(n, cands):
    for c in cands:
        if n % c == 0:
            return c
    return n


def _sigmoid(x):
    return 1.0 / (1.0 + jnp.exp(-x))


def _softplus(x):
    return jnp.maximum(x, 0.0) + jnp.log(1.0 + jnp.exp(-jnp.abs(x)))


def _dot(a, b, precision=None):
    return jnp.dot(a, b, preferred_element_type=F32, precision=precision)


def _dot_nt(a, b):
    return lax.dot_general(a, b, (((1,), (1,)), ((), ())), preferred_element_type=F32)


def _rms_to_scratch(x_ref, g_ref, xn_ref):
    x = x_ref[...]
    ms = jnp.mean(x * x, axis=-1, keepdims=True)
    xn_ref[...] = ((x * lax.rsqrt(ms + NORM_EPS)) * g_ref[...]).astype(BF16)


def _norm_swiglu_kernel(x_ref, g_ref, wg_ref, wu_ref, o_ref, xn_ref):
    @pl.when(pl.program_id(1) == 0)
    def _():
        _rms_to_scratch(x_ref, g_ref, xn_ref)

    xn = xn_ref[...]
    gate = _dot(xn, wg_ref[...])
    up = _dot(xn, wu_ref[...])
    o_ref[...] = ((gate * _sigmoid(gate)) * up).astype(o_ref.dtype)


def norm_swiglu(x, g, w_gu):
    m, d = x.shape
    f = w_gu.shape[1] // 2
    tm = _pick(m, (1056, 768, 704, 528, 512, 384, 256, 128))
    tn = _pick(f, (512, 256, 128))
    nj = f // tn
    return pl.pallas_call(
        _norm_swiglu_kernel,
        grid=(m // tm, nj),
        in_specs=[pl.BlockSpec((tm, d), lambda i, j: (i, 0)),
                  pl.BlockSpec((1, d), lambda i, j: (0, 0)),
                  pl.BlockSpec((d, tn), lambda i, j: (0, j)),
                  pl.BlockSpec((d, tn), lambda i, j: (0, j + nj))],
        out_specs=pl.BlockSpec((tm, tn), lambda i, j: (i, j)),
        out_shape=jax.ShapeDtypeStruct((m, f), BF16),
        scratch_shapes=[pltpu.VMEM((tm, d), BF16)],
        compiler_params=_cparams(2),
        name="norm_swiglu",
    )(x, g.reshape(1, d), w_gu, w_gu)


def _matmul_residual_kernel(h_ref, w_ref, x_ref, o_ref, *, scale):
    o_ref[...] = x_ref[...] + scale * _dot(h_ref[...], w_ref[...])


def matmul_residual(h, w, x, scale):
    m, k = h.shape
    n = w.shape[1]
    tm = _pick(m, (1056, 768, 704, 528, 512, 384, 256, 128))
    tn = _pick(n, (512, 256, 128))
    return pl.pallas_call(
        functools.partial(_matmul_residual_kernel, scale=scale),
        grid=(m // tm, n // tn),
        in_specs=[pl.BlockSpec((tm, k), lambda i, j: (i, 0)),
                  pl.BlockSpec((k, tn), lambda i, j: (0, j)),
                  pl.BlockSpec((tm, tn), lambda i, j: (i, j))],
        out_specs=pl.BlockSpec((tm, tn), lambda i, j: (i, j)),
        out_shape=jax.ShapeDtypeStruct((m, n), F32),
        compiler_params=_cparams(2),
        name="matmul_residual",
    )(h, w, x)


def _norm_matmul_kernel(x_ref, g_ref, w_ref, o_ref, xn_ref):
    @pl.when(pl.program_id(1) == 0)
    def _():
        _rms_to_scratch(x_ref, g_ref, xn_ref)

    o_ref[...] = _dot(xn_ref[...], w_ref[...])


def norm_matmul(x, g, w):
    m, d = x.shape
    n = w.shape[1]
    tm = _pick(m, (1056, 768, 704, 528, 512, 384, 256, 128))
    tn = _pick(n, (1024, 512, 256, 128))
    return pl.pallas_call(
        _norm_matmul_kernel,
        grid=(m // tm, n // tn),
        in_specs=[pl.BlockSpec((tm, d), lambda i, j: (i, 0)),
                  pl.BlockSpec((1, d), lambda i, j: (0, 0)),
                  pl.BlockSpec((d, tn), lambda i, j: (0, j))],
        out_specs=pl.BlockSpec((tm, tn), lambda i, j: (i, j)),
        out_shape=jax.ShapeDtypeStruct((m, n), F32),
        scratch_shapes=[pltpu.VMEM((tm, d), BF16)],
        compiler_params=_cparams(2),
        name="norm_matmul",
    )(x, g.reshape(1, d), w)


def _merge_kernel(p_ref, s_ref, a_ref, g0_ref, g1_ref, g2_ref, wp_ref, ws_ref, wa_ref, o_ref):
    merged = (_sigmoid(g0_ref[...]) * _dot(p_ref[...], wp_ref[...])
              + _sigmoid(g1_ref[...]) * _dot(s_ref[...], ws_ref[...])
              + _sigmoid(g2_ref[...]) * _dot(a_ref[...], wa_ref[...]))
    o_ref[...] = merged.astype(o_ref.dtype)


def gated_merge(pool_out, ssm_out, att, proj, gate_col, w_pool, w_ssm, w_attn):
    m, kb = pool_out.shape
    d = w_pool.shape[1]
    tm = _pick(m, (1056, 768, 704, 528, 512, 384, 256, 128))
    tn = _pick(d, (512, 256, 128))
    g_blk = gate_col // tn
    nj = d // tn
    branch = pl.BlockSpec((tm, kb), lambda i, j: (i, 0))
    weight = pl.BlockSpec((kb, tn), lambda i, j: (0, j))

    def gate(r):
        return pl.BlockSpec((tm, tn), lambda i, j: (i, g_blk + r * nj + j))

    return pl.pallas_call(
        _merge_kernel,
        grid=(m // tm, nj),
        in_specs=[branch, branch, branch, gate(0), gate(1), gate(2), weight, weight, weight],
        out_specs=pl.BlockSpec((tm, tn), lambda i, j: (i, j)),
        out_shape=jax.ShapeDtypeStruct((m, d), BF16),
        compiler_params=_cparams(2),
        name="gated_merge",
    )(pool_out, ssm_out, att, proj, proj, proj, w_pool, w_ssm, w_attn)


def _prep_kernel(xp_ref, sm_ref, hist_ref, pw_ref, ps_ref, fb_ref, *refs, rows, pos0, n_valid, with_cum):
    if with_cum:
        pool_ref, logf_ref, ccol_ref, crow_ref, full_ref, carry_ref = refs
    else:
        pool_ref, logf_ref, full_ref = refs
    t = pl.program_id(1)
    hr = POOL_HIST_ROWS

    @pl.when(t == 0)
    def _():
        full_ref[0:hr, :] = hist_ref[...]
        if with_cum:
            carry_ref[...] = jnp.zeros_like(carry_ref)

    x = xp_ref[...]
    full_ref[hr:hr + rows, :] = x
    row = t * rows + lax.broadcasted_iota(jnp.int32, (rows, 1), 0)
    gd = x.shape[1] // len(POOL_WINDOWS)
    for g, w in enumerate(POOL_WINDOWS):
        cols = slice(g * gd, (g + 1) * gd)
        s = full_ref[hr:hr + rows, cols]
        for k in range(1, w):
            s = s + full_ref[hr - k:hr - k + rows, cols]
        cnt = jnp.minimum(w, pos0 + row + 1).astype(F32)
        pooled = s / cnt - x[:, cols]
        mixed = _dot(pooled.astype(BF16), pw_ref[g])
        pool_ref[:, cols] = (mixed * ps_ref[:, cols]).astype(pool_ref.dtype)
    full_ref[0:hr, :] = full_ref[rows:rows + hr, :]

    z = sm_ref[:, F_LANE:F_LANE + logf_ref.shape[1]] + fb_ref[...]
    logf = -_softplus(-z)
    logf_ref[...] = logf
    if with_cum:
        lf = jnp.where(row < n_valid, logf, 0.0)
        r_i = lax.broadcasted_iota(jnp.int32, (rows, rows), 0)
        c_i = lax.broadcasted_iota(jnp.int32, (rows, rows), 1)
        tril = (r_i >= c_i).astype(F32)
        cs = _dot(tril, lf, HIGHEST) + carry_ref[...]
        ccol_ref[...] = cs
        carry_ref[...] = cs[rows - 1:rows, :]
        nh = cs.shape[1]
        wide = jnp.concatenate([cs, jnp.zeros((rows, LANES - nh), F32)], axis=1)
        crow_ref[...] = wide.T[0:nh, :]


def pool_and_forget(proj, cols, hist, pool_w, pool_scale, forget_bias, *, nb, rows, pos0, n_valid, with_cum):
    mtot = proj.shape[0]
    nt = mtot // (nb * rows)
    pd = pool_w.shape[0] * pool_w.shape[1]
    nh = forget_bias.shape[0]
    xp_blk = cols["pool"] // pd
    sm_blk = cols["small"] // SMALL_W
    in_specs = [pl.BlockSpec((rows, pd), lambda b, t: (b * nt + t, xp_blk)),
                pl.BlockSpec((rows, SMALL_W), lambda b, t: (b * nt + t, sm_blk)),
                pl.BlockSpec((None, POOL_HIST_ROWS, pd), lambda b, t: (b, 0, 0)),
                pl.BlockSpec(pool_w.shape, lambda b, t: (0, 0, 0)),
                pl.BlockSpec((1, pd), lambda b, t: (0, 0)),
                pl.BlockSpec((1, nh), lambda b, t: (0, 0))]
    out_specs = [pl.BlockSpec((rows, pd), lambda b, t: (b * nt + t, 0)),
                 pl.BlockSpec((rows, nh), lambda b, t: (b * nt + t, 0))]
    out_shape = [jax.ShapeDtypeStruct((mtot, pd), BF16), jax.ShapeDtypeStruct((mtot, nh), F32)]
    scratch = [pltpu.VMEM((POOL_HIST_ROWS + rows, pd), F32)]
    if with_cum:
        out_specs += [pl.BlockSpec((rows, nh), lambda b, t: (b * nt + t, 0)),
                      pl.BlockSpec((None, nh, rows), lambda b, t: (b, 0, t))]
        out_shape += [jax.ShapeDtypeStruct((mtot, nh), F32), jax.ShapeDtypeStruct((nb, nh, nt * rows), F32)]
        scratch += [pltpu.VMEM((1, nh), F32)]
    return pl.pallas_call(
        functools.partial(_prep_kernel, rows=rows, pos0=pos0, n_valid=n_valid, with_cum=with_cum),
        grid=(nb, nt),
        in_specs=in_specs, out_specs=out_specs, out_shape=out_shape, scratch_shapes=scratch,
        compiler_params=_cparams(2),
        name="pool_and_forget",
    )(proj, proj, hist, pool_w, pool_scale.reshape(1, pd), forget_bias.reshape(1, nh))


def _ssd_kernel(xz_ref, z_ref, hist_ref, h0_ref, cw_ref, cb_ref, dtb_ref, alog_ref, dskip_ref, nw_ref,
                ex_ref, ext_ref, o_ref, h_ref, full_ref, y_ref, *, rows, n_valid, d_inner, n_state, head_dim):
    t = SEQ_TILE
    c = pl.program_id(1)
    hr = CONV_HIST_ROWS
    cdim = full_ref.shape[1]
    nheads = d_inner // head_dim
    ngroups = (cdim - d_inner) // (2 * n_state)
    pair_w = 2 * head_dim
    pairs_per_group = d_inner // pair_w // ngroups

    @pl.when(c == 0)
    def _():
        full_ref[0:hr, :] = hist_ref[...]
        h_ref[...] = h0_ref[...]

    full_ref[hr:hr + rows, :] = xz_ref[:, 0:cdim]
    if rows < t:
        full_ref[hr + rows:hr + t, :] = jnp.zeros((t - rows, cdim), F32)
    conv = cb_ref[...]
    for k in range(CONV_TAPS):
        off = hr - (CONV_TAPS - 1) + k
        conv = conv + full_ref[off:off + t, :] * cw_ref[k:k + 1, :]
    act = conv * _sigmoid(conv)
    full_ref[0:hr, :] = full_ref[t:t + hr, :]

    dt_raw = xz_ref[:, cdim + DT_LANE:cdim + DT_LANE + nheads]
    if rows < t:
        dt_raw = jnp.concatenate([dt_raw, jnp.zeros((t - rows, nheads), F32)], axis=0)
    row = c * t + lax.broadcasted_iota(jnp.int32, (t, 1), 0)
    dt = jnp.where(row < n_valid, _softplus(dt_raw + dtb_ref[...]), 0.0)
    a = -jnp.exp(alog_ref[...])
    r_i = lax.broadcasted_iota(jnp.int32, (t, t), 0)
    c_i = lax.broadcasted_iota(jnp.int32, (t, t), 1)
    causal = r_i >= c_i
    acum = _dot(causal.astype(F32), dt * a, HIGHEST)
    lane_pad = jnp.zeros((t, LANES - nheads), F32)
    acum_row = jnp.concatenate([acum, lane_pad], axis=1).T[0:nheads, :]
    dt_row = jnp.concatenate([dt, lane_pad], axis=1).T[0:nheads, :]
    a_last = acum[t - 1:t, :]
    decay_in = jnp.exp(acum)
    to_end = jnp.exp(a_last - acum) * dt
    wide = _dot(jnp.concatenate([decay_in, to_end], axis=0), ex_ref[...], HIGHEST)
    decay_in_x = wide[0:t, :]
    to_end_x = wide[t:2 * t, :]
    chunk_decay = jnp.broadcast_to(jnp.exp(acum_row[:, t - 1:t]), (nheads, n_state))
    chunk_decay_x = _dot(ext_ref[...], chunk_decay, HIGHEST)

    xs = act[:, 0:d_inner]
    xs_b = xs.astype(BF16)
    lane = lax.broadcasted_iota(jnp.int32, (t, pair_w), 1)
    first = lane < head_dim
    for g in range(ngroups):
        bm = act[:, d_inner + g * n_state:d_inner + (g + 1) * n_state].astype(BF16)
        cm = act[:, d_inner + (ngroups + g) * n_state:d_inner + (ngroups + g + 1) * n_state].astype(BF16)
        cb = _dot_nt(cm, bm)
        for j in range(pairs_per_group):
            p = g * pairs_per_group + j
            cols = slice(p * pair_w, (p + 1) * pair_w)
            wts = []
            for h in (2 * p, 2 * p + 1):
                seg = acum[:, h:h + 1] - acum_row[h:h + 1, :]
                decay = jnp.exp(jnp.where(causal, seg, -jnp.inf))
                wts.append((cb * decay) * dt_row[h:h + 1, :])
            lhs = jnp.concatenate(wts, axis=1).astype(BF16)
            xp = xs_b[:, cols]
            zero = jnp.zeros_like(xp)
            rhs = jnp.concatenate([jnp.where(first, xp, zero), jnp.where(first, zero, xp)], axis=0)
            y = _dot(lhs, rhs)
            hp = h_ref[p * pair_w:(p + 1) * pair_w, :]
            y = y + _dot_nt(cm, hp.astype(BF16)) * decay_in_x[:, cols]
            y_ref[:, cols] = y + dskip_ref[:, cols] * xs[:, cols]
            upd = (xs[:, cols] * to_end_x[:, cols]).T.astype(BF16)
            h_ref[p * pair_w:(p + 1) * pair_w, :] = chunk_decay_x[p * pair_w:(p + 1) * pair_w, :] * hp + _dot(upd, bm)

    zz = z_ref[...]
    if rows < t:
        zz = jnp.concatenate([zz, jnp.zeros((t - rows, d_inner), F32)], axis=0)
    gated = y_ref[...] * (zz * _sigmoid(zz))
    ms = jnp.mean(gated * gated, axis=-1, keepdims=True)
    out = (gated * lax.rsqrt(ms + NORM_EPS)) * nw_ref[...]
    o_ref[...] = out[0:rows, :].astype(o_ref.dtype)


def ssd_mixer(proj, cols, hist, h0, conv_w, conv_b, dt_bias, a_log, d_skip, ssm_norm, *, nb, rows, n_valid):
    mtot = proj.shape[0]
    nt = mtot // (nb * rows)
    cdim = conv_w.shape[1]
    nheads = dt_bias.shape[0]
    d_inner, n_state = h0.shape[1], h0.shape[2]
    head_dim = d_inner // nheads
    xz_w = cdim + SMALL_W
    expand = jnp.repeat(jnp.eye(nheads, dtype=F32), head_dim, axis=1)
    const2 = lambda b, t: (0, 0)
    return pl.pallas_call(
        functools.partial(_ssd_kernel, rows=rows, n_valid=n_valid, d_inner=d_inner, n_state=n_state,
                          head_dim=head_dim),
        grid=(nb, nt),
        in_specs=[pl.BlockSpec((rows, xz_w), lambda b, t: (b * nt + t, cols["xbc"] // xz_w)),
                  pl.BlockSpec((rows, d_inner), lambda b, t: (b * nt + t, cols["z"] // d_inner)),
                  pl.BlockSpec((None, CONV_HIST_ROWS, cdim), lambda b, t: (b, 0, 0)),
                  pl.BlockSpec((None, d_inner, n_state), lambda b, t: (b, 0, 0)),
                  pl.BlockSpec((CONV_TAPS, cdim), const2),
                  pl.BlockSpec((1, cdim), const2),
                  pl.BlockSpec((1, nheads), const2),
                  pl.BlockSpec((1, nheads), const2),
                  pl.BlockSpec((1, d_inner), const2),
                  pl.BlockSpec((1, d_inner), const2),
                  pl.BlockSpec((nheads, d_inner), const2),
                  pl.BlockSpec((d_inner, nheads), const2)],
        out_specs=[pl.BlockSpec((rows, d_inner), lambda b, t: (b * nt + t, 0)),
                   pl.BlockSpec((None, d_inner, n_state), lambda b, t: (b, 0, 0))],
        out_shape=[jax.ShapeDtypeStruct((mtot, d_inner), BF16),
                   jax.ShapeDtypeStruct((nb, d_inner, n_state), F32)],
        scratch_shapes=[pltpu.VMEM((CONV_HIST_ROWS + SEQ_TILE, cdim), F32),
                        pltpu.VMEM((SEQ_TILE, d_inner), F32)],
        compiler_params=_cparams(2),
        name="ssd_mixer",
    )(proj, proj, hist, h0, conv_w, conv_b.reshape(1, cdim), dt_bias.reshape(1, nheads),
      a_log.reshape(1, nheads), jnp.repeat(d_skip, head_dim).reshape(1, d_inner),
      ssm_norm.reshape(1, d_inner), expand, expand.T)


def _attn_kernel(q_ref, k_ref, v_ref, ccol_ref, crow_ref, o_ref, m_ref, l_ref, acc_ref, *, tq, tk, head_dim):
    pair = pl.program_id(1)
    qi = pl.program_id(2)
    nh = ccol_ref.shape[1]
    scale = head_dim ** -0.5
    lane_q = lax.broadcasted_iota(jnp.int32, (tq, 2 * head_dim), 1)
    lane_k = lax.broadcasted_iota(jnp.int32, (tk, 2 * head_dim), 1)
    q = q_ref[...] * scale
    zq = jnp.zeros_like(q)
    q_heads = (jnp.where(lane_q < head_dim, q, zq).astype(BF16), jnp.where(lane_q < head_dim, zq, q).astype(BF16))
    head_id = lax.broadcasted_iota(jnp.int32, (1, nh), 1)
    ccol = ccol_ref[...]
    c_q = [jnp.sum(jnp.where(head_id == 2 * pair + e, ccol, 0.0), axis=1, keepdims=True) for e in range(2)]

    m_ref[...] = jnp.full(m_ref.shape, NEG_INF, F32)
    l_ref[...] = jnp.zeros_like(l_ref)
    acc_ref[...] = jnp.zeros_like(acc_ref)
    q_pos = qi * tq + lax.broadcasted_iota(jnp.int32, (tq, 1), 0)

    def step(kt, masked):
        ks = pl.multiple_of(kt * tk, tk)
        kb = k_ref[pl.ds(ks, tk), :].astype(BF16)
        vb = v_ref[pl.ds(ks, tk), :]
        zv = jnp.zeros_like(vb)
        v_stack = jnp.concatenate([jnp.where(lane_k < head_dim, vb, zv), jnp.where(lane_k < head_dim, zv, vb)],
                                  axis=0).astype(BF16)
        probs, alphas = [], []
        for e in range(2):
            c_k = crow_ref[pl.ds(2 * pair + e, 1), pl.ds(ks, tk)]
            s = _dot_nt(q_heads[e], kb) + (c_q[e] - c_k)
            if masked:
                k_pos = ks + lax.broadcasted_iota(jnp.int32, (1, tk), 1)
                s = jnp.where(k_pos <= q_pos, s, NEG_INF)
            m_old = m_ref[e]
            m_new = jnp.maximum(m_old, jnp.max(s, axis=1, keepdims=True))
            alpha = jnp.exp(m_old - m_new)
            p = jnp.exp(s - m_new)
            l_ref[e] = alpha * l_ref[e] + jnp.sum(p, axis=1, keepdims=True)
            m_ref[e] = m_new
            probs.append(p.astype(BF16))
            alphas.append(alpha)
        alpha_x = jnp.where(lane_q < head_dim, alphas[0], alphas[1])
        acc_ref[...] = alpha_x * acc_ref[...] + _dot(jnp.concatenate(probs, axis=1), v_stack)

    n_full = (qi * tq + 1) // tk
    n_all = ((qi + 1) * tq + tk - 1) // tk

    def full_body(kt, carry):
        step(kt, False)
        return carry

    def diag_body(kt, carry):
        step(kt, True)
        return carry

    lax.fori_loop(0, n_full, full_body, 0)
    lax.fori_loop(n_full, n_all, diag_body, 0)
    l_x = jnp.where(lane_q < head_dim, l_ref[0], l_ref[1])
    o_ref[...] = (acc_ref[...] / l_x).astype(o_ref.dtype)


def prompt_attention(proj, cols, ccol, crow, *, nb, lp, head_dim):
    mtot = proj.shape[0]
    nh = ccol.shape[1]
    pw = 2 * head_dim
    tq = _pick(lp, (384, 256, 128))
    tk = tq
    nq = lp // tq
    return pl.pallas_call(
        functools.partial(_attn_kernel, tq=tq, tk=tk, head_dim=head_dim),
        grid=(nb, nh // 2, nq),
        in_specs=[pl.BlockSpec((tq, pw), lambda b, p, i: (b * nq + i, cols["q"] // pw + p)),
                  pl.BlockSpec((lp, pw), lambda b, p, i: (b, cols["k"] // pw + p)),
                  pl.BlockSpec((lp, pw), lambda b, p, i: (b, cols["v"] // pw + p)),
                  pl.BlockSpec((tq, nh), lambda b, p, i: (b * nq + i, 0)),
                  pl.BlockSpec((None, nh, lp), lambda b, p, i: (b, 0, 0))],
        out_specs=pl.BlockSpec((tq, pw), lambda b, p, i: (b * nq + i, p)),
        out_shape=jax.ShapeDtypeStruct((mtot, nh * head_dim), BF16),
        scratch_shapes=[pltpu.VMEM((2, tq, 1), F32), pltpu.VMEM((2, tq, 1), F32), pltpu.VMEM((tq, pw), F32)],
        compiler_params=_cparams(3),
        name="prompt_attention",
    )(proj, proj, proj, ccol, crow)


def _decode_attn_kernel(pt_ref, q_ref, kn_ref, vn_ref, lfn_ref, ex_ref, *refs, pages, head_dim):
    k_refs = refs[0:pages]
    v_refs = refs[pages:2 * pages]
    lf_refs = refs[2 * pages:3 * pages]
    o_ref, qbd_ref, m_ref, l_ref, carry_ref, acc_ref = refs[3 * pages:]
    s_id = pl.program_id(1)
    nh = lfn_ref.shape[1]
    hd_all = nh * head_dim
    psz = k_refs[0].shape[0]
    ex = ex_ref[...]
    sub = acc_ref.shape[0]

    @pl.when(s_id == 0)
    def _():
        q = q_ref[0:1, :] * head_dim ** -0.5
        qbd = jnp.where(ex > 0.5, jnp.broadcast_to(q, (nh, hd_all)), 0.0).astype(BF16)
        qbd_ref[...] = qbd
        s_new = _dot_nt(kn_ref[...].astype(BF16), qbd)[0:1, :]
        m_ref[...] = s_new
        l_ref[...] = jnp.ones_like(l_ref)
        carry_ref[...] = lfn_ref[0:1, :]
        r_i = lax.broadcasted_iota(jnp.int32, (sub, hd_all), 0)
        acc_ref[...] = jnp.where(r_i == 0, vn_ref[0:sub, :], 0.0)

    r_i = lax.broadcasted_iota(jnp.int32, (psz, psz), 0)
    c_i = lax.broadcasted_iota(jnp.int32, (psz, psz), 1)
    later = (c_i > r_i).astype(F32)
    qbd = qbd_ref[...]
    for i in range(pages):
        lf = lf_refs[i][...]
        carry = carry_ref[...]
        bias = _dot(later, lf, HIGHEST) + carry
        carry_ref[...] = carry + jnp.sum(lf, axis=0, keepdims=True)
        s = _dot_nt(k_refs[i][...].astype(BF16), qbd) + bias
        m_old = m_ref[...]
        m_new = jnp.maximum(m_old, jnp.max(s, axis=0, keepdims=True))
        alpha = jnp.exp(m_old - m_new)
        p = jnp.exp(s - m_new)
        l_ref[...] = alpha * l_ref[...] + jnp.sum(p, axis=0, keepdims=True)
        m_ref[...] = m_new
        p_x = _dot(p.astype(BF16), ex.astype(BF16))
        alpha_x = _dot(jnp.broadcast_to(alpha, (sub, nh)), ex, HIGHEST)
        pv = (p_x * v_refs[i][...]).reshape(psz // sub, sub, hd_all).sum(axis=0)
        acc_ref[...] = alpha_x * acc_ref[...] + pv

    @pl.when(s_id == pl.num_programs(1) - 1)
    def _():
        l_x = _dot(jnp.broadcast_to(l_ref[...], (sub, nh)), ex, HIGHEST)[0:1, :]
        out = jnp.sum(acc_ref[...], axis=0, keepdims=True) / l_x
        r_o = lax.broadcasted_iota(jnp.int32, o_ref.shape, 0)
        o_ref[...] = jnp.where(r_o == 0, jnp.broadcast_to(out, o_ref.shape), 0.0).astype(o_ref.dtype)


def decode_attention(proj, cols, logf_new, cache_k, cache_v, cache_logf, page_table, layer, *, head_dim):
    nb, n_pages = page_table.shape
    n_pool, depth, psz, nh = cache_logf.shape
    hd_all = nh * head_dim
    pages = _pick(n_pages, (4, 2, 1))
    ck = cache_k.reshape(n_pool, depth, psz, hd_all)
    cv = cache_v.reshape(n_pool, depth, psz, hd_all)
    expand = jnp.repeat(jnp.eye(nh, dtype=F32), head_dim, axis=1)

    def page(i, width):
        def index(b, s, pt):
            return (pt[b, n_pages - 1 - (s * pages + i)], layer, 0, 0)
        return pl.BlockSpec((None, None, psz, width), index)

    def row(col_blk, width):
        return pl.BlockSpec((DEC_ROWS, width), lambda b, s, pt: (b, col_blk))

    grid_spec = pltpu.PrefetchScalarGridSpec(
        num_scalar_prefetch=1,
        grid=(nb, n_pages // pages),
        in_specs=([row(cols["q"] // hd_all, hd_all), row(cols["k"] // hd_all, hd_all),
                   row(cols["v"] // hd_all, hd_all), row(0, nh),
                   pl.BlockSpec((nh, hd_all), lambda b, s, pt: (0, 0))]
                  + [page(i, hd_all) for i in range(pages)]
                  + [page(i, hd_all) for i in range(pages)]
                  + [page(i, nh) for i in range(pages)]),
        out_specs=pl.BlockSpec((DEC_ROWS, hd_all), lambda b, s, pt: (b, 0)),
        scratch_shapes=[pltpu.VMEM((nh, hd_all), BF16), pltpu.VMEM((1, nh), F32), pltpu.VMEM((1, nh), F32),
                        pltpu.VMEM((1, nh), F32), pltpu.VMEM((8, hd_all), F32)],
    )
    return pl.pallas_call(
        functools.partial(_decode_attn_kernel, pages=pages, head_dim=head_dim),
        grid_spec=grid_spec,
        out_shape=jax.ShapeDtypeStruct((nb * DEC_ROWS, hd_all), BF16),
        compiler_params=_cparams(2),
        name="decode_attention",
    )(page_table, proj, proj, proj, logf_new, expand, *([ck] * pages), *([cv] * pages), *([cache_logf] * pages))


def _final_norm_shift_kernel(a_ref, b_ref, g_ref, o_ref, *, shift):
    x = jnp.concatenate([a_ref[shift:, :], b_ref[0:shift, :]], axis=0)
    ms = jnp.mean(x * x, axis=-1, keepdims=True)
    o_ref[...] = (x * lax.rsqrt(ms + NORM_EPS)) * g_ref[...]


def final_norm_prompt(x, g, *, nb, lp, seq, shift):
    d = x.shape[1]
    tr = SEQ_TILE
    nt = seq // tr
    nl = lp // tr
    return pl.pallas_call(
        functools.partial(_final_norm_shift_kernel, shift=shift),
        grid=(nb, nt),
        in_specs=[pl.BlockSpec((tr, d), lambda b, t: (b * nl + t, 0)),
                  pl.BlockSpec((tr, d), lambda b, t: (b * nl + t + 1, 0)),
                  pl.BlockSpec((1, d), lambda b, t: (0, 0))],
        out_specs=pl.BlockSpec((None, tr, d), lambda b, t: (b, t, 0)),
        out_shape=jax.ShapeDtypeStruct((nb, seq, d), F32),
        compiler_params=_cparams(2),
        name="final_norm_prompt",
    )(x, x, g.reshape(1, d))


def _final_norm_kernel(x_ref, g_ref, o_ref):
    x = x_ref[...]
    ms = jnp.mean(x * x, axis=-1, keepdims=True)
    o_ref[...] = (x * lax.rsqrt(ms + NORM_EPS)) * g_ref[...]


def final_norm(x, g):
    m, d = x.shape
    return pl.pallas_call(
        _final_norm_kernel,
        grid=(1,),
        in_specs=[pl.BlockSpec((m, d), lambda i: (0, 0)), pl.BlockSpec((1, d), lambda i: (0, 0))],
        out_specs=pl.BlockSpec((m, d), lambda i: (0, 0)),
        out_shape=jax.ShapeDtypeStruct((m, d), F32),
        compiler_params=_cparams(1),
        name="final_norm",
    )(x, g.reshape(1, d))


def _pack_w_in(w_in, dims):
    d = w_in.shape[0]
    pd, di, cd, nh_s, ad, nh_a, dm = dims
    off = {}
    o = 0
    for name, width in (("pool", pd), ("z", di), ("xbc", cd), ("dt", nh_s), ("q", ad), ("k", ad), ("v", ad),
                        ("f", nh_a), ("gate", w_in.shape[1] - (pd + di + cd + nh_s + 3 * ad + nh_a))):
        off[name] = (o, width)
        o += width

    def seg(name):
        s, w = off[name]
        return w_in[:, s:s + w]

    small = jnp.zeros((d, SMALL_W), w_in.dtype)
    small = small.at[:, DT_LANE:DT_LANE + nh_s].set(seg("dt")).at[:, F_LANE:F_LANE + nh_a].set(seg("f"))
    parts = [("xbc", seg("xbc")), ("small", small), ("pool", seg("pool")), ("z", seg("z")), ("q", seg("q")),
             ("k", seg("k")), ("v", seg("v")), ("gate", seg("gate"))]
    cols = {}
    o = 0
    for name, arr in parts:
        cols[name] = o
        o += arr.shape[1]
    return jnp.concatenate([a for _, a in parts], axis=1).astype(BF16), cols


def kernel(x_prompt, x_sample, cache_k, cache_v, cache_logf, page_table, state_ssm, state_conv, state_pool,
           meta_tokens, norm_ffn1, ffn1_w_gu, ffn1_w_down, norm_mix, w_in, pool_w, pool_scale, conv_w, conv_b,
           dt_bias, a_log, d_skip, ssm_norm, forget_bias, w_br_pool, w_br_ssm, w_br_attn, w_out, norm_ffn2,
           ffn2_w_gu, ffn2_w_down, final_norm_w):
    nb, seq, dm = x_prompt.shape
    ns = x_sample.shape[0]
    depth = w_in.shape[0]
    n_meta = meta_tokens.shape[0]
    nh_a, hd_a = cache_k.shape[3], cache_k.shape[4]
    ad = nh_a * hd_a
    psz = cache_k.shape[2]
    past_len = page_table.shape[1] * psz
    nh_s, hd_s, n_state = state_ssm.shape[2], state_ssm.shape[3], state_ssm.shape[4]
    di = nh_s * hd_s
    cd = state_conv.shape[3]
    conv_hist = state_conv.shape[2]
    pd = state_pool.shape[3]
    pool_hist = state_pool.shape[2]
    l = seq + n_meta
    lp = -(-l // SEQ_TILE) * SEQ_TILE
    assert x_sample.shape[1] == 1 and seq % SEQ_TILE == 0 and n_meta % 8 == 0 and n_meta < SEQ_TILE
    assert 2 * hd_a == LANES and 2 * hd_s == LANES and n_state == LANES
    assert pool_hist < POOL_HIST_ROWS and conv_hist < CONV_HIST_ROWS

    xp = jnp.concatenate([jnp.broadcast_to(meta_tokens[None], (nb, n_meta, dm)), x_prompt,
                          jnp.zeros((nb, lp - l, dm), F32)], axis=1).reshape(nb * lp, dm)
    xs = jnp.pad(x_sample, ((0, 0), (0, DEC_ROWS - 1), (0, 0))).reshape(ns * DEC_ROWS, dm)

    zero_pool_hist = jnp.zeros((nb, POOL_HIST_ROWS, pd), F32)
    zero_conv_hist = jnp.zeros((nb, CONV_HIST_ROWS, cd), F32)
    zero_state = jnp.zeros((nb, di, n_state), F32)

    outs = {k: [] for k in ("kp", "vp", "lfp", "hp", "cp", "pp", "ks", "vs", "lfs", "hs", "cs", "ps")}
    for i in range(depth):
        w_in_i, cols = _pack_w_in(w_in[i], (pd, di, cd, nh_s, ad, nh_a, dm))
        wgu1, wd1 = ffn1_w_gu[i].astype(BF16), ffn1_w_down[i].astype(BF16)
        wgu2, wd2 = ffn2_w_gu[i].astype(BF16), ffn2_w_down[i].astype(BF16)
        wbp, wbs, wba = w_br_pool[i].astype(BF16), w_br_ssm[i].astype(BF16), w_br_attn[i].astype(BF16)
        wo = w_out[i].astype(BF16)
        pw = pool_w[i].astype(BF16)

        def layer(x, mix):
            x = matmul_residual(norm_swiglu(x, norm_ffn1[i], wgu1), wd1, x, 0.5)
            proj = norm_matmul(x, norm_mix[i], w_in_i)
            pool_out, ssm_out, att, extra = mix(proj)
            merged = gated_merge(pool_out, ssm_out, att, proj, cols["gate"], wbp, wbs, wba)
            x = matmul_residual(merged, wo, x, 1.0)
            x = matmul_residual(norm_swiglu(x, norm_ffn2[i], wgu2), wd2, x, 0.5)
            return x, proj, extra

        def mix_prompt(proj):
            pool_out, logf, ccol, crow = pool_and_forget(
                proj, cols, zero_pool_hist, pw, pool_scale[i], forget_bias[i],
                nb=nb, rows=SEQ_TILE, pos0=0, n_valid=l, with_cum=True)
            ssm_out, h_new = ssd_mixer(proj, cols, zero_conv_hist, zero_state, conv_w[i], conv_b[i], dt_bias[i],
                                       a_log[i], d_skip[i], ssm_norm[i], nb=nb, rows=SEQ_TILE, n_valid=l)
            att = prompt_attention(proj, cols, ccol, crow, nb=nb, lp=lp, head_dim=hd_a)
            return pool_out, ssm_out, att, (logf, h_new)

        def mix_sample(proj):
            hist_p = jnp.pad(state_pool[:, i], ((0, 0), (POOL_HIST_ROWS - pool_hist, 0), (0, 0)))
            hist_c = jnp.pad(state_conv[:, i], ((0, 0), (CONV_HIST_ROWS - conv_hist, 0), (0, 0)))
            pool_out, logf = pool_and_forget(
                proj, cols, hist_p, pw, pool_scale[i], forget_bias[i],
                nb=ns, rows=DEC_ROWS, pos0=past_len, n_valid=1, with_cum=False)
            ssm_out, h_new = ssd_mixer(proj, cols, hist_c, state_ssm[:, i].reshape(ns, di, n_state), conv_w[i],
                                       conv_b[i], dt_bias[i], a_log[i], d_skip[i], ssm_norm[i],
                                       nb=ns, rows=DEC_ROWS, n_valid=1)
            att = decode_attention(proj, cols, logf, cache_k, cache_v, cache_logf, page_table, i, head_dim=hd_a)
            return pool_out, ssm_out, att, (logf, h_new)

        xp, proj_p, (lf_p, h_p) = layer(xp, mix_prompt)
        xs, proj_s, (lf_s, h_s) = layer(xs, mix_sample)

        pp3 = proj_p.reshape(nb, lp, -1)
        outs["kp"].append(pp3[:, :l, cols["k"]:cols["k"] + ad].reshape(nb, l, nh_a, hd_a))
        outs["vp"].append(pp3[:, :l, cols["v"]:cols["v"] + ad].reshape(nb, l, nh_a, hd_a))
        outs["lfp"].append(lf_p.reshape(nb, lp, nh_a)[:, :l])
        outs["hp"].append(h_p.reshape(nb, nh_s, hd_s, n_state))
        outs["cp"].append(pp3[:, l - conv_hist:l, cols["xbc"]:cols["xbc"] + cd])
        outs["pp"].append(pp3[:, l - pool_hist:l, cols["pool"]:cols["pool"] + pd])
        ps0 = proj_s.reshape(ns, DEC_ROWS, -1)[:, 0:1]
        outs["ks"].append(ps0[:, :, cols["k"]:cols["k"] + ad].reshape(ns, 1, nh_a, hd_a))
        outs["vs"].append(ps0[:, :, cols["v"]:cols["v"] + ad].reshape(ns, 1, nh_a, hd_a))
        outs["lfs"].append(lf_s.reshape(ns, DEC_ROWS, nh_a)[:, 0:1])
        outs["hs"].append(h_s.reshape(ns, nh_s, hd_s, n_state))
        outs["cs"].append(jnp.concatenate([state_conv[:, i, 1:], ps0[:, :, cols["xbc"]:cols["xbc"] + cd]], axis=1))
        outs["ps"].append(jnp.concatenate([state_pool[:, i, 1:], ps0[:, :, cols["pool"]:cols["pool"] + pd]], axis=1))

    y_prompt = final_norm_prompt(xp, final_norm_w, nb=nb, lp=lp, seq=seq, shift=n_meta)
    y_sample = final_norm(xs, final_norm_w).reshape(ns, DEC_ROWS, dm)[:, 0:1]
    st = lambda k: jnp.stack(outs[k], axis=1)
    return (y_prompt, y_sample, st("kp"), st("vp"), st("lfp"), st("ks"), st("vs"), st("lfs"),
            st("hp"), st("hs"), st("cp"), st("cs"), st("pp"), st("ps"))
```

```python
import functools
import math

import jax
import jax.numpy as jnp
from jax import lax
from jax.experimental import pallas as pl
from jax.experimental.pallas import tpu as pltpu

F32 = jnp.float32
BF16 = jnp.bfloat16
HIGHEST = lax.Precision.HIGHEST

NORM_EPS = 1e-6
NEG_INF = -1e30
LOG2E = 1.4426950408889634
POOL_WINDOWS = (2, 4, 8, 16)
POOL_HIST_ROWS = 16
CONV_TAPS = 4
CONV_HIST_ROWS = 8
SEQ_TILE = 128
DEC_ROWS = 16
LANES = 128
SMALL_W = 512
DT_LANE = 0
F_LANE = 128
VMEM_LIMIT_BYTES = 56 * 1024 * 1024


def _cparams(n_axes):
    return pltpu.CompilerParams(dimension_semantics=("arbitrary",) * n_axes,
                                vmem_limit_bytes=VMEM_LIMIT_BYTES)


def _pick(n, cands):
    for c in cands:
        if n % c == 0:
            return c
    return n


def _sigmoid(x):
    return 1.0 / (1.0 + jnp.exp(-x))


def _softplus(x):
    return jnp.maximum(x, 0.0) + jnp.log(1.0 + jnp.exp(-jnp.abs(x)))


def _dot(a, b, precision=None):
    return jnp.dot(a, b, preferred_element_type=F32, precision=precision)


def _dot_nt(a, b):
    return lax.dot_general(a, b, (((1,), (1,)), ((), ())), preferred_element_type=F32)


def _rms_to_scratch(x_ref, g_ref, xn_ref):
    x = x_ref[...]
    ms = jnp.mean(x * x, axis=-1, keepdims=True)
    xn_ref[...] = ((x * lax.rsqrt(ms + NORM_EPS)) * g_ref[...]).astype(BF16)


def _norm_swiglu_kernel(x_ref, g_ref, wg_ref, wu_ref, o_ref, xn_ref):
    @pl.when(pl.program_id(1) == 0)
    def _():
        _rms_to_scratch(x_ref, g_ref, xn_ref)

    xn = xn_ref[...]
    gate = _dot(xn, wg_ref[...])
    up = _dot(xn, wu_ref[...])
    o_ref[...] = ((gate * _sigmoid(gate)) * up).astype(o_ref.dtype)


def norm_swiglu(x, g, w_gu):
    m, d = x.shape
    f = w_gu.shape[1] // 2
    tm = _pick(m, (1056, 768, 704, 528, 512, 384, 256, 128))
    tn = _pick(f, (512, 256, 128))
    nj = f // tn
    return pl.pallas_call(
        _norm_swiglu_kernel,
        grid=(m // tm, nj),
        in_specs=[pl.BlockSpec((tm, d), lambda i, j: (i, 0)),
                  pl.BlockSpec((1, d), lambda i, j: (0, 0)),
                  pl.BlockSpec((d, tn), lambda i, j: (0, j)),
                  pl.BlockSpec((d, tn), lambda i, j: (0, j + nj))],
        out_specs=pl.BlockSpec((tm, tn), lambda i, j: (i, j)),
        out_shape=jax.ShapeDtypeStruct((m, f), BF16),
        scratch_shapes=[pltpu.VMEM((tm, d), BF16)],
        compiler_params=_cparams(2),
        name="norm_swiglu",
    )(x, g.reshape(1, d), w_gu, w_gu)


def _matmul_residual_kernel(h_ref, w_ref, x_ref, o_ref, *, scale):
    o_ref[...] = x_ref[...] + scale * _dot(h_ref[...], w_ref[...])


def matmul_residual(h, w, x, scale):
    m, k = h.shape
    n = w.shape[1]
    tm = _pick(m, (1056, 768, 704, 528, 512, 384, 256, 128))
    tn = _pick(n, (512, 256, 128))
    return pl.pallas_call(
        functools.partial(_matmul_residual_kernel, scale=scale),
        grid=(m // tm, n // tn),
        in_specs=[pl.BlockSpec((tm, k), lambda i, j: (i, 0)),
                  pl.BlockSpec((k, tn), lambda i, j: (0, j)),
                  pl.BlockSpec((tm, tn), lambda i, j: (i, j))],
        out_specs=pl.BlockSpec((tm, tn), lambda i, j: (i, j)),
        out_shape=jax.ShapeDtypeStruct((m, n), F32),
        compiler_params=_cparams(2),
        name="matmul_residual",
    )(h, w, x)


def _norm_matmul_kernel(x_ref, g_ref, w_ref, o_ref, xn_ref):
    @pl.when(pl.program_id(1) == 0)
    def _():
        _rms_to_scratch(x_ref, g_ref, xn_ref)

    o_ref[...] = _dot(xn_ref[...], w_ref[...])


def norm_matmul(x, g, w):
    m, d = x.shape
    n = w.shape[1]
    tm = _pick(m, (1056, 768, 704, 528, 512, 384, 256, 128))
    tn = _pick(n, (1024, 512, 256, 128))
    return pl.pallas_call(
        _norm_matmul_kernel,
        grid=(m // tm, n // tn),
        in_specs=[pl.BlockSpec((tm, d), lambda i, j: (i, 0)),
                  pl.BlockSpec((1, d), lambda i, j: (0, 0)),
                  pl.BlockSpec((d, tn), lambda i, j: (0, j))],
        out_specs=pl.BlockSpec((tm, tn), lambda i, j: (i, j)),
        out_shape=jax.ShapeDtypeStruct((m, n), F32),
        scratch_shapes=[pltpu.VMEM((tm, d), BF16)],
        compiler_params=_cparams(2),
        name="norm_matmul",
    )(x, g.reshape(1, d), w)


def _merge_kernel(p_ref, s_ref, a_ref, g0_ref, g1_ref, g2_ref, wp_ref, ws_ref, wa_ref, o_ref):
    merged = (_sigmoid(g0_ref[...]) * _dot(p_ref[...], wp_ref[...])
              + _sigmoid(g1_ref[...]) * _dot(s_ref[...], ws_ref[...])
              + _sigmoid(g2_ref[...]) * _dot(a_ref[...], wa_ref[...]))
    o_ref[...] = merged.astype(o_ref.dtype)


def gated_merge(pool_out, ssm_out, att, proj, gate_col, w_pool, w_ssm, w_attn):
    m, kb = pool_out.shape
    d = w_pool.shape[1]
    tm = _pick(m, (1056, 768, 704, 528, 512, 384, 256, 128))
    tn = _pick(d, (512, 256, 128))
    g_blk = gate_col // tn
    nj = d // tn
    branch = pl.BlockSpec((tm, kb), lambda i, j: (i, 0))
    weight = pl.BlockSpec((kb, tn), lambda i, j: (0, j))

    def gate(r):
        return pl.BlockSpec((tm, tn), lambda i, j: (i, g_blk + r * nj + j))

    return pl.pallas_call(
        _merge_kernel,
        grid=(m // tm, nj),
        in_specs=[branch, branch, branch, gate(0), gate(1), gate(2), weight, weight, weight],
        out_specs=pl.BlockSpec((tm, tn), lambda i, j: (i, j)),
        out_shape=jax.ShapeDtypeStruct((m, d), BF16),
        compiler_params=_cparams(2),
        name="gated_merge",
    )(pool_out, ssm_out, att, proj, proj, proj, w_pool, w_ssm, w_attn)


def _prep_kernel(xp_ref, sm_ref, hist_ref, pw_ref, ps_ref, fb_ref, *refs, rows, pos0, n_valid, with_cum):
    if with_cum:
        pool_ref, logf_ref, ccol_ref, crow_ref, full_ref, carry_ref = refs
    else:
        pool_ref, logf_ref, full_ref = refs
    t = pl.program_id(1)
    hr = POOL_HIST_ROWS

    @pl.when(t == 0)
    def _():
        full_ref[0:hr, :] = hist_ref[...]
        if with_cum:
            carry_ref[...] = jnp.zeros_like(carry_ref)

    x = xp_ref[...]
    full_ref[hr:hr + rows, :] = x
    row = t * rows + lax.broadcasted_iota(jnp.int32, (rows, 1), 0)
    gd = x.shape[1] // len(POOL_WINDOWS)
    for g, w in enumerate(POOL_WINDOWS):
        cols = slice(g * gd, (g + 1) * gd)
        s = full_ref[hr:hr + rows, cols]
        for k in range(1, w):
            s = s + full_ref[hr - k:hr - k + rows, cols]
        cnt = jnp.minimum(w, pos0 + row + 1).astype(F32)
        pooled = s / cnt - x[:, cols]
        mixed = _dot(pooled.astype(BF16), pw_ref[g])
        pool_ref[:, cols] = (mixed * ps_ref[:, cols]).astype(pool_ref.dtype)
    full_ref[0:hr, :] = full_ref[rows:rows + hr, :]

    z = sm_ref[:, F_LANE:F_LANE + logf_ref.shape[1]] + fb_ref[...]
    logf = -_softplus(-z)
    logf_ref[...] = logf
    if with_cum:
        lf = jnp.where(row < n_valid, logf, 0.0)
        r_i = lax.broadcasted_iota(jnp.int32, (rows, rows), 0)
        c_i = lax.broadcasted_iota(jnp.int32, (rows, rows), 1)
        tril = (r_i >= c_i).astype(F32)
        cs = _dot(tril, lf, HIGHEST) + carry_ref[...]
        ccol_ref[...] = cs
        carry_ref[...] = cs[rows - 1:rows, :]
        nh = cs.shape[1]
        wide = jnp.concatenate([cs, jnp.zeros((rows, LANES - nh), F32)], axis=1)
        crow_ref[...] = wide.T[0:nh, :]


def pool_and_forget(proj, cols, hist, pool_w, pool_scale, forget_bias, *, nb, rows, pos0, n_valid, with_cum):
    mtot = proj.shape[0]
    nt = mtot // (nb * rows)
    pd = pool_w.shape[0] * pool_w.shape[1]
    nh = forget_bias.shape[0]
    xp_blk = cols["pool"] // pd
    sm_blk = cols["small"] // SMALL_W
    in_specs = [pl.BlockSpec((rows, pd), lambda b, t: (b * nt + t, xp_blk)),
                pl.BlockSpec((rows, SMALL_W), lambda b, t: (b * nt + t, sm_blk)),
                pl.BlockSpec((None, POOL_HIST_ROWS, pd), lambda b, t: (b, 0, 0)),
                pl.BlockSpec(pool_w.shape, lambda b, t: (0, 0, 0)),
                pl.BlockSpec((1, pd), lambda b, t: (0, 0)),
                pl.BlockSpec((1, nh), lambda b, t: (0, 0))]
    out_specs = [pl.BlockSpec((rows, pd), lambda b, t: (b * nt + t, 0)),
                 pl.BlockSpec((rows, nh), lambda b, t: (b * nt + t, 0))]
    out_shape = [jax.ShapeDtypeStruct((mtot, pd), BF16), jax.ShapeDtypeStruct((mtot, nh), F32)]
    scratch = [pltpu.VMEM((POOL_HIST_ROWS + rows, pd), F32)]
    if with_cum:
        out_specs += [pl.BlockSpec((rows, nh), lambda b, t: (b * nt + t, 0)),
                      pl.BlockSpec((None, nh, rows), lambda b, t: (b, 0, t))]
        out_shape += [jax.ShapeDtypeStruct((mtot, nh), F32), jax.ShapeDtypeStruct((nb, nh, nt * rows), F32)]
        scratch += [pltpu.VMEM((1, nh), F32)]
    return pl.pallas_call(
        functools.partial(_prep_kernel, rows=rows, pos0=pos0, n_valid=n_valid, with_cum=with_cum),
        grid=(nb, nt),
        in_specs=in_specs, out_specs=out_specs, out_shape=out_shape, scratch_shapes=scratch,
        compiler_params=_cparams(2),
        name="pool_and_forget",
    )(proj, proj, hist, pool_w, pool_scale.reshape(1, pd), forget_bias.reshape(1, nh))


def _ssd_kernel(xz_ref, z_ref, hist_ref, h0_ref, cw_ref, cb_ref, dtb_ref, alog_ref, dskip_ref, nw_ref,
                ex_ref, ext_ref, o_ref, h_ref, full_ref, y_ref, *, rows, n_valid, d_inner, n_state, head_dim):
    t = SEQ_TILE
    c = pl.program_id(1)
    hr = CONV_HIST_ROWS
    cdim = full_ref.shape[1]
    nheads = d_inner // head_dim
    ngroups = (cdim - d_inner) // (2 * n_state)
    pair_w = 2 * head_dim
    pairs_per_group = d_inner // pair_w // ngroups

    @pl.when(c == 0)
    def _():
        full_ref[0:hr, :] = hist_ref[...]
        h_ref[...] = h0_ref[...]

    full_ref[hr:hr + rows, :] = xz_ref[:, 0:cdim]
    if rows < t:
        full_ref[hr + rows:hr + t, :] = jnp.zeros((t - rows, cdim), F32)
    conv = cb_ref[...]
    for k in range(CONV_TAPS):
        off = hr - (CONV_TAPS - 1) + k
        conv = conv + full_ref[off:off + t, :] * cw_ref[k:k + 1, :]
    act = conv * _sigmoid(conv)
    full_ref[0:hr, :] = full_ref[t:t + hr, :]

    dt_raw = xz_ref[:, cdim + DT_LANE:cdim + DT_LANE + nheads]
    if rows < t:
        dt_raw = jnp.concatenate([dt_raw, jnp.zeros((t - rows, nheads), F32)], axis=0)
    row = c * t + lax.broadcasted_iota(jnp.int32, (t, 1), 0)
    dt = jnp.where(row < n_valid, _softplus(dt_raw + dtb_ref[...]), 0.0)
    a = -jnp.exp(alog_ref[...])
    r_i = lax.broadcasted_iota(jnp.int32, (t, t), 0)
    c_i = lax.broadcasted_iota(jnp.int32, (t, t), 1)
    causal = r_i >= c_i
    acum = _dot(causal.astype(F32), dt * a, HIGHEST)
    lane_pad = jnp.zeros((t, LANES - nheads), F32)
    acum_row = jnp.concatenate([acum, lane_pad], axis=1).T[0:nheads, :]
    dt_row = jnp.concatenate([dt, lane_pad], axis=1).T[0:nheads, :]
    a_last = acum[t - 1:t, :]
    decay_in = jnp.exp(acum)
    to_end = jnp.exp(a_last - acum) * dt
    wide = _dot(jnp.concatenate([decay_in, to_end], axis=0), ex_ref[...], HIGHEST)
    decay_in_x = wide[0:t, :]
    to_end_x = wide[t:2 * t, :]
    chunk_decay = jnp.broadcast_to(jnp.exp(acum_row[:, t - 1:t]), (nheads, n_state))
    chunk_decay_x = _dot(ext_ref[...], chunk_decay, HIGHEST)

    xs = act[:, 0:d_inner]
    xs_b = xs.astype(BF16)
    lane = lax.broadcasted_iota(jnp.int32, (t, pair_w), 1)
    first = lane < head_dim
    for g in range(ngroups):
        bm = act[:, d_inner + g * n_state:d_inner + (g + 1) * n_state].astype(BF16)
        cm = act[:, d_inner + (ngroups + g) * n_state:d_inner + (ngroups + g + 1) * n_state].astype(BF16)
        cb = _dot_nt(cm, bm)
        for j in range(pairs_per_group):
            p = g * pairs_per_group + j
            cols = slice(p * pair_w, (p + 1) * pair_w)
            wts = []
            for h in (2 * p, 2 * p + 1):
                seg = acum[:, h:h + 1] - acum_row[h:h + 1, :]
                decay = jnp.exp(jnp.where(causal, seg, -jnp.inf))
                wts.append((cb * decay) * dt_row[h:h + 1, :])
            lhs = jnp.concatenate(wts, axis=1).astype(BF16)
            xp = xs_b[:, cols]
            zero = jnp.zeros_like(xp)
            rhs = jnp.concatenate([jnp.where(first, xp, zero), jnp.where(first, zero, xp)], axis=0)
            y = _dot(lhs, rhs)
            hp = h_ref[p * pair_w:(p + 1) * pair_w, :]
            y = y + _dot_nt(cm, hp.astype(BF16)) * decay_in_x[:, cols]
            y_ref[:, cols] = y + dskip_ref[:, cols] * xs[:, cols]
            upd = (xs[:, cols] * to_end_x[:, cols]).T.astype(BF16)
            h_ref[p * pair_w:(p + 1) * pair_w, :] = chunk_decay_x[p * pair_w:(p + 1) * pair_w, :] * hp + _dot(upd, bm)

    zz = z_ref[...]
    if rows < t:
        zz = jnp.concatenate([zz, jnp.zeros((t - rows, d_inner), F32)], axis=0)
    gated = y_ref[...] * (zz * _sigmoid(zz))
    ms = jnp.mean(gated * gated, axis=-1, keepdims=True)
    out = (gated * lax.rsqrt(ms + NORM_EPS)) * nw_ref[...]
    o_ref[...] = out[0:rows, :].astype(o_ref.dtype)


def ssd_mixer(proj, cols, hist, h0, conv_w, conv_b, dt_bias, a_log, d_skip, ssm_norm, *, nb, rows, n_valid):
    mtot = proj.shape[0]
    nt = mtot // (nb * rows)
    cdim = conv_w.shape[1]
    nheads = dt_bias.shape[0]
    d_inner, n_state = h0.shape[1], h0.shape[2]
    head_dim = d_inner // nheads
    xz_w = cdim + SMALL_W
    expand = jnp.repeat(jnp.eye(nheads, dtype=F32), head_dim, axis=1)
    const2 = lambda b, t: (0, 0)
    return pl.pallas_call(
        functools.partial(_ssd_kernel, rows=rows, n_valid=n_valid, d_inner=d_inner, n_state=n_state,
                          head_dim=head_dim),
        grid=(nb, nt),
        in_specs=[pl.BlockSpec((rows, xz_w), lambda b, t: (b * nt + t, cols["xbc"] // xz_w)),
                  pl.BlockSpec((rows, d_inner), lambda b, t: (b * nt + t, cols["z"] // d_inner)),
                  pl.BlockSpec((None, CONV_HIST_ROWS, cdim), lambda b, t: (b, 0, 0)),
                  pl.BlockSpec((None, d_inner, n_state), lambda b, t: (b, 0, 0)),
                  pl.BlockSpec((CONV_TAPS, cdim), const2),
                  pl.BlockSpec((1, cdim), const2),
                  pl.BlockSpec((1, nheads), const2),
                  pl.BlockSpec((1, nheads), const2),
                  pl.BlockSpec((1, d_inner), const2),
                  pl.BlockSpec((1, d_inner), const2),
                  pl.BlockSpec((nheads, d_inner), const2),
                  pl.BlockSpec((d_inner, nheads), const2)],
        out_specs=[pl.BlockSpec((rows, d_inner), lambda b, t: (b * nt + t, 0)),
                   pl.BlockSpec((None, d_inner, n_state), lambda b, t: (b, 0, 0))],
        out_shape=[jax.ShapeDtypeStruct((mtot, d_inner), BF16),
                   jax.ShapeDtypeStruct((nb, d_inner, n_state), F32)],
        scratch_shapes=[pltpu.VMEM((CONV_HIST_ROWS + SEQ_TILE, cdim), F32),
                        pltpu.VMEM((SEQ_TILE, d_inner), F32)],
        compiler_params=_cparams(2),
        name="ssd_mixer",
    )(proj, proj, hist, h0, conv_w, conv_b.reshape(1, cdim), dt_bias.reshape(1, nheads),
      a_log.reshape(1, nheads), jnp.repeat(d_skip, head_dim).reshape(1, d_inner),
      ssm_norm.reshape(1, d_inner), expand, expand.T)


def _split3(x):
    hi = x.astype(BF16).astype(F32)
    r = x - hi
    mid = r.astype(BF16).astype(F32)
    lo = (r - mid).astype(BF16).astype(F32)
    return hi, mid, lo


def _attn_kernel(q_ref, k_ref, v_ref, ccol_ref, crow_ref, o_ref, kaug_ref, vt_ref, acc_ref, s_ref, *, tq, head_dim, lp):
    pair = pl.program_id(1)
    qi = pl.program_id(2)
    nh = ccol_ref.shape[1]
    pw = 2 * head_dim
    tk = tq
    n_parts = 3
    one_lane = 2 * n_parts

    @pl.when(qi == 0)
    def _():
        hid = lax.broadcasted_iota(jnp.int32, (nh, pw), 0)
        lid = lax.broadcasted_iota(jnp.int32, (nh, pw), 1)
        sel = jnp.where((hid == 2 * pair) & (lid < n_parts), 1.0,
                        jnp.where((hid == 2 * pair + 1) & (lid >= n_parts) & (lid < one_lane), 1.0, 0.0))
        lane = lax.broadcasted_iota(jnp.int32, (SEQ_TILE, pw), 1)
        part = lane - jnp.where(lane >= n_parts, n_parts, 0)

        def body(i, carry):
            r = pl.multiple_of(i * SEQ_TILE, SEQ_TILE)
            c2 = _dot(ccol_ref[pl.ds(r, SEQ_TILE), :], sel, HIGHEST) * (-LOG2E)
            hi, mid, lo = _split3(c2)
            bias = jnp.where(part == 0, hi, jnp.where(part == 1, mid, lo))
            bias = jnp.where(lane < one_lane, bias, jnp.where(lane < one_lane + n_parts, 1.0, 0.0))
            kaug_ref[pl.ds(r, SEQ_TILE), 0:pw] = k_ref[pl.ds(r, SEQ_TILE), :].astype(BF16)
            kaug_ref[pl.ds(r, SEQ_TILE), pw:2 * pw] = bias.astype(BF16)
            vt_ref[:, pl.ds(r, SEQ_TILE)] = v_ref[pl.ds(r, SEQ_TILE), :].T.astype(BF16)
            return carry

        lax.fori_loop(0, lp // SEQ_TILE, body, 0)

    qs = pl.multiple_of(qi * tq, tq)
    qt = (q_ref[...] * (head_dim ** -0.5 * LOG2E)).T
    row = lax.broadcasted_iota(jnp.int32, (pw, tq), 0)
    q_aug = []
    for e in range(2):
        head_rows = (row < head_dim) if e == 0 else (row >= head_dim)
        c_q = crow_ref[pl.ds(2 * pair + e, 1), pl.ds(qs, tq)] * LOG2E
        hi, mid, lo = _split3(c_q)
        ones_rows = (row >= n_parts * e) & (row < n_parts * (e + 1))
        qb = jnp.where(ones_rows, 1.0,
                       jnp.where(row == one_lane, hi, jnp.where(row == one_lane + 1, mid,
                                                                jnp.where(row == one_lane + 2, lo, 0.0))))
        q_aug.append(jnp.concatenate([jnp.where(head_rows, qt, 0.0), qb], axis=0).astype(BF16))

    acc_ref[...] = jnp.zeros_like(acc_ref)
    q_pos = qs + lax.broadcasted_iota(jnp.int32, (1, tq), 1)

    def scores(kt):
        ks = pl.multiple_of(kt * tk, tk)
        ka = kaug_ref[pl.ds(ks, tk), :]
        return [_dot(ka, q_aug[e]) for e in range(2)]

    def consume(kt, s_pair, carry, masked):
        ks = pl.multiple_of(kt * tk, tk)
        vt = vt_ref[:, pl.ds(ks, tk)]
        new = []
        for e in range(2):
            m_old, l_old = carry[2 * e], carry[2 * e + 1]
            s = s_pair[e]
            if masked:
                k_pos = ks + lax.broadcasted_iota(jnp.int32, (tk, 1), 0)
                s = jnp.where(k_pos <= q_pos, s, NEG_INF)
            m_new = jnp.maximum(m_old, jnp.max(s, axis=0, keepdims=True))
            alpha = jnp.exp2(m_old - m_new)
            p = jnp.exp2(s - m_new)
            l_new = alpha * l_old + jnp.sum(p, axis=0, keepdims=True)
            acc_ref[e] = alpha * acc_ref[e] + _dot(vt, p.astype(BF16))
            new += [m_new, l_new]
        return tuple(new)

    def body(kt, carry):
        s_cur = [s_ref[0], s_ref[1]]
        s_next = scores(kt + 1)
        carry = consume(kt, s_cur, carry, False)
        s_ref[0] = s_next[0]
        s_ref[1] = s_next[1]
        return carry

    s_first = scores(0)
    s_ref[0] = s_first[0]
    s_ref[1] = s_first[1]
    init = (jnp.full((1, tq), NEG_INF, F32), jnp.zeros((1, tq), F32)) * 2
    carry = lax.fori_loop(0, qi, body, init)
    carry = consume(qi, [s_ref[0], s_ref[1]], carry, True)
    out_t = jnp.where(row < head_dim, acc_ref[0] / carry[1], acc_ref[1] / carry[3])
    o_ref[...] = out_t.T.astype(o_ref.dtype)


def prompt_attention(proj, cols, ccol, crow, *, nb, lp, head_dim):
    mtot = proj.shape[0]
    nh = ccol.shape[1]
    pw = 2 * head_dim
    tq = _pick(lp, (384, 256, 128))
    nq = lp // tq
    return pl.pallas_call(
        functools.partial(_attn_kernel, tq=tq, head_dim=head_dim, lp=lp),
        grid=(nb, nh // 2, nq),
        in_specs=[pl.BlockSpec((tq, pw), lambda b, p, i: (b * nq + i, cols["q"] // pw + p)),
                  pl.BlockSpec((lp, pw), lambda b, p, i: (b, cols["k"] // pw + p)),
                  pl.BlockSpec((lp, pw), lambda b, p, i: (b, cols["v"] // pw + p)),
                  pl.BlockSpec((lp, nh), lambda b, p, i: (b, 0)),
                  pl.BlockSpec((None, nh, lp), lambda b, p, i: (b, 0, 0))],
        out_specs=pl.BlockSpec((tq, pw), lambda b, p, i: (b * nq + i, p)),
        out_shape=jax.ShapeDtypeStruct((mtot, nh * head_dim), BF16),
        scratch_shapes=[pltpu.VMEM((lp, 2 * pw), BF16), pltpu.VMEM((pw, lp), BF16), pltpu.VMEM((2, pw, tq), F32),
                        pltpu.VMEM((2, tq, tq), F32)],
        compiler_params=_cparams(3),
        name="prompt_attention",
    )(proj, proj, proj, ccol, crow)


def _suffix_kernel(pt_ref, *refs, pages):
    lf_refs = refs[0:pages]
    o_ref, carry_ref = refs[pages:]
    psz = lf_refs[0].shape[0]

    @pl.when(pl.program_id(2) == 0)
    def _():
        carry_ref[...] = jnp.zeros_like(carry_ref)

    r_i = lax.broadcasted_iota(jnp.int32, (psz, psz), 0)
    c_i = lax.broadcasted_iota(jnp.int32, (psz, psz), 1)
    later = (c_i > r_i).astype(BF16)
    for i in range(pages):
        lf = lf_refs[i][...]
        carry = carry_ref[...]
        hi, mid, lo = _split3(lf)
        within = (_dot(later, hi.astype(BF16)) + _dot(later, mid.astype(BF16))) + _dot(later, lo.astype(BF16))
        o_ref[pages - 1 - i] = within + carry
        carry_ref[...] = carry + jnp.sum(lf, axis=0, keepdims=True)


def past_forget_suffix(cache_logf, page_table):
    nb, n_pages = page_table.shape
    _, depth, psz, nh = cache_logf.shape
    pages = _pick(n_pages, (16, 8, 4, 2, 1))
    ns = n_pages // pages

    def page(i):
        def index(d, b, s, pt):
            return (pt[b, n_pages - 1 - (s * pages + i)], d, 0, 0)
        return pl.BlockSpec((None, None, psz, nh), index)

    grid_spec = pltpu.PrefetchScalarGridSpec(
        num_scalar_prefetch=1,
        grid=(depth, nb, ns),
        in_specs=[page(i) for i in range(pages)],
        out_specs=pl.BlockSpec((None, None, pages, psz, nh), lambda d, b, s, pt: (d, b, ns - 1 - s, 0, 0)),
        scratch_shapes=[pltpu.VMEM((1, nh), F32)],
    )
    return pl.pallas_call(
        functools.partial(_suffix_kernel, pages=pages),
        grid_spec=grid_spec,
        out_shape=jax.ShapeDtypeStruct((depth, nb, n_pages, psz, nh), F32),
        compiler_params=_cparams(3),
        name="past_forget_suffix",
    )(page_table, *([cache_logf] * pages))


def _decode_attn_kernel(pt_ref, q_ref, kn_ref, vn_ref, lfn_ref, *refs, pages):
    k_refs = refs[0:pages]
    v_refs = refs[pages:2 * pages]
    d_refs = refs[2 * pages:3 * pages]
    o_ref, m_ref, l_ref, acc_ref = refs[3 * pages:]
    s_id = pl.program_id(1)
    psz, nh, hd = k_refs[0].shape
    qm = (q_ref[...] * hd ** -0.5).astype(BF16)

    @pl.when(s_id == 0)
    def _():
        s_all = _dot_nt(qm, kn_ref[...].astype(BF16))
        eye = lax.broadcasted_iota(jnp.int32, (nh, nh), 0) == lax.broadcasted_iota(jnp.int32, (nh, nh), 1)
        m_ref[...] = jnp.sum(jnp.where(eye, s_all, 0.0), axis=1, keepdims=True)
        l_ref[...] = jnp.ones_like(l_ref)
        acc_ref[...] = vn_ref[...]

    col = lax.broadcasted_iota(jnp.int32, (nh, psz * nh), 1)
    own = jnp.bitwise_and(col, nh - 1) == lax.broadcasted_iota(jnp.int32, (nh, psz * nh), 0)
    lf_new = lfn_ref[...]
    for i in range(pages):
        k2 = k_refs[i][...].reshape(psz * nh, hd).astype(BF16)
        s = _dot_nt(qm, k2) + (d_refs[i][...] + lf_new)
        s = jnp.where(own, s, NEG_INF)
        m_old = m_ref[...]
        m_new = jnp.maximum(m_old, jnp.max(s, axis=1, keepdims=True))
        alpha = jnp.exp(m_old - m_new)
        p = jnp.exp(s - m_new)
        l_ref[...] = alpha * l_ref[...] + jnp.sum(p, axis=1, keepdims=True)
        m_ref[...] = m_new
        v2 = v_refs[i][...].reshape(psz * nh, hd).astype(BF16)
        acc_ref[...] = alpha * acc_ref[...] + _dot(p.astype(BF16), v2)

    @pl.when(s_id == pl.num_programs(1) - 1)
    def _():
        o_ref[...] = acc_ref[...] / l_ref[...]


def decode_attention(q, k_new, v_new, logf_new, suffix, cache_k, cache_v, page_table, layer):
    nb, n_pages = page_table.shape
    _, _, psz, nh, hd = cache_k.shape
    assert nh & (nh - 1) == 0
    pages = _pick(n_pages, (8, 4, 2, 1))
    lfn = jnp.tile(logf_new, (1, psz)).reshape(nb, 1, psz * nh)

    def page(i):
        def index(b, s, pt):
            return (pt[b, s * pages + i], layer, 0, 0, 0)
        return pl.BlockSpec((None, None, psz, nh, hd), index)

    def bias(i):
        return pl.BlockSpec((None, None, 1, psz * nh), lambda b, s, pt: (b, s * pages + i, 0, 0))

    tok = pl.BlockSpec((None, nh, hd), lambda b, s, pt: (b, 0, 0))
    grid_spec = pltpu.PrefetchScalarGridSpec(
        num_scalar_prefetch=1,
        grid=(nb, n_pages // pages),
        in_specs=([tok, tok, tok, pl.BlockSpec((None, 1, psz * nh), lambda b, s, pt: (b, 0, 0))]
                  + [page(i) for i in range(pages)] + [page(i) for i in range(pages)]
                  + [bias(i) for i in range(pages)]),
        out_specs=tok,
        scratch_shapes=[pltpu.VMEM((nh, 1), F32), pltpu.VMEM((nh, 1), F32), pltpu.VMEM((nh, hd), F32)],
    )
    return pl.pallas_call(
        functools.partial(_decode_attn_kernel, pages=pages),
        grid_spec=grid_spec,
        out_shape=jax.ShapeDtypeStruct((nb, nh, hd), F32),
        compiler_params=_cparams(2),
        name="decode_attention",
    )(page_table, q, k_new, v_new, lfn, *([cache_k] * pages), *([cache_v] * pages), *([suffix] * pages))


def _final_norm_shift_kernel(a_ref, b_ref, g_ref, o_ref, *, shift):
    x = jnp.concatenate([a_ref[shift:, :], b_ref[0:shift, :]], axis=0)
    ms = jnp.mean(x * x, axis=-1, keepdims=True)
    o_ref[...] = (x * lax.rsqrt(ms + NORM_EPS)) * g_ref[...]


def final_norm_prompt(x, g, *, nb, lp, seq, shift):
    d = x.shape[1]
    tr = SEQ_TILE
    nt = seq // tr
    nl = lp // tr
    return pl.pallas_call(
        functools.partial(_final_norm_shift_kernel, shift=shift),
        grid=(nb, nt),
        in_specs=[pl.BlockSpec((tr, d), lambda b, t: (b * nl + t, 0)),
                  pl.BlockSpec((tr, d), lambda b, t: (b * nl + t + 1, 0)),
                  pl.BlockSpec((1, d), lambda b, t: (0, 0))],
        out_specs=pl.BlockSpec((None, tr, d), lambda b, t: (b, t, 0)),
        out_shape=jax.ShapeDtypeStruct((nb, seq, d), F32),
        compiler_params=_cparams(2),
        name="final_norm_prompt",
    )(x, x, g.reshape(1, d))


def _final_norm_kernel(x_ref, g_ref, o_ref):
    x = x_ref[...]
    ms = jnp.mean(x * x, axis=-1, keepdims=True)
    o_ref[...] = (x * lax.rsqrt(ms + NORM_EPS)) * g_ref[...]


def final_norm(x, g):
    m, d = x.shape
    return pl.pallas_call(
        _final_norm_kernel,
        grid=(1,),
        in_specs=[pl.BlockSpec((m, d), lambda i: (0, 0)), pl.BlockSpec((1, d), lambda i: (0, 0))],
        out_specs=pl.BlockSpec((m, d), lambda i: (0, 0)),
        out_shape=jax.ShapeDtypeStruct((m, d), F32),
        compiler_params=_cparams(1),
        name="final_norm",
    )(x, g.reshape(1, d))


def _pack_w_in(w_in, dims):
    d = w_in.shape[0]
    pd, di, cd, nh_s, ad, nh_a, dm = dims
    off = {}
    o = 0
    for name, width in (("pool", pd), ("z", di), ("xbc", cd), ("dt", nh_s), ("q", ad), ("k", ad), ("v", ad),
                        ("f", nh_a), ("gate", w_in.shape[1] - (pd + di + cd + nh_s + 3 * ad + nh_a))):
        off[name] = (o, width)
        o += width

    def seg(name):
        s, w = off[name]
        return w_in[:, s:s + w]

    small = jnp.zeros((d, SMALL_W), w_in.dtype)
    small = small.at[:, DT_LANE:DT_LANE + nh_s].set(seg("dt")).at[:, F_LANE:F_LANE + nh_a].set(seg("f"))
    parts = [("xbc", seg("xbc")), ("small", small), ("pool", seg("pool")), ("z", seg("z")), ("q", seg("q")),
             ("k", seg("k")), ("v", seg("v")), ("gate", seg("gate"))]
    cols = {}
    o = 0
    for name, arr in parts:
        cols[name] = o
        o += arr.shape[1]
    return jnp.concatenate([a for _, a in parts], axis=1).astype(BF16), cols


def kernel(x_prompt, x_sample, cache_k, cache_v, cache_logf, page_table, state_ssm, state_conv, state_pool,
           meta_tokens, norm_ffn1, ffn1_w_gu, ffn1_w_down, norm_mix, w_in, pool_w, pool_scale, conv_w, conv_b,
           dt_bias, a_log, d_skip, ssm_norm, forget_bias, w_br_pool, w_br_ssm, w_br_attn, w_out, norm_ffn2,
           ffn2_w_gu, ffn2_w_down, final_norm_w):
    nb, seq, dm = x_prompt.shape
    ns = x_sample.shape[0]
    depth = w_in.shape[0]
    n_meta = meta_tokens.shape[0]
    nh_a, hd_a = cache_k.shape[3], cache_k.shape[4]
    ad = nh_a * hd_a
    psz = cache_k.shape[2]
    past_len = page_table.shape[1] * psz
    nh_s, hd_s, n_state = state_ssm.shape[2], state_ssm.shape[3], state_ssm.shape[4]
    di = nh_s * hd_s
    cd = state_conv.shape[3]
    conv_hist = state_conv.shape[2]
    pd = state_pool.shape[3]
    pool_hist = state_pool.shape[2]
    l = seq + n_meta
    lp = -(-l // SEQ_TILE) * SEQ_TILE
    assert x_sample.shape[1] == 1 and seq % SEQ_TILE == 0 and n_meta % 8 == 0 and n_meta < SEQ_TILE
    assert 2 * hd_a == LANES and 2 * hd_s == LANES and n_state == LANES
    assert pool_hist < POOL_HIST_ROWS and conv_hist < CONV_HIST_ROWS

    xp = jnp.concatenate([jnp.broadcast_to(meta_tokens[None], (nb, n_meta, dm)), x_prompt,
                          jnp.zeros((nb, lp - l, dm), F32)], axis=1).reshape(nb * lp, dm)
    xs = jnp.pad(x_sample, ((0, 0), (0, DEC_ROWS - 1), (0, 0))).reshape(ns * DEC_ROWS, dm)

    zero_pool_hist = jnp.zeros((nb, POOL_HIST_ROWS, pd), F32)
    zero_conv_hist = jnp.zeros((nb, CONV_HIST_ROWS, cd), F32)
    zero_state = jnp.zeros((nb, di, n_state), F32)
    n_pages = page_table.shape[1]
    suffix = past_forget_suffix(cache_logf, page_table).reshape(depth, ns, n_pages, 1, psz * nh_a)

    outs = {k: [] for k in ("kp", "vp", "lfp", "hp", "cp", "pp", "ks", "vs", "lfs", "hs", "cs", "ps")}
    for i in range(depth):
        w_in_i, cols = _pack_w_in(w_in[i], (pd, di, cd, nh_s, ad, nh_a, dm))
        wgu1, wd1 = ffn1_w_gu[i].astype(BF16), ffn1_w_down[i].astype(BF16)
        wgu2, wd2 = ffn2_w_gu[i].astype(BF16), ffn2_w_down[i].astype(BF16)
        wbp, wbs, wba = w_br_pool[i].astype(BF16), w_br_ssm[i].astype(BF16), w_br_attn[i].astype(BF16)
        wo = w_out[i].astype(BF16)
        pw = pool_w[i].astype(BF16)

        def layer(x, mix):
            x = matmul_residual(norm_swiglu(x, norm_ffn1[i], wgu1), wd1, x, 0.5)
            proj = norm_matmul(x, norm_mix[i], w_in_i)
            pool_out, ssm_out, att, extra = mix(proj)
            merged = gated_merge(pool_out, ssm_out, att, proj, cols["gate"], wbp, wbs, wba)
            x = matmul_residual(merged, wo, x, 1.0)
            x = matmul_residual(norm_swiglu(x, norm_ffn2[i], wgu2), wd2, x, 0.5)
            return x, proj, extra

        def mix_prompt(proj):
            pool_out, logf, ccol, crow = pool_and_forget(
                proj, cols, zero_pool_hist, pw, pool_scale[i], forget_bias[i],
                nb=nb, rows=SEQ_TILE, pos0=0, n_valid=l, with_cum=True)
            ssm_out, h_new = ssd_mixer(proj, cols, zero_conv_hist, zero_state, conv_w[i], conv_b[i], dt_bias[i],
                                       a_log[i], d_skip[i], ssm_norm[i], nb=nb, rows=SEQ_TILE, n_valid=l)
            att = prompt_attention(proj, cols, ccol, crow, nb=nb, lp=lp, head_dim=hd_a)
            return pool_out, ssm_out, att, (logf, h_new)

        def mix_sample(proj):
            hist_p = jnp.pad(state_pool[:, i], ((0, 0), (POOL_HIST_ROWS - pool_hist, 0), (0, 0)))
            hist_c = jnp.pad(state_conv[:, i], ((0, 0), (CONV_HIST_ROWS - conv_hist, 0), (0, 0)))
            pool_out, logf = pool_and_forget(
                proj, cols, hist_p, pw, pool_scale[i], forget_bias[i],
                nb=ns, rows=DEC_ROWS, pos0=past_len, n_valid=1, with_cum=False)
            ssm_out, h_new = ssd_mixer(proj, cols, hist_c, state_ssm[:, i].reshape(ns, di, n_state), conv_w[i],
                                       conv_b[i], dt_bias[i], a_log[i], d_skip[i], ssm_norm[i],
                                       nb=ns, rows=DEC_ROWS, n_valid=1)
            tok = proj.reshape(ns, DEC_ROWS, -1)[:, 0]
            heads = lambda name: tok[:, cols[name]:cols[name] + ad].reshape(ns, nh_a, hd_a)
            att = decode_attention(heads("q"), heads("k"), heads("v"), logf.reshape(ns, DEC_ROWS, nh_a)[:, 0],
                                   suffix[i], cache_k, cache_v, page_table, i)
            att = jnp.pad(att.reshape(ns, 1, ad), ((0, 0), (0, DEC_ROWS - 1), (0, 0)))
            return pool_out, ssm_out, att.reshape(ns * DEC_ROWS, ad).astype(BF16), (logf, h_new)

        xp, proj_p, (lf_p, h_p) = layer(xp, mix_prompt)
        xs, proj_s, (lf_s, h_s) = layer(xs, mix_sample)

        pp3 = proj_p.reshape(nb, lp, -1)
        outs["kp"].append(pp3[:, :l, cols["k"]:cols["k"] + ad].reshape(nb, l, nh_a, hd_a))
        outs["vp"].append(pp3[:, :l, cols["v"]:cols["v"] + ad].reshape(nb, l, nh_a, hd_a))
        outs["lfp"].append(lf_p.reshape(nb, lp, nh_a)[:, :l])
        outs["hp"].append(h_p.reshape(nb, nh_s, hd_s, n_state))
        outs["cp"].append(pp3[:, l - conv_hist:l, cols["xbc"]:cols["xbc"] + cd])
        outs["pp"].append(pp3[:, l - pool_hist:l, cols["pool"]:cols["pool"] + pd])
        ps0 = proj_s.reshape(ns, DEC_ROWS, -1)[:, 0:1]
        outs["ks"].append(ps0[:, :, cols["k"]:cols["k"] + ad].reshape(ns, 1, nh_a, hd_a))
        outs["vs"].append(ps0[:, :, cols["v"]:cols["v"] + ad].reshape(ns, 1, nh_a, hd_a))
        outs["lfs"].append(lf_s.reshape(ns, DEC_ROWS, nh_a)[:, 0:1])
        outs["hs"].append(h_s.reshape(ns, nh_s, hd_s, n_state))
        outs["cs"].append(jnp.concatenate([state_conv[:, i, 1:], ps0[:, :, cols["xbc"]:cols["xbc"] + cd]], axis=1))
        outs["ps"].append(jnp.concatenate([state_pool[:, i, 1:], ps0[:, :, cols["pool"]:cols["pool"] + pd]], axis=1))

    y_prompt = final_norm_prompt(xp, final_norm_w, nb=nb, lp=lp, seq=seq, shift=n_meta)
    y_sample = final_norm(xs, final_norm_w).reshape(ns, DEC_ROWS, dm)[:, 0:1]
    st = lambda k: jnp.stack(outs[k], axis=1)
    return (y_prompt, y_sample, st("kp"), st("vp"), st("lfp"), st("ks"), st("vs"), st("lfs"),
            st("hp"), st("hs"), st("cp"), st("cs"), st("pp"), st("ps"))
```

```python
import functools
import math

import jax
import jax.numpy as jnp
from jax import lax
from jax.experimental import pallas as pl
from jax.experimental.pallas import tpu as pltpu

F32 = jnp.float32
BF16 = jnp.bfloat16
HIGHEST = lax.Precision.HIGHEST

NORM_EPS = 1e-6
NEG_INF = -1e30
LOG2E = 1.4426950408889634
POOL_WINDOWS = (2, 4, 8, 16)
POOL_HIST_ROWS = 16
CONV_TAPS = 4
CONV_HIST_ROWS = 8
SEQ_TILE = 128
DEC_ROWS = 16
LANES = 128
SMALL_W = 512
DT_LANE = 0
F_LANE = 128
VMEM_LIMIT_BYTES = 56 * 1024 * 1024
ROW_TILES = (1072, 1056, 768, 704, 528, 512, 384, 256, 128)


def _cparams(n_axes):
    return pltpu.CompilerParams(dimension_semantics=("arbitrary",) * n_axes,
                                vmem_limit_bytes=VMEM_LIMIT_BYTES)


def _pick(n, cands):
    for c in cands:
        if n % c == 0:
            return c
    return n


def _sigmoid(x):
    return 1.0 / (1.0 + jnp.exp(-x))


def _softplus(x):
    return jnp.maximum(x, 0.0) + jnp.log(1.0 + jnp.exp(-jnp.abs(x)))


def _dot(a, b, precision=None):
    return jnp.dot(a, b, preferred_element_type=F32, precision=precision)


def _dot_nt(a, b):
    return lax.dot_general(a, b, (((1,), (1,)), ((), ())), preferred_element_type=F32)


def _rms_to_scratch(x_ref, g_ref, xn_ref):
    x = x_ref[...]
    ms = jnp.mean(x * x, axis=-1, keepdims=True)
    xn_ref[...] = ((x * lax.rsqrt(ms + NORM_EPS)) * g_ref[...]).astype(BF16)


def _norm_swiglu_kernel(x_ref, g_ref, wg_ref, wu_ref, o_ref, xn_ref):
    @pl.when(pl.program_id(1) == 0)
    def _():
        _rms_to_scratch(x_ref, g_ref, xn_ref)

    xn = xn_ref[...]
    gate = _dot(xn, wg_ref[...].astype(BF16))
    up = _dot(xn, wu_ref[...].astype(BF16))
    o_ref[...] = ((gate * _sigmoid(gate)) * up).astype(o_ref.dtype)


def norm_swiglu(x, g, w_gu, layer):
    m, d = x.shape
    f = w_gu.shape[2] // 2
    tm = _pick(m, ROW_TILES)
    tn = _pick(f, (512, 256, 128))
    nj = f // tn
    return pl.pallas_call(
        _norm_swiglu_kernel,
        grid=(m // tm, nj),
        in_specs=[pl.BlockSpec((tm, d), lambda i, j: (i, 0), pipeline_mode=pl.Buffered(1)),
                  pl.BlockSpec((1, d), lambda i, j: (0, 0)),
                  pl.BlockSpec((None, d, tn), lambda i, j: (layer, 0, j)),
                  pl.BlockSpec((None, d, tn), lambda i, j: (layer, 0, j + nj))],
        out_specs=pl.BlockSpec((tm, tn), lambda i, j: (i, j)),
        out_shape=jax.ShapeDtypeStruct((m, f), BF16),
        scratch_shapes=[pltpu.VMEM((tm, d), BF16)],
        compiler_params=_cparams(2),
        name="norm_swiglu",
    )(x, g.reshape(1, d), w_gu, w_gu)


def _matmul_residual_kernel(h_ref, w_ref, x_ref, o_ref, *, scale):
    o_ref[...] = x_ref[...] + scale * _dot(h_ref[...], w_ref[...].astype(BF16))


def matmul_residual(h, w, layer, x, scale):
    m, k = h.shape
    n = w.shape[2]
    tm = _pick(m, ROW_TILES)
    tn = _pick(n, (256, 128))
    return pl.pallas_call(
        functools.partial(_matmul_residual_kernel, scale=scale),
        grid=(m // tm, n // tn),
        in_specs=[pl.BlockSpec((tm, k), lambda i, j: (i, 0)),
                  pl.BlockSpec((None, k, tn), lambda i, j: (layer, 0, j)),
                  pl.BlockSpec((tm, tn), lambda i, j: (i, j))],
        out_specs=pl.BlockSpec((tm, tn), lambda i, j: (i, j)),
        out_shape=jax.ShapeDtypeStruct((m, n), F32),
        compiler_params=_cparams(2),
        name="matmul_residual",
    )(h, w, x)


def _norm_matmul_kernel(x_ref, g_ref, w_ref, o_ref, xn_ref):
    @pl.when(pl.program_id(1) == 0)
    def _():
        _rms_to_scratch(x_ref, g_ref, xn_ref)

    o_ref[...] = _dot(xn_ref[...], w_ref[...])


def norm_matmul(x, g, w):
    m, d = x.shape
    n = w.shape[1]
    tm = _pick(m, ROW_TILES)
    tn = _pick(n, (1024, 512, 256, 128))
    return pl.pallas_call(
        _norm_matmul_kernel,
        grid=(m // tm, n // tn),
        in_specs=[pl.BlockSpec((tm, d), lambda i, j: (i, 0)),
                  pl.BlockSpec((1, d), lambda i, j: (0, 0)),
                  pl.BlockSpec((d, tn), lambda i, j: (0, j))],
        out_specs=pl.BlockSpec((tm, tn), lambda i, j: (i, j)),
        out_shape=jax.ShapeDtypeStruct((m, n), F32),
        scratch_shapes=[pltpu.VMEM((tm, d), BF16)],
        compiler_params=_cparams(2),
        name="norm_matmul",
    )(x, g.reshape(1, d), w)


def _merge_kernel(p_ref, s_ref, a_ref, g0_ref, g1_ref, g2_ref, wp_ref, ws_ref, wa_ref, o_ref):
    merged = (_sigmoid(g0_ref[...]) * _dot(p_ref[...], wp_ref[...].astype(BF16))
              + _sigmoid(g1_ref[...]) * _dot(s_ref[...], ws_ref[...].astype(BF16))
              + _sigmoid(g2_ref[...]) * _dot(a_ref[...], wa_ref[...].astype(BF16)))
    o_ref[...] = merged.astype(o_ref.dtype)


def gated_merge(pool_out, ssm_out, att, proj, gate_col, w_pool, w_ssm, w_attn, layer):
    m, kb = pool_out.shape
    d = w_pool.shape[2]
    tm = _pick(m, ROW_TILES)
    tn = _pick(d, (256, 128))
    g_blk = gate_col // tn
    nj = d // tn
    branch = pl.BlockSpec((tm, kb), lambda i, j: (i, 0))
    weight = pl.BlockSpec((None, kb, tn), lambda i, j: (layer, 0, j))

    def gate(r):
        return pl.BlockSpec((tm, tn), lambda i, j: (i, g_blk + r * nj + j))

    return pl.pallas_call(
        _merge_kernel,
        grid=(m // tm, nj),
        in_specs=[branch, branch, branch, gate(0), gate(1), gate(2), weight, weight, weight],
        out_specs=pl.BlockSpec((tm, tn), lambda i, j: (i, j)),
        out_shape=jax.ShapeDtypeStruct((m, d), BF16),
        compiler_params=_cparams(2),
        name="gated_merge",
    )(pool_out, ssm_out, att, proj, proj, proj, w_pool, w_ssm, w_attn)


def _prep_kernel(xp_ref, sm_ref, hist_ref, pw_ref, ps_ref, fb_ref, *refs, rows, pos0, n_valid, with_cum):
    if with_cum:
        pool_ref, logf_ref, ccol_ref, crow_ref, full_ref, carry_ref = refs
    else:
        pool_ref, logf_ref, full_ref = refs
    t = pl.program_id(1)
    hr = POOL_HIST_ROWS

    @pl.when(t == 0)
    def _():
        full_ref[0:hr, :] = hist_ref[...]
        if with_cum:
            carry_ref[...] = jnp.zeros_like(carry_ref)

    x = xp_ref[...]
    full_ref[hr:hr + rows, :] = x
    row = t * rows + lax.broadcasted_iota(jnp.int32, (rows, 1), 0)
    gd = x.shape[1] // len(POOL_WINDOWS)
    for g, w in enumerate(POOL_WINDOWS):
        cols = slice(g * gd, (g + 1) * gd)
        s = full_ref[hr:hr + rows, cols]
        for k in range(1, w):
            s = s + full_ref[hr - k:hr - k + rows, cols]
        cnt = jnp.minimum(w, pos0 + row + 1).astype(F32)
        pooled = s / cnt - x[:, cols]
        mixed = _dot(pooled.astype(BF16), pw_ref[g])
        pool_ref[:, cols] = (mixed * ps_ref[:, cols]).astype(pool_ref.dtype)
    full_ref[0:hr, :] = full_ref[rows:rows + hr, :]

    z = sm_ref[:, F_LANE:F_LANE + logf_ref.shape[1]] + fb_ref[...]
    logf = -_softplus(-z)
    logf_ref[...] = logf
    if with_cum:
        lf = jnp.where(row < n_valid, logf, 0.0)
        r_i = lax.broadcasted_iota(jnp.int32, (rows, rows), 0)
        c_i = lax.broadcasted_iota(jnp.int32, (rows, rows), 1)
        tril = (r_i >= c_i).astype(F32)
        cs = _dot(tril, lf, HIGHEST) + carry_ref[...]
        ccol_ref[...] = cs
        carry_ref[...] = cs[rows - 1:rows, :]
        nh = cs.shape[1]
        wide = jnp.concatenate([cs, jnp.zeros((rows, LANES - nh), F32)], axis=1)
        crow_ref[...] = wide.T[0:nh, :]


def pool_and_forget(proj, cols, hist, pool_w, pool_scale, forget_bias, *, nb, nt, rows, row0, pos0, n_valid,
                    with_cum):
    mtot = nb * nt * rows
    blk0 = row0 // rows
    pd = pool_w.shape[0] * pool_w.shape[1]
    nh = forget_bias.shape[0]
    xp_blk = cols["pool"] // pd
    sm_blk = cols["small"] // SMALL_W
    in_specs = [pl.BlockSpec((rows, pd), lambda b, t: (blk0 + b * nt + t, xp_blk)),
                pl.BlockSpec((rows, SMALL_W), lambda b, t: (blk0 + b * nt + t, sm_blk)),
                pl.BlockSpec((None, POOL_HIST_ROWS, pd), lambda b, t: (b, 0, 0)),
                pl.BlockSpec(pool_w.shape, lambda b, t: (0, 0, 0)),
                pl.BlockSpec((1, pd), lambda b, t: (0, 0)),
                pl.BlockSpec((1, nh), lambda b, t: (0, 0))]
    out_specs = [pl.BlockSpec((rows, pd), lambda b, t: (b * nt + t, 0)),
                 pl.BlockSpec((rows, nh), lambda b, t: (b * nt + t, 0))]
    out_shape = [jax.ShapeDtypeStruct((mtot, pd), BF16), jax.ShapeDtypeStruct((mtot, nh), F32)]
    scratch = [pltpu.VMEM((POOL_HIST_ROWS + rows, pd), F32)]
    if with_cum:
        out_specs += [pl.BlockSpec((rows, nh), lambda b, t: (b * nt + t, 0)),
                      pl.BlockSpec((None, nh, rows), lambda b, t: (b, 0, t))]
        out_shape += [jax.ShapeDtypeStruct((mtot, nh), F32), jax.ShapeDtypeStruct((nb, nh, nt * rows), F32)]
        scratch += [pltpu.VMEM((1, nh), F32)]
    return pl.pallas_call(
        functools.partial(_prep_kernel, rows=rows, pos0=pos0, n_valid=n_valid, with_cum=with_cum),
        grid=(nb, nt),
        in_specs=in_specs, out_specs=out_specs, out_shape=out_shape, scratch_shapes=scratch,
        compiler_params=_cparams(2),
        name="pool_and_forget",
    )(proj, proj, hist, pool_w, pool_scale.reshape(1, pd), forget_bias.reshape(1, nh))


def _ssd_kernel(xz_ref, z_ref, hist_ref, h0_ref, cw_ref, cb_ref, dtb_ref, alog_ref, dskip_ref, nw_ref,
                ex_ref, ext_ref, o_ref, h_ref, full_ref, y_ref, *, rows, n_valid, d_inner, n_state, head_dim):
    t = SEQ_TILE
    c = pl.program_id(1)
    hr = CONV_HIST_ROWS
    cdim = full_ref.shape[1]
    nheads = d_inner // head_dim
    ngroups = (cdim - d_inner) // (2 * n_state)
    pair_w = 2 * head_dim
    pairs_per_group = d_inner // pair_w // ngroups

    @pl.when(c == 0)
    def _():
        full_ref[0:hr, :] = hist_ref[...]
        h_ref[...] = h0_ref[...]

    full_ref[hr:hr + rows, :] = xz_ref[:, 0:cdim]
    if rows < t:
        full_ref[hr + rows:hr + t, :] = jnp.zeros((t - rows, cdim), F32)
    conv = cb_ref[...]
    for k in range(CONV_TAPS):
        off = hr - (CONV_TAPS - 1) + k
        conv = conv + full_ref[off:off + t, :] * cw_ref[k:k + 1, :]
    act = conv * _sigmoid(conv)
    full_ref[0:hr, :] = full_ref[t:t + hr, :]

    dt_raw = xz_ref[:, cdim + DT_LANE:cdim + DT_LANE + nheads]
    if rows < t:
        dt_raw = jnp.concatenate([dt_raw, jnp.zeros((t - rows, nheads), F32)], axis=0)
    row = c * t + lax.broadcasted_iota(jnp.int32, (t, 1), 0)
    dt = jnp.where(row < n_valid, _softplus(dt_raw + dtb_ref[...]), 0.0)
    a = -jnp.exp(alog_ref[...])
    r_i = lax.broadcasted_iota(jnp.int32, (t, t), 0)
    c_i = lax.broadcasted_iota(jnp.int32, (t, t), 1)
    causal = r_i >= c_i
    acum = _dot(causal.astype(F32), dt * a, HIGHEST)
    lane_pad = jnp.zeros((t, LANES - nheads), F32)
    acum_row = jnp.concatenate([acum, lane_pad], axis=1).T[0:nheads, :]
    dt_row = jnp.concatenate([dt, lane_pad], axis=1).T[0:nheads, :]
    a_last = acum[t - 1:t, :]
    decay_in = jnp.exp(acum)
    to_end = jnp.exp(a_last - acum) * dt
    wide = _dot(jnp.concatenate([decay_in, to_end], axis=0), ex_ref[...], HIGHEST)
    decay_in_x = wide[0:t, :]
    to_end_x = wide[t:2 * t, :]
    chunk_decay = jnp.broadcast_to(jnp.exp(acum_row[:, t - 1:t]), (nheads, n_state))
    chunk_decay_x = _dot(ext_ref[...], chunk_decay, HIGHEST)

    xs = act[:, 0:d_inner]
    xs_b = xs.astype(BF16)
    lane = lax.broadcasted_iota(jnp.int32, (t, pair_w), 1)
    first = lane < head_dim
    for g in range(ngroups):
        bm = act[:, d_inner + g * n_state:d_inner + (g + 1) * n_state].astype(BF16)
        cm = act[:, d_inner + (ngroups + g) * n_state:d_inner + (ngroups + g + 1) * n_state].astype(BF16)
        cb = _dot_nt(cm, bm)
        for j in range(pairs_per_group):
            p = g * pairs_per_group + j
            cols = slice(p * pair_w, (p + 1) * pair_w)
            wts = []
            for h in (2 * p, 2 * p + 1):
                seg = acum[:, h:h + 1] - acum_row[h:h + 1, :]
                decay = jnp.exp(jnp.where(causal, seg, -jnp.inf))
                wts.append((cb * decay) * dt_row[h:h + 1, :])
            lhs = jnp.concatenate(wts, axis=1).astype(BF16)
            xp = xs_b[:, cols]
            zero = jnp.zeros_like(xp)
            rhs = jnp.concatenate([jnp.where(first, xp, zero), jnp.where(first, zero, xp)], axis=0)
            y = _dot(lhs, rhs)
            hp = h_ref[p * pair_w:(p + 1) * pair_w, :]
            y = y + _dot_nt(cm, hp.astype(BF16)) * decay_in_x[:, cols]
            y_ref[:, cols] = y + dskip_ref[:, cols] * xs[:, cols]
            upd = (xs[:, cols] * to_end_x[:, cols]).T.astype(BF16)
            h_ref[p * pair_w:(p + 1) * pair_w, :] = chunk_decay_x[p * pair_w:(p + 1) * pair_w, :] * hp + _dot(upd, bm)

    zz = z_ref[...]
    if rows < t:
        zz = jnp.concatenate([zz, jnp.zeros((t - rows, d_inner), F32)], axis=0)
    gated = y_ref[...] * (zz * _sigmoid(zz))
    ms = jnp.mean(gated * gated, axis=-1, keepdims=True)
    out = (gated * lax.rsqrt(ms + NORM_EPS)) * nw_ref[...]
    o_ref[...] = out[0:rows, :].astype(o_ref.dtype)


def ssd_mixer(proj, cols, hist, h0, conv_w, conv_b, dt_bias, a_log, d_skip, ssm_norm, *, nb, nt, rows, row0,
              n_valid):
    mtot = nb * nt * rows
    blk0 = row0 // rows
    cdim = conv_w.shape[1]
    nheads = dt_bias.shape[0]
    d_inner, n_state = h0.shape[1], h0.shape[2]
    head_dim = d_inner // nheads
    xz_w = cdim + SMALL_W
    expand = jnp.repeat(jnp.eye(nheads, dtype=F32), head_dim, axis=1)
    const2 = lambda b, t: (0, 0)
    return pl.pallas_call(
        functools.partial(_ssd_kernel, rows=rows, n_valid=n_valid, d_inner=d_inner, n_state=n_state,
                          head_dim=head_dim),
        grid=(nb, nt),
        in_specs=[pl.BlockSpec((rows, xz_w), lambda b, t: (blk0 + b * nt + t, cols["xbc"] // xz_w)),
                  pl.BlockSpec((rows, d_inner), lambda b, t: (blk0 + b * nt + t, cols["z"] // d_inner)),
                  pl.BlockSpec((None, CONV_HIST_ROWS, cdim), lambda b, t: (b, 0, 0)),
                  pl.BlockSpec((None, d_inner, n_state), lambda b, t: (b, 0, 0)),
                  pl.BlockSpec((CONV_TAPS, cdim), const2),
                  pl.BlockSpec((1, cdim), const2),
                  pl.BlockSpec((1, nheads), const2),
                  pl.BlockSpec((1, nheads), const2),
                  pl.BlockSpec((1, d_inner), const2),
                  pl.BlockSpec((1, d_inner), const2),
                  pl.BlockSpec((nheads, d_inner), const2),
                  pl.BlockSpec((d_inner, nheads), const2)],
        out_specs=[pl.BlockSpec((rows, d_inner), lambda b, t: (b * nt + t, 0)),
                   pl.BlockSpec((None, d_inner, n_state), lambda b, t: (b, 0, 0))],
        out_shape=[jax.ShapeDtypeStruct((mtot, d_inner), BF16),
                   jax.ShapeDtypeStruct((nb, d_inner, n_state), F32)],
        scratch_shapes=[pltpu.VMEM((CONV_HIST_ROWS + SEQ_TILE, cdim), F32),
                        pltpu.VMEM((SEQ_TILE, d_inner), F32)],
        compiler_params=_cparams(2),
        name="ssd_mixer",
    )(proj, proj, hist, h0, conv_w, conv_b.reshape(1, cdim), dt_bias.reshape(1, nheads),
      a_log.reshape(1, nheads), jnp.repeat(d_skip, head_dim).reshape(1, d_inner),
      ssm_norm.reshape(1, d_inner), expand, expand.T)


def _split3(x):
    hi = x.astype(BF16).astype(F32)
    r = x - hi
    mid = r.astype(BF16).astype(F32)
    lo = (r - mid).astype(BF16).astype(F32)
    return hi, mid, lo


def _attn_kernel(q_ref, k_ref, v_ref, ccol_ref, crow_ref, o_ref, kaug_ref, vt_ref, acc_ref, s_ref, *, tq, head_dim, lp):
    pair = pl.program_id(1)
    qi = pl.program_id(2)
    nh = ccol_ref.shape[1]
    pw = 2 * head_dim
    tk = tq
    n_parts = 3
    one_lane = 2 * n_parts

    @pl.when(qi == 0)
    def _():
        hid = lax.broadcasted_iota(jnp.int32, (nh, pw), 0)
        lid = lax.broadcasted_iota(jnp.int32, (nh, pw), 1)
        sel = jnp.where((hid == 2 * pair) & (lid < n_parts), 1.0,
                        jnp.where((hid == 2 * pair + 1) & (lid >= n_parts) & (lid < one_lane), 1.0, 0.0))
        lane = lax.broadcasted_iota(jnp.int32, (SEQ_TILE, pw), 1)
        part = lane - jnp.where(lane >= n_parts, n_parts, 0)

        def body(i, carry):
            r = pl.multiple_of(i * SEQ_TILE, SEQ_TILE)
            c2 = _dot(ccol_ref[pl.ds(r, SEQ_TILE), :], sel, HIGHEST) * (-LOG2E)
            hi, mid, lo = _split3(c2)
            bias = jnp.where(part == 0, hi, jnp.where(part == 1, mid, lo))
            bias = jnp.where(lane < one_lane, bias, jnp.where(lane < one_lane + n_parts, 1.0, 0.0))
            kaug_ref[pl.ds(r, SEQ_TILE), 0:pw] = k_ref[pl.ds(r, SEQ_TILE), :].astype(BF16)
            kaug_ref[pl.ds(r, SEQ_TILE), pw:2 * pw] = bias.astype(BF16)
            vt_ref[:, pl.ds(r, SEQ_TILE)] = v_ref[pl.ds(r, SEQ_TILE), :].T.astype(BF16)
            return carry

        lax.fori_loop(0, lp // SEQ_TILE, body, 0)

    qs = pl.multiple_of(qi * tq, tq)
    qt = (q_ref[...] * (head_dim ** -0.5 * LOG2E)).T
    row = lax.broadcasted_iota(jnp.int32, (pw, tq), 0)
    q_aug = []
    for e in range(2):
        head_rows = (row < head_dim) if e == 0 else (row >= head_dim)
        c_q = crow_ref[pl.ds(2 * pair + e, 1), pl.ds(qs, tq)] * LOG2E
        hi, mid, lo = _split3(c_q)
        ones_rows = (row >= n_parts * e) & (row < n_parts * (e + 1))
        qb = jnp.where(ones_rows, 1.0,
                       jnp.where(row == one_lane, hi, jnp.where(row == one_lane + 1, mid,
                                                                jnp.where(row == one_lane + 2, lo, 0.0))))
        q_aug.append(jnp.concatenate([jnp.where(head_rows, qt, 0.0), qb], axis=0).astype(BF16))

    acc_ref[...] = jnp.zeros_like(acc_ref)
    q_pos = qs + lax.broadcasted_iota(jnp.int32, (1, tq), 1)

    def scores(kt):
        ks = pl.multiple_of(kt * tk, tk)
        ka = kaug_ref[pl.ds(ks, tk), :]
        return [_dot(ka, q_aug[e]) for e in range(2)]

    def consume(kt, s_pair, carry, masked):
        ks = pl.multiple_of(kt * tk, tk)
        vt = vt_ref[:, pl.ds(ks, tk)]
        new = []
        for e in range(2):
            m_old, l_old = carry[2 * e], carry[2 * e + 1]
            s = s_pair[e]
            if masked:
                k_pos = ks + lax.broadcasted_iota(jnp.int32, (tk, 1), 0)
                s = jnp.where(k_pos <= q_pos, s, NEG_INF)
            m_new = jnp.maximum(m_old, jnp.max(s, axis=0, keepdims=True))
            alpha = jnp.exp2(m_old - m_new)
            p = jnp.exp2(s - m_new)
            l_new = alpha * l_old + jnp.sum(p, axis=0, keepdims=True)
            acc_ref[e] = alpha * acc_ref[e] + _dot(vt, p.astype(BF16))
            new += [m_new, l_new]
        return tuple(new)

    def body(kt, carry):
        s_cur = [s_ref[0], s_ref[1]]
        s_next = scores(kt + 1)
        carry = consume(kt, s_cur, carry, False)
        s_ref[0] = s_next[0]
        s_ref[1] = s_next[1]
        return carry

    s_first = scores(0)
    s_ref[0] = s_first[0]
    s_ref[1] = s_first[1]
    init = (jnp.full((1, tq), NEG_INF, F32), jnp.zeros((1, tq), F32)) * 2
    carry = lax.fori_loop(0, qi, body, init)
    carry = consume(qi, [s_ref[0], s_ref[1]], carry, True)
    out_t = jnp.where(row < head_dim, acc_ref[0] / carry[1], acc_ref[1] / carry[3])
    o_ref[...] = out_t.T.astype(o_ref.dtype)


def prompt_attention(proj, cols, ccol, crow, *, nb, lp, head_dim):
    mtot = nb * lp
    nh = ccol.shape[1]
    pw = 2 * head_dim
    tq = _pick(lp, (384, 256, 128))
    nq = lp // tq
    return pl.pallas_call(
        functools.partial(_attn_kernel, tq=tq, head_dim=head_dim, lp=lp),
        grid=(nb, nh // 2, nq),
        in_specs=[pl.BlockSpec((tq, pw), lambda b, p, i: (b * nq + i, cols["q"] // pw + p)),
                  pl.BlockSpec((lp, pw), lambda b, p, i: (b, cols["k"] // pw + p)),
                  pl.BlockSpec((lp, pw), lambda b, p, i: (b, cols["v"] // pw + p)),
                  pl.BlockSpec((lp, nh), lambda b, p, i: (b, 0)),
                  pl.BlockSpec((None, nh, lp), lambda b, p, i: (b, 0, 0))],
        out_specs=pl.BlockSpec((tq, pw), lambda b, p, i: (b * nq + i, p)),
        out_shape=jax.ShapeDtypeStruct((mtot, nh * head_dim), BF16),
        scratch_shapes=[pltpu.VMEM((lp, 2 * pw), BF16), pltpu.VMEM((pw, lp), BF16), pltpu.VMEM((2, pw, tq), F32),
                        pltpu.VMEM((2, tq, tq), F32)],
        compiler_params=_cparams(3),
        name="prompt_attention",
    )(proj, proj, proj, ccol, crow)


def _decode_attn_kernel(pt_ref, q_ref, kn_ref, vn_ref, lfn_ref, *refs, pages):
    k_refs = refs[0:pages]
    v_refs = refs[pages:2 * pages]
    lf_refs = refs[2 * pages:3 * pages]
    o_ref, qb_ref, s_ref, p_ref, a_ref, m_ref, l_ref, carry_ref, acc_ref = refs[3 * pages:]
    s_id = pl.program_id(1)
    nh, hd, psz = k_refs[0].shape

    @pl.when(s_id == 0)
    def _():
        qb_ref[...] = q_ref[...] * hd ** -0.5
        lane = lax.broadcasted_iota(jnp.int32, (1, psz), 1)
        for h in range(nh):
            s_ref[h:h + 1, :] = jnp.sum(kn_ref[h] * qb_ref[h], axis=0, keepdims=True)
            acc_ref[h] = jnp.where(lane == 0, vn_ref[h], 0.0)
        m_ref[...] = s_ref[:, 0:1]
        l_ref[...] = jnp.ones_like(l_ref)
        carry_ref[...] = lfn_ref[:, 0:1]

    r_i = lax.broadcasted_iota(jnp.int32, (psz, psz), 0)
    c_i = lax.broadcasted_iota(jnp.int32, (psz, psz), 1)
    later = (r_i > c_i).astype(BF16)
    for i in range(pages):
        for h in range(nh):
            s_ref[h:h + 1, :] = jnp.sum(k_refs[i][h] * qb_ref[h], axis=0, keepdims=True)
        lf = lf_refs[i][...]
        hi, mid, lo = _split3(lf)
        within = (_dot(hi.astype(BF16), later) + _dot(mid.astype(BF16), later)) + _dot(lo.astype(BF16), later)
        carry = carry_ref[...]
        s = s_ref[...] + (within + carry)
        carry_ref[...] = carry + jnp.sum(lf, axis=1, keepdims=True)
        m_old = m_ref[...]
        m_new = jnp.maximum(m_old, jnp.max(s, axis=1, keepdims=True))
        alpha = jnp.exp(m_old - m_new)
        p = jnp.exp(s - m_new)
        l_ref[...] = alpha * l_ref[...] + jnp.sum(p, axis=1, keepdims=True)
        m_ref[...] = m_new
        p_ref[...] = p
        a_ref[...] = jnp.broadcast_to(alpha, (nh, psz))
        for h in range(nh):
            acc_ref[h] = a_ref[h:h + 1, :] * acc_ref[h] + v_refs[i][h] * p_ref[h:h + 1, :]

    @pl.when(s_id == pl.num_programs(1) - 1)
    def _():
        for h in range(nh):
            o_h = jnp.sum(acc_ref[h], axis=1, keepdims=True) / l_ref[h:h + 1, :]
            o_ref[h] = jnp.broadcast_to(o_h, (hd, psz))


def decode_attention(q, k_new, v_new, logf_new, cache_k, cache_v, cache_logf, page_table, layer):
    nb, n_pages = page_table.shape
    _, _, psz, nh, hd = cache_k.shape
    pages = _pick(n_pages, (8, 4, 2, 1))
    kt = jnp.transpose(cache_k, (0, 1, 3, 4, 2))
    vt = jnp.transpose(cache_v, (0, 1, 3, 4, 2))
    lft = jnp.transpose(cache_logf, (0, 1, 3, 2))
    rep = lambda x: jnp.broadcast_to(x[..., None], x.shape + (psz,))

    def page(i, dims):
        def index(b, s, pt):
            return (pt[b, n_pages - 1 - (s * pages + i)], layer) + (0,) * len(dims)
        return pl.BlockSpec((None, None) + dims, index)

    tok = pl.BlockSpec((None, nh, hd, psz), lambda b, s, pt: (b, 0, 0, 0))
    grid_spec = pltpu.PrefetchScalarGridSpec(
        num_scalar_prefetch=1,
        grid=(nb, n_pages // pages),
        in_specs=([tok, tok, tok, pl.BlockSpec((None, nh, psz), lambda b, s, pt: (b, 0, 0))]
                  + [page(i, (nh, hd, psz)) for i in range(pages)]
                  + [page(i, (nh, hd, psz)) for i in range(pages)]
                  + [page(i, (nh, psz)) for i in range(pages)]),
        out_specs=tok,
        scratch_shapes=[pltpu.VMEM((nh, hd, psz), F32), pltpu.VMEM((nh, psz), F32), pltpu.VMEM((nh, psz), F32),
                        pltpu.VMEM((nh, psz), F32), pltpu.VMEM((nh, 1), F32), pltpu.VMEM((nh, 1), F32),
                        pltpu.VMEM((nh, 1), F32), pltpu.VMEM((nh, hd, psz), F32)],
    )
    out = pl.pallas_call(
        functools.partial(_decode_attn_kernel, pages=pages),
        grid_spec=grid_spec,
        out_shape=jax.ShapeDtypeStruct((nb, nh, hd, psz), F32),
        compiler_params=_cparams(2),
        name="decode_attention",
    )(page_table, rep(q), rep(k_new), rep(v_new), rep(logf_new),
      *([kt] * pages), *([vt] * pages), *([lft] * pages))
    return out[..., 0]


def _final_norm_shift_kernel(a_ref, b_ref, g_ref, o_ref, *, shift):
    x = jnp.concatenate([a_ref[shift:, :], b_ref[0:shift, :]], axis=0)
    ms = jnp.mean(x * x, axis=-1, keepdims=True)
    o_ref[...] = (x * lax.rsqrt(ms + NORM_EPS)) * g_ref[...]


def final_norm_prompt(x, g, *, nb, lp, seq, shift):
    d = x.shape[1]
    tr = SEQ_TILE
    nt = seq // tr
    nl = lp // tr
    return pl.pallas_call(
        functools.partial(_final_norm_shift_kernel, shift=shift),
        grid=(nb, nt),
        in_specs=[pl.BlockSpec((tr, d), lambda b, t: (b * nl + t, 0)),
                  pl.BlockSpec((tr, d), lambda b, t: (b * nl + t + 1, 0)),
                  pl.BlockSpec((1, d), lambda b, t: (0, 0))],
        out_specs=pl.BlockSpec((None, tr, d), lambda b, t: (b, t, 0)),
        out_shape=jax.ShapeDtypeStruct((nb, seq, d), F32),
        compiler_params=_cparams(2),
        name="final_norm_prompt",
    )(x, x, g.reshape(1, d))


def _final_norm_kernel(x_ref, g_ref, o_ref):
    x = x_ref[...]
    ms = jnp.mean(x * x, axis=-1, keepdims=True)
    o_ref[...] = (x * lax.rsqrt(ms + NORM_EPS)) * g_ref[...]


def final_norm(x, g):
    m, d = x.shape
    return pl.pallas_call(
        _final_norm_kernel,
        grid=(1,),
        in_specs=[pl.BlockSpec((m, d), lambda i: (0, 0)), pl.BlockSpec((1, d), lambda i: (0, 0))],
        out_specs=pl.BlockSpec((m, d), lambda i: (0, 0)),
        out_shape=jax.ShapeDtypeStruct((m, d), F32),
        compiler_params=_cparams(1),
        name="final_norm",
    )(x, g.reshape(1, d))


def _pack_w_in(w_in, dims):
    d = w_in.shape[0]
    pd, di, cd, nh_s, ad, nh_a, dm = dims
    off = {}
    o = 0
    for name, width in (("pool", pd), ("z", di), ("xbc", cd), ("dt", nh_s), ("q", ad), ("k", ad), ("v", ad),
                        ("f", nh_a), ("gate", w_in.shape[1] - (pd + di + cd + nh_s + 3 * ad + nh_a))):
        off[name] = (o, width)
        o += width

    def seg(name):
        s, w = off[name]
        return w_in[:, s:s + w]

    assert DT_LANE == 0 and nh_s <= F_LANE
    zeros = lambda w: jnp.zeros((d, w), w_in.dtype)
    small = jnp.concatenate([seg("dt"), zeros(F_LANE - nh_s), seg("f"), zeros(SMALL_W - F_LANE - nh_a)], axis=1)
    parts = [("xbc", seg("xbc")), ("small", small), ("pool", seg("pool")), ("z", seg("z")), ("q", seg("q")),
             ("k", seg("k")), ("v", seg("v")), ("gate", seg("gate"))]
    cols = {}
    o = 0
    for name, arr in parts:
        cols[name] = o
        o += arr.shape[1]
    return jnp.concatenate([a for _, a in parts], axis=1).astype(BF16), cols


def kernel(x_prompt, x_sample, cache_k, cache_v, cache_logf, page_table, state_ssm, state_conv, state_pool,
           meta_tokens, norm_ffn1, ffn1_w_gu, ffn1_w_down, norm_mix, w_in, pool_w, pool_scale, conv_w, conv_b,
           dt_bias, a_log, d_skip, ssm_norm, forget_bias, w_br_pool, w_br_ssm, w_br_attn, w_out, norm_ffn2,
           ffn2_w_gu, ffn2_w_down, final_norm_w):
    nb, seq, dm = x_prompt.shape
    ns = x_sample.shape[0]
    depth = w_in.shape[0]
    n_meta = meta_tokens.shape[0]
    nh_a, hd_a = cache_k.shape[3], cache_k.shape[4]
    ad = nh_a * hd_a
    psz = cache_k.shape[2]
    past_len = page_table.shape[1] * psz
    nh_s, hd_s, n_state = state_ssm.shape[2], state_ssm.shape[3], state_ssm.shape[4]
    di = nh_s * hd_s
    cd = state_conv.shape[3]
    conv_hist = state_conv.shape[2]
    pd = state_pool.shape[3]
    pool_hist = state_pool.shape[2]
    l = seq + n_meta
    lp = -(-l // SEQ_TILE) * SEQ_TILE
    assert x_sample.shape[1] == 1 and seq % SEQ_TILE == 0 and n_meta % 8 == 0 and n_meta < SEQ_TILE
    assert 2 * hd_a == LANES and 2 * hd_s == LANES and n_state == LANES
    assert pool_hist < POOL_HIST_ROWS and conv_hist < CONV_HIST_ROWS

    mp = nb * lp
    x = jnp.concatenate(
        [jnp.concatenate([jnp.broadcast_to(meta_tokens[None], (nb, n_meta, dm)), x_prompt,
                          jnp.zeros((nb, lp - l, dm), F32)], axis=1).reshape(mp, dm),
         jnp.pad(x_sample, ((0, 0), (0, DEC_ROWS - 1), (0, 0))).reshape(ns * DEC_ROWS, dm)], axis=0)

    zero_pool_hist = jnp.zeros((nb, POOL_HIST_ROWS, pd), F32)
    zero_conv_hist = jnp.zeros((nb, CONV_HIST_ROWS, cd), F32)
    zero_state = jnp.zeros((nb, di, n_state), F32)
    pool_w_b = pool_w.astype(BF16)

    outs = {k: [] for k in ("kp", "vp", "lfp", "hp", "cp", "pp", "ks", "vs", "lfs", "hs", "cs", "ps")}
    for i in range(depth):
        w_in_i, cols = _pack_w_in(w_in[i], (pd, di, cd, nh_s, ad, nh_a, dm))
        pw = pool_w_b[i]

        x = matmul_residual(norm_swiglu(x, norm_ffn1[i], ffn1_w_gu, i), ffn1_w_down, i, x, 0.5)
        proj = norm_matmul(x, norm_mix[i], w_in_i)

        pool_p, lf_p, ccol, crow = pool_and_forget(
            proj, cols, zero_pool_hist, pw, pool_scale[i], forget_bias[i],
            nb=nb, nt=lp // SEQ_TILE, rows=SEQ_TILE, row0=0, pos0=0, n_valid=l, with_cum=True)
        ssm_p, h_p = ssd_mixer(proj, cols, zero_conv_hist, zero_state, conv_w[i], conv_b[i], dt_bias[i], a_log[i],
                               d_skip[i], ssm_norm[i], nb=nb, nt=lp // SEQ_TILE, rows=SEQ_TILE, row0=0, n_valid=l)
        att_p = prompt_attention(proj, cols, ccol, crow, nb=nb, lp=lp, head_dim=hd_a)

        hist_p = jnp.pad(state_pool[:, i], ((0, 0), (POOL_HIST_ROWS - pool_hist, 0), (0, 0)))
        hist_c = jnp.pad(state_conv[:, i], ((0, 0), (CONV_HIST_ROWS - conv_hist, 0), (0, 0)))
        pool_s, lf_s = pool_and_forget(
            proj, cols, hist_p, pw, pool_scale[i], forget_bias[i],
            nb=ns, nt=1, rows=DEC_ROWS, row0=mp, pos0=past_len, n_valid=1, with_cum=False)
        ssm_s, h_s = ssd_mixer(proj, cols, hist_c, state_ssm[:, i].reshape(ns, di, n_state), conv_w[i], conv_b[i],
                               dt_bias[i], a_log[i], d_skip[i], ssm_norm[i],
                               nb=ns, nt=1, rows=DEC_ROWS, row0=mp, n_valid=1)
        ps0 = proj[mp:].reshape(ns, DEC_ROWS, -1)[:, 0:1]
        heads = lambda name: ps0[:, 0, cols[name]:cols[name] + ad].reshape(ns, nh_a, hd_a)
        att_s = decode_attention(heads("q"), heads("k"), heads("v"), lf_s.reshape(ns, DEC_ROWS, nh_a)[:, 0],
                                 cache_k, cache_v, cache_logf, page_table, i)
        att_s = jnp.pad(att_s.reshape(ns, 1, ad), ((0, 0), (0, DEC_ROWS - 1), (0, 0)))
        att_s = att_s.reshape(ns * DEC_ROWS, ad).astype(BF16)

        merged = gated_merge(jnp.concatenate([pool_p, pool_s]), jnp.concatenate([ssm_p, ssm_s]),
                             jnp.concatenate([att_p, att_s]), proj, cols["gate"], w_br_pool, w_br_ssm, w_br_attn, i)
        x = matmul_residual(merged, w_out, i, x, 1.0)
        x = matmul_residual(norm_swiglu(x, norm_ffn2[i], ffn2_w_gu, i), ffn2_w_down, i, x, 0.5)

        def prompt_rows(first, last, name, width):
            return jnp.stack([proj[b * lp + first:b * lp + last, cols[name]:cols[name] + width] for b in range(nb)])

        outs["kp"].append(prompt_rows(0, l, "k", ad).reshape(nb, l, nh_a, hd_a))
        outs["vp"].append(prompt_rows(0, l, "v", ad).reshape(nb, l, nh_a, hd_a))
        outs["lfp"].append(lf_p.reshape(nb, lp, nh_a)[:, :l])
        outs["hp"].append(h_p.reshape(nb, nh_s, hd_s, n_state))
        outs["cp"].append(prompt_rows(l - conv_hist, l, "xbc", cd))
        outs["pp"].append(prompt_rows(l - pool_hist, l, "pool", pd))
        outs["ks"].append(ps0[:, :, cols["k"]:cols["k"] + ad].reshape(ns, 1, nh_a, hd_a))
        outs["vs"].append(ps0[:, :, cols["v"]:cols["v"] + ad].reshape(ns, 1, nh_a, hd_a))
        outs["lfs"].append(lf_s.reshape(ns, DEC_ROWS, nh_a)[:, 0:1])
        outs["hs"].append(h_s.reshape(ns, nh_s, hd_s, n_state))
        outs["cs"].append(jnp.concatenate([state_conv[:, i, 1:], ps0[:, :, cols["xbc"]:cols["xbc"] + cd]], axis=1))
        outs["ps"].append(jnp.concatenate([state_pool[:, i, 1:], ps0[:, :, cols["pool"]:cols["pool"] + pd]], axis=1))

    y_prompt = final_norm_prompt(x, final_norm_w, nb=nb, lp=lp, seq=seq, shift=n_meta)
    y_sample = final_norm(x[mp:], final_norm_w).reshape(ns, DEC_ROWS, dm)[:, 0:1]
    st = lambda k: jnp.stack(outs[k], axis=1)
    return (y_prompt, y_sample, st("kp"), st("vp"), st("lfp"), st("ks"), st("vs"), st("lfs"),
            st("hp"), st("hs"), st("cp"), st("cs"), st("pp"), st("ps"))
```

```python
import functools
import math

import jax
import jax.numpy as jnp
from jax import lax
from jax.experimental import pallas as pl
from jax.experimental.pallas import tpu as pltpu

F32 = jnp.float32
BF16 = jnp.bfloat16

NORM_EPS = 1e-6
NEG_INF = -1e30
LOG2E = 1.4426950408889634
POOL_WINDOWS = (2, 4, 8, 16)
POOL_HIST_ROWS = 16
CONV_TAPS = 4
CONV_HIST_ROWS = 8
SEQ_TILE = 128
DEC_ROWS = 16
LANES = 128
SUBLANES = 8
SMALL_W = 512
DT_LANE = 0
F_LANE = 128
VMEM_LIMIT_BYTES = 56 * 1024 * 1024
ROW_TILES = (1072, 1056, 768, 704, 528, 512, 384, 256, 128)


def _cparams(n_axes):
    return pltpu.CompilerParams(dimension_semantics=("arbitrary",) * n_axes,
                                vmem_limit_bytes=VMEM_LIMIT_BYTES)


def _pick(n, cands):
    for c in cands:
        if n % c == 0:
            return c
    return n


def _sigmoid(x):
    return 1.0 / (1.0 + jnp.exp(-x))


def _softplus(x):
    return jnp.maximum(x, 0.0) + jnp.log(1.0 + jnp.exp(-jnp.abs(x)))


def _dot(a, b):
    return jnp.dot(a, b, preferred_element_type=F32)


def _dot_nt(a, b):
    return lax.dot_general(a, b, (((1,), (1,)), ((), ())), preferred_element_type=F32)


def _split3(x):
    hi = x.astype(BF16).astype(F32)
    r = x - hi
    mid = r.astype(BF16).astype(F32)
    lo = (r - mid).astype(BF16).astype(F32)
    return hi, mid, lo


def _select_dot(sel, x):
    s = sel.astype(BF16)
    hi, mid, lo = _split3(x)
    return (_dot(s, hi.astype(BF16)) + _dot(s, mid.astype(BF16))) + _dot(s, lo.astype(BF16))


def _dot_select(x, sel):
    s = sel.astype(BF16)
    hi, mid, lo = _split3(x)
    return (_dot(hi.astype(BF16), s) + _dot(mid.astype(BF16), s)) + _dot(lo.astype(BF16), s)


def _rms_to_scratch(x_ref, g_ref, xn_ref):
    x = x_ref[...]
    ms = jnp.mean(x * x, axis=-1, keepdims=True)
    xn_ref[...] = ((x * lax.rsqrt(ms + NORM_EPS)) * g_ref[...]).astype(BF16)


def _norm_swiglu_kernel(x_ref, g_ref, wg_ref, wu_ref, o_ref, xn_ref):
    @pl.when(pl.program_id(1) == 0)
    def _():
        _rms_to_scratch(x_ref, g_ref, xn_ref)

    xn = xn_ref[...]
    gate = _dot(xn, wg_ref[...].astype(BF16))
    up = _dot(xn, wu_ref[...].astype(BF16))
    o_ref[...] = ((gate * _sigmoid(gate)) * up).astype(o_ref.dtype)


def norm_swiglu(x, g, w_gu, layer):
    m, d = x.shape
    f = w_gu.shape[2] // 2
    tm = _pick(m, ROW_TILES)
    tn = _pick(f, (512, 256, 128))
    nj = f // tn
    return pl.pallas_call(
        _norm_swiglu_kernel,
        grid=(m // tm, nj),
        in_specs=[pl.BlockSpec((tm, d), lambda i, j: (i, 0)),
                  pl.BlockSpec((1, d), lambda i, j: (0, 0)),
                  pl.BlockSpec((None, d, tn), lambda i, j: (layer, 0, j)),
                  pl.BlockSpec((None, d, tn), lambda i, j: (layer, 0, j + nj))],
        out_specs=pl.BlockSpec((tm, tn), lambda i, j: (i, j)),
        out_shape=jax.ShapeDtypeStruct((m, f), BF16),
        scratch_shapes=[pltpu.VMEM((tm, d), BF16)],
        compiler_params=_cparams(2),
        name="norm_swiglu",
    )(x, g.reshape(1, d), w_gu, w_gu)


def _matmul_residual_kernel(h_ref, w_ref, x_ref, o_ref, *, scale):
    o_ref[...] = x_ref[...] + scale * _dot(h_ref[...], w_ref[...].astype(BF16))


def matmul_residual(h, w, layer, x, scale):
    m, k = h.shape
    n = w.shape[2]
    tm = _pick(m, ROW_TILES)
    tn = _pick(n, (256, 128))
    return pl.pallas_call(
        functools.partial(_matmul_residual_kernel, scale=scale),
        grid=(m // tm, n // tn),
        in_specs=[pl.BlockSpec((tm, k), lambda i, j: (i, 0)),
                  pl.BlockSpec((None, k, tn), lambda i, j: (layer, 0, j)),
                  pl.BlockSpec((tm, tn), lambda i, j: (i, j))],
        out_specs=pl.BlockSpec((tm, tn), lambda i, j: (i, j)),
        out_shape=jax.ShapeDtypeStruct((m, n), F32),
        compiler_params=_cparams(2),
        name="matmul_residual",
    )(h, w, x)


def _norm_matmul_kernel(x_ref, g_ref, w_ref, o_ref, xn_ref):
    @pl.when(pl.program_id(1) == 0)
    def _():
        _rms_to_scratch(x_ref, g_ref, xn_ref)

    o_ref[...] = _dot(xn_ref[...], w_ref[...])


def norm_matmul(x, g, w, layer):
    m, d = x.shape
    n = w.shape[2]
    tm = _pick(m, ROW_TILES)
    tn = _pick(n, (1024, 512, 256, 128))
    return pl.pallas_call(
        _norm_matmul_kernel,
        grid=(m // tm, n // tn),
        in_specs=[pl.BlockSpec((tm, d), lambda i, j: (i, 0)),
                  pl.BlockSpec((1, d), lambda i, j: (0, 0)),
                  pl.BlockSpec((None, d, tn), lambda i, j: (layer, 0, j))],
        out_specs=pl.BlockSpec((tm, tn), lambda i, j: (i, j)),
        out_shape=jax.ShapeDtypeStruct((m, n), F32),
        scratch_shapes=[pltpu.VMEM((tm, d), BF16)],
        compiler_params=_cparams(2),
        name="norm_matmul",
    )(x, g.reshape(1, d), w)


def _merge_kernel(p_ref, s_ref, a_ref, g0_ref, g1_ref, g2_ref, wp_ref, ws_ref, wa_ref, o_ref):
    merged = (_sigmoid(g0_ref[...]) * _dot(p_ref[...], wp_ref[...].astype(BF16))
              + _sigmoid(g1_ref[...]) * _dot(s_ref[...], ws_ref[...].astype(BF16))
              + _sigmoid(g2_ref[...]) * _dot(a_ref[...], wa_ref[...].astype(BF16)))
    o_ref[...] = merged.astype(o_ref.dtype)


def gated_merge(pool_out, ssm_out, att, proj, gate_col, w_pool, w_ssm, w_attn, layer):
    m, kb = pool_out.shape
    d = w_pool.shape[2]
    tm = _pick(m, ROW_TILES)
    tn = _pick(d, (256, 128))
    g_blk = gate_col // tn
    nj = d // tn
    branch = pl.BlockSpec((tm, kb), lambda i, j: (i, 0))
    weight = pl.BlockSpec((None, kb, tn), lambda i, j: (layer, 0, j))

    def gate(r):
        return pl.BlockSpec((tm, tn), lambda i, j: (i, g_blk + r * nj + j))

    return pl.pallas_call(
        _merge_kernel,
        grid=(m // tm, nj),
        in_specs=[branch, branch, branch, gate(0), gate(1), gate(2), weight, weight, weight],
        out_specs=pl.BlockSpec((tm, tn), lambda i, j: (i, j)),
        out_shape=jax.ShapeDtypeStruct((m, d), BF16),
        compiler_params=_cparams(2),
        name="gated_merge",
    )(pool_out, ssm_out, att, proj, proj, proj, w_pool, w_ssm, w_attn)


def _prep_kernel(xp_ref, sm_ref, hist_ref, pw_ref, ps_ref, fb_ref, *refs, rows, pos0, n_valid, with_cum):
    if with_cum:
        pool_ref, logf_ref, ccol_ref, crow_ref, full_ref, carry_ref = refs
    else:
        pool_ref, logf_ref, full_ref = refs
    t = pl.program_id(1)
    hr = POOL_HIST_ROWS

    @pl.when(t == 0)
    def _():
        full_ref[0:hr, :] = hist_ref[...]
        if with_cum:
            carry_ref[...] = jnp.zeros_like(carry_ref)

    x = xp_ref[...]
    full_ref[hr:hr + rows, :] = x
    row = t * rows + lax.broadcasted_iota(jnp.int32, (rows, 1), 0)
    gd = x.shape[1] // len(POOL_WINDOWS)
    for g, w in enumerate(POOL_WINDOWS):
        cols = slice(g * gd, (g + 1) * gd)
        s = full_ref[hr:hr + rows, cols]
        for k in range(1, w):
            s = s + full_ref[hr - k:hr - k + rows, cols]
        cnt = jnp.minimum(w, pos0 + row + 1).astype(F32)
        pooled = s / cnt - x[:, cols]
        mixed = _dot(pooled.astype(BF16), pw_ref[g])
        pool_ref[:, cols] = (mixed * ps_ref[:, cols]).astype(pool_ref.dtype)
    full_ref[0:hr, :] = full_ref[rows:rows + hr, :]

    z = sm_ref[:, F_LANE:F_LANE + logf_ref.shape[1]] + fb_ref[...]
    logf = -_softplus(-z)
    logf_ref[...] = logf
    if with_cum:
        lf = jnp.where(row < n_valid, logf, 0.0)
        r_i = lax.broadcasted_iota(jnp.int32, (rows, rows), 0)
        c_i = lax.broadcasted_iota(jnp.int32, (rows, rows), 1)
        cs = _select_dot(r_i >= c_i, lf) + carry_ref[...]
        ccol_ref[...] = cs
        carry_ref[...] = cs[rows - 1:rows, :]
        nh = cs.shape[1]
        wide = jnp.concatenate([cs, jnp.zeros((rows, LANES - nh), F32)], axis=1)
        crow_ref[...] = wide.T[0:nh, :]


def pool_and_forget(proj, cols, hist, pool_w, pool_scale, forget_bias, *, nb, nt, rows, row0, pos0, n_valid,
                    with_cum, out_rows=None):
    mtot = nb * nt * rows
    blk0 = row0 // rows
    pd = pool_w.shape[0] * pool_w.shape[1]
    nh = forget_bias.shape[0]
    xp_blk = cols["pool"] // pd
    sm_blk = cols["small"] // SMALL_W
    in_specs = [pl.BlockSpec((rows, pd), lambda b, t: (blk0 + b * nt + t, xp_blk)),
                pl.BlockSpec((rows, SMALL_W), lambda b, t: (blk0 + b * nt + t, sm_blk)),
                pl.BlockSpec((None, POOL_HIST_ROWS, pd), lambda b, t: (b, 0, 0)),
                pl.BlockSpec(pool_w.shape, lambda b, t: (0, 0, 0)),
                pl.BlockSpec((1, pd), lambda b, t: (0, 0)),
                pl.BlockSpec((1, nh), lambda b, t: (0, 0))]
    out_specs = [pl.BlockSpec((rows, pd), lambda b, t: (b * nt + t, 0)),
                 pl.BlockSpec((rows, nh), lambda b, t: (b * nt + t, 0))]
    out_shape = [jax.ShapeDtypeStruct((out_rows or mtot, pd), BF16), jax.ShapeDtypeStruct((mtot, nh), F32)]
    scratch = [pltpu.VMEM((POOL_HIST_ROWS + rows, pd), F32)]
    if with_cum:
        out_specs += [pl.BlockSpec((rows, nh), lambda b, t: (b * nt + t, 0)),
                      pl.BlockSpec((None, nh, rows), lambda b, t: (b, 0, t))]
        out_shape += [jax.ShapeDtypeStruct((mtot, nh), F32), jax.ShapeDtypeStruct((nb, nh, nt * rows), F32)]
        scratch += [pltpu.VMEM((1, nh), F32)]
    return pl.pallas_call(
        functools.partial(_prep_kernel, rows=rows, pos0=pos0, n_valid=n_valid, with_cum=with_cum),
        grid=(nb, nt),
        in_specs=in_specs, out_specs=out_specs, out_shape=out_shape, scratch_shapes=scratch,
        compiler_params=_cparams(2),
        name="pool_and_forget",
    )(proj, proj, hist, pool_w, pool_scale.reshape(1, pd), forget_bias.reshape(1, nh))


def _ssd_kernel(xz_ref, z_ref, hist_ref, h0_ref, cw_ref, cb_ref, dtb_ref, alog_ref, dskip_ref, nw_ref,
                ex_ref, ext_ref, o_ref, h_ref, full_ref, y_ref, *, rows, n_valid, d_inner, n_state, head_dim):
    t = SEQ_TILE
    c = pl.program_id(1)
    hr = CONV_HIST_ROWS
    cdim = full_ref.shape[1]
    nheads = d_inner // head_dim
    ngroups = (cdim - d_inner) // (2 * n_state)
    pair_w = 2 * head_dim
    pairs_per_group = d_inner // pair_w // ngroups

    @pl.when(c == 0)
    def _():
        full_ref[0:hr, :] = hist_ref[...]
        h_ref[...] = h0_ref[...]

    full_ref[hr:hr + rows, :] = xz_ref[:, 0:cdim]
    if rows < t:
        full_ref[hr + rows:hr + t, :] = jnp.zeros((t - rows, cdim), F32)
    conv = cb_ref[...]
    for k in range(CONV_TAPS):
        off = hr - (CONV_TAPS - 1) + k
        conv = conv + full_ref[off:off + t, :] * cw_ref[k:k + 1, :]
    act = conv * _sigmoid(conv)
    full_ref[0:hr, :] = full_ref[t:t + hr, :]

    dt_raw = xz_ref[:, cdim + DT_LANE:cdim + DT_LANE + nheads]
    if rows < t:
        dt_raw = jnp.concatenate([dt_raw, jnp.zeros((t - rows, nheads), F32)], axis=0)
    row = c * t + lax.broadcasted_iota(jnp.int32, (t, 1), 0)
    dt = jnp.where(row < n_valid, _softplus(dt_raw + dtb_ref[...]), 0.0)
    a = -jnp.exp(alog_ref[...])
    r_i = lax.broadcasted_iota(jnp.int32, (t, t), 0)
    c_i = lax.broadcasted_iota(jnp.int32, (t, t), 1)
    causal = r_i >= c_i
    acum = _select_dot(causal, dt * a)
    lane_pad = jnp.zeros((t, LANES - nheads), F32)
    acum_row = jnp.concatenate([acum, lane_pad], axis=1).T[0:nheads, :]
    dt_row = jnp.concatenate([dt, lane_pad], axis=1).T[0:nheads, :]
    a_last = acum[t - 1:t, :]
    decay_in = jnp.exp(acum)
    to_end = jnp.exp(a_last - acum) * dt
    wide = _dot_select(jnp.concatenate([decay_in, to_end], axis=0), ex_ref[...])
    decay_in_x = wide[0:t, :]
    to_end_x = wide[t:2 * t, :]
    chunk_decay = jnp.broadcast_to(jnp.exp(acum_row[:, t - 1:t]), (nheads, n_state))
    chunk_decay_x = _select_dot(ext_ref[...], chunk_decay)

    xs = act[:, 0:d_inner]
    xs_b = xs.astype(BF16)
    lane = lax.broadcasted_iota(jnp.int32, (t, pair_w), 1)
    first = lane < head_dim
    for g in range(ngroups):
        bm = act[:, d_inner + g * n_state:d_inner + (g + 1) * n_state].astype(BF16)
        cm = act[:, d_inner + (ngroups + g) * n_state:d_inner + (ngroups + g + 1) * n_state].astype(BF16)
        cb = _dot_nt(cm, bm)
        for j in range(pairs_per_group):
            p = g * pairs_per_group + j
            cols = slice(p * pair_w, (p + 1) * pair_w)
            wts = []
            for h in (2 * p, 2 * p + 1):
                seg = acum[:, h:h + 1] - acum_row[h:h + 1, :]
                decay = jnp.exp(jnp.where(causal, seg, -jnp.inf))
                wts.append((cb * decay) * dt_row[h:h + 1, :])
            lhs = jnp.concatenate(wts, axis=1).astype(BF16)
            xp = xs_b[:, cols]
            zero = jnp.zeros_like(xp)
            rhs = jnp.concatenate([jnp.where(first, xp, zero), jnp.where(first, zero, xp)], axis=0)
            y = _dot(lhs, rhs)
            hp = h_ref[p * pair_w:(p + 1) * pair_w, :]
            y = y + _dot_nt(cm, hp.astype(BF16)) * decay_in_x[:, cols]
            y_ref[:, cols] = y + dskip_ref[:, cols] * xs[:, cols]
            upd = (xs[:, cols] * to_end_x[:, cols]).T.astype(BF16)
            h_ref[p * pair_w:(p + 1) * pair_w, :] = chunk_decay_x[p * pair_w:(p + 1) * pair_w, :] * hp + _dot(upd, bm)

    zz = z_ref[...]
    if rows < t:
        zz = jnp.concatenate([zz, jnp.zeros((t - rows, d_inner), F32)], axis=0)
    gated = y_ref[...] * (zz * _sigmoid(zz))
    ms = jnp.mean(gated * gated, axis=-1, keepdims=True)
    out = (gated * lax.rsqrt(ms + NORM_EPS)) * nw_ref[...]
    o_ref[...] = out[0:rows, :].astype(o_ref.dtype)


def ssd_mixer(proj, cols, hist, h0, conv_w, conv_b, dt_bias, a_log, d_skip, ssm_norm, *, nb, nt, rows, row0,
              n_valid, out_rows=None):
    mtot = nb * nt * rows
    blk0 = row0 // rows
    cdim = conv_w.shape[1]
    nheads = dt_bias.shape[0]
    d_inner, n_state = h0.shape[1], h0.shape[2]
    head_dim = d_inner // nheads
    xz_w = cdim + SMALL_W
    expand = jnp.repeat(jnp.eye(nheads, dtype=F32), head_dim, axis=1)
    const2 = lambda b, t: (0, 0)
    return pl.pallas_call(
        functools.partial(_ssd_kernel, rows=rows, n_valid=n_valid, d_inner=d_inner, n_state=n_state,
                          head_dim=head_dim),
        grid=(nb, nt),
        in_specs=[pl.BlockSpec((rows, xz_w), lambda b, t: (blk0 + b * nt + t, cols["xbc"] // xz_w)),
                  pl.BlockSpec((rows, d_inner), lambda b, t: (blk0 + b * nt + t, cols["z"] // d_inner)),
                  pl.BlockSpec((None, CONV_HIST_ROWS, cdim), lambda b, t: (b, 0, 0)),
                  pl.BlockSpec((None, d_inner, n_state), lambda b, t: (b, 0, 0)),
                  pl.BlockSpec((CONV_TAPS, cdim), const2),
                  pl.BlockSpec((1, cdim), const2),
                  pl.BlockSpec((1, nheads), const2),
                  pl.BlockSpec((1, nheads), const2),
                  pl.BlockSpec((1, d_inner), const2),
                  pl.BlockSpec((1, d_inner), const2),
                  pl.BlockSpec((nheads, d_inner), const2),
                  pl.BlockSpec((d_inner, nheads), const2)],
        out_specs=[pl.BlockSpec((rows, d_inner), lambda b, t: (b * nt + t, 0)),
                   pl.BlockSpec((None, d_inner, n_state), lambda b, t: (b, 0, 0))],
        out_shape=[jax.ShapeDtypeStruct((out_rows or mtot, d_inner), BF16),
                   jax.ShapeDtypeStruct((nb, d_inner, n_state), F32)],
        scratch_shapes=[pltpu.VMEM((CONV_HIST_ROWS + SEQ_TILE, cdim), F32),
                        pltpu.VMEM((SEQ_TILE, d_inner), F32)],
        compiler_params=_cparams(2),
        name="ssd_mixer",
    )(proj, proj, hist, h0, conv_w, conv_b.reshape(1, cdim), dt_bias.reshape(1, nheads),
      a_log.reshape(1, nheads), jnp.repeat(d_skip, head_dim).reshape(1, d_inner),
      ssm_norm.reshape(1, d_inner), expand, expand.T)


def _attn_kernel(q_ref, k_ref, v_ref, ccol_ref, crow_ref, o_ref, kaug_ref, vt_ref, acc_ref, s_ref, *, tq, head_dim, lp):
    pair = pl.program_id(1)
    qi = pl.program_id(2)
    nh = ccol_ref.shape[1]
    pw = 2 * head_dim
    tk = tq
    n_parts = 3
    one_lane = 2 * n_parts

    @pl.when(qi == 0)
    def _():
        hid = lax.broadcasted_iota(jnp.int32, (nh, pw), 0)
        lid = lax.broadcasted_iota(jnp.int32, (nh, pw), 1)
        sel = jnp.where((hid == 2 * pair) & (lid < n_parts), 1.0,
                        jnp.where((hid == 2 * pair + 1) & (lid >= n_parts) & (lid < one_lane), 1.0, 0.0))
        lane = lax.broadcasted_iota(jnp.int32, (SEQ_TILE, pw), 1)
        part = lane - jnp.where(lane >= n_parts, n_parts, 0)

        def body(i, carry):
            r = pl.multiple_of(i * SEQ_TILE, SEQ_TILE)
            c2 = _dot_select(ccol_ref[pl.ds(r, SEQ_TILE), :], sel) * (-LOG2E)
            hi, mid, lo = _split3(c2)
            bias = jnp.where(part == 0, hi, jnp.where(part == 1, mid, lo))
            bias = jnp.where(lane < one_lane, bias, jnp.where(lane < one_lane + n_parts, 1.0, 0.0))
            kaug_ref[pl.ds(r, SEQ_TILE), 0:pw] = k_ref[pl.ds(r, SEQ_TILE), :].astype(BF16)
            kaug_ref[pl.ds(r, SEQ_TILE), pw:2 * pw] = bias.astype(BF16)
            vt_ref[:, pl.ds(r, SEQ_TILE)] = v_ref[pl.ds(r, SEQ_TILE), :].T.astype(BF16)
            return carry

        lax.fori_loop(0, lp // SEQ_TILE, body, 0)

    qs = pl.multiple_of(qi * tq, tq)
    qt = (q_ref[...] * (head_dim ** -0.5 * LOG2E)).T
    row = lax.broadcasted_iota(jnp.int32, (pw, tq), 0)
    q_aug = []
    for e in range(2):
        head_rows = (row < head_dim) if e == 0 else (row >= head_dim)
        c_q = crow_ref[pl.ds(2 * pair + e, 1), pl.ds(qs, tq)] * LOG2E
        hi, mid, lo = _split3(c_q)
        ones_rows = (row >= n_parts * e) & (row < n_parts * (e + 1))
        qb = jnp.where(ones_rows, 1.0,
                       jnp.where(row == one_lane, hi, jnp.where(row == one_lane + 1, mid,
                                                                jnp.where(row == one_lane + 2, lo, 0.0))))
        q_aug.append(jnp.concatenate([jnp.where(head_rows, qt, 0.0), qb], axis=0).astype(BF16))

    acc_ref[...] = jnp.zeros_like(acc_ref)
    q_pos = qs + lax.broadcasted_iota(jnp.int32, (1, tq), 1)

    def scores(kt):
        ks = pl.multiple_of(kt * tk, tk)
        ka = kaug_ref[pl.ds(ks, tk), :]
        return [_dot(ka, q_aug[e]) for e in range(2)]

    def consume(kt, s_pair, carry, masked):
        ks = pl.multiple_of(kt * tk, tk)
        vt = vt_ref[:, pl.ds(ks, tk)]
        new = []
        for e in range(2):
            m_old, l_old = carry[2 * e], carry[2 * e + 1]
            s = s_pair[e]
            if masked:
                k_pos = ks + lax.broadcasted_iota(jnp.int32, (tk, 1), 0)
                s = jnp.where(k_pos <= q_pos, s, NEG_INF)
            m_new = jnp.maximum(m_old, jnp.max(s, axis=0, keepdims=True))
            alpha = jnp.exp2(m_old - m_new)
            p = jnp.exp2(s - m_new)
            l_new = alpha * l_old + jnp.sum(p, axis=0, keepdims=True)
            acc_ref[e] = alpha * acc_ref[e] + _dot(vt, p.astype(BF16))
            new += [m_new, l_new]
        return tuple(new)

    def body(kt, carry):
        s_cur = [s_ref[0], s_ref[1]]
        s_next = scores(kt + 1)
        carry = consume(kt, s_cur, carry, False)
        s_ref[0] = s_next[0]
        s_ref[1] = s_next[1]
        return carry

    s_first = scores(0)
    s_ref[0] = s_first[0]
    s_ref[1] = s_first[1]
    init = (jnp.full((1, tq), NEG_INF, F32), jnp.zeros((1, tq), F32)) * 2
    carry = lax.fori_loop(0, qi, body, init)
    carry = consume(qi, [s_ref[0], s_ref[1]], carry, True)
    out_t = jnp.where(row < head_dim, acc_ref[0] / carry[1], acc_ref[1] / carry[3])
    o_ref[...] = out_t.T.astype(o_ref.dtype)


def prompt_attention(proj, cols, ccol, crow, *, nb, lp, head_dim, out_rows):
    mtot = out_rows
    nh = ccol.shape[1]
    pw = 2 * head_dim
    tq = _pick(lp, (384, 256, 128))
    nq = lp // tq
    return pl.pallas_call(
        functools.partial(_attn_kernel, tq=tq, head_dim=head_dim, lp=lp),
        grid=(nb, nh // 2, nq),
        in_specs=[pl.BlockSpec((tq, pw), lambda b, p, i: (b * nq + i, cols["q"] // pw + p)),
                  pl.BlockSpec((lp, pw), lambda b, p, i: (b, cols["k"] // pw + p)),
                  pl.BlockSpec((lp, pw), lambda b, p, i: (b, cols["v"] // pw + p)),
                  pl.BlockSpec((lp, nh), lambda b, p, i: (b, 0)),
                  pl.BlockSpec((None, nh, lp), lambda b, p, i: (b, 0, 0))],
        out_specs=pl.BlockSpec((tq, pw), lambda b, p, i: (b * nq + i, p)),
        out_shape=jax.ShapeDtypeStruct((mtot, nh * head_dim), BF16),
        scratch_shapes=[pltpu.VMEM((lp, 2 * pw), BF16), pltpu.VMEM((pw, lp), BF16), pltpu.VMEM((2, pw, tq), F32),
                        pltpu.VMEM((2, tq, tq), F32)],
        compiler_params=_cparams(3),
        name="prompt_attention",
    )(proj, proj, proj, ccol, crow)


def _decode_attn_kernel(pt_ref, q_ref, kn_ref, vn_ref, lfn_ref, *refs, pages):
    k_refs = refs[0:pages]
    v_refs = refs[pages:2 * pages]
    lf_refs = refs[2 * pages:3 * pages]
    o_ref, qb_ref, s_ref, p_ref, a_ref, m_ref, l_ref, carry_ref, acc_ref = refs[3 * pages:]
    s_id = pl.program_id(1)
    nh, hd, psz = k_refs[0].shape

    @pl.when(s_id == 0)
    def _():
        qb_ref[...] = q_ref[...] * hd ** -0.5
        lane = lax.broadcasted_iota(jnp.int32, (1, psz), 1)
        for h in range(nh):
            s_ref[h:h + 1, :] = jnp.sum(kn_ref[h] * qb_ref[h], axis=0, keepdims=True)
            acc_ref[h] = jnp.where(lane == 0, vn_ref[h], 0.0)
        m_ref[...] = s_ref[:, 0:1]
        l_ref[...] = jnp.ones_like(l_ref)
        carry_ref[...] = lfn_ref[:, 0:1]

    r_i = lax.broadcasted_iota(jnp.int32, (psz, psz), 0)
    c_i = lax.broadcasted_iota(jnp.int32, (psz, psz), 1)
    later = (r_i > c_i).astype(BF16)
    for i in range(pages):
        for h in range(nh):
            s_ref[h:h + 1, :] = jnp.sum(k_refs[i][h] * qb_ref[h], axis=0, keepdims=True)
        lf = lf_refs[i][...]
        hi, mid, lo = _split3(lf)
        within = (_dot(hi.astype(BF16), later) + _dot(mid.astype(BF16), later)) + _dot(lo.astype(BF16), later)
        carry = carry_ref[...]
        s = s_ref[...] + (within + carry)
        carry_ref[...] = carry + jnp.sum(lf, axis=1, keepdims=True)
        m_old = m_ref[...]
        m_new = jnp.maximum(m_old, jnp.max(s, axis=1, keepdims=True))
        alpha = jnp.exp(m_old - m_new)
        p = jnp.exp(s - m_new)
        l_ref[...] = alpha * l_ref[...] + jnp.sum(p, axis=1, keepdims=True)
        m_ref[...] = m_new
        p_ref[...] = p
        a_ref[...] = jnp.broadcast_to(alpha, (nh, psz))
        for h in range(nh):
            acc_ref[h] = a_ref[h:h + 1, :] * acc_ref[h] + v_refs[i][h] * p_ref[h:h + 1, :]

    @pl.when(s_id == pl.num_programs(1) - 1)
    def _():
        for h in range(nh):
            o_h = jnp.sum(acc_ref[h], axis=1, keepdims=True) / l_ref[h:h + 1, :]
            o_ref[h] = jnp.broadcast_to(o_h, (hd, psz))


def decode_attention(q, k_new, v_new, logf_new, cache_k, cache_v, cache_logf, page_table, layer):
    nb, n_pages = page_table.shape
    _, _, psz, nh, hd = cache_k.shape
    pages = _pick(n_pages, (8, 4, 2, 1))
    kt = jnp.transpose(cache_k, (0, 1, 3, 4, 2))
    vt = jnp.transpose(cache_v, (0, 1, 3, 4, 2))
    lft = jnp.transpose(cache_logf, (0, 1, 3, 2))
    rep = lambda x: jnp.broadcast_to(x[..., None], x.shape + (psz,))

    def page(i, dims):
        def index(b, s, pt):
            return (pt[b, n_pages - 1 - (s * pages + i)], layer) + (0,) * len(dims)
        return pl.BlockSpec((None, None) + dims, index)

    tok = pl.BlockSpec((None, nh, hd, psz), lambda b, s, pt: (b, 0, 0, 0))
    grid_spec = pltpu.PrefetchScalarGridSpec(
        num_scalar_prefetch=1,
        grid=(nb, n_pages // pages),
        in_specs=([tok, tok, tok, pl.BlockSpec((None, nh, psz), lambda b, s, pt: (b, 0, 0))]
                  + [page(i, (nh, hd, psz)) for i in range(pages)]
                  + [page(i, (nh, hd, psz)) for i in range(pages)]
                  + [page(i, (nh, psz)) for i in range(pages)]),
        out_specs=tok,
        scratch_shapes=[pltpu.VMEM((nh, hd, psz), F32), pltpu.VMEM((nh, psz), F32), pltpu.VMEM((nh, psz), F32),
                        pltpu.VMEM((nh, psz), F32), pltpu.VMEM((nh, 1), F32), pltpu.VMEM((nh, 1), F32),
                        pltpu.VMEM((nh, 1), F32), pltpu.VMEM((nh, hd, psz), F32)],
    )
    out = pl.pallas_call(
        functools.partial(_decode_attn_kernel, pages=pages),
        grid_spec=grid_spec,
        out_shape=jax.ShapeDtypeStruct((nb, nh, hd, psz), F32),
        compiler_params=_cparams(2),
        name="decode_attention",
    )(page_table, rep(q), rep(k_new), rep(v_new), rep(logf_new),
      *([kt] * pages), *([vt] * pages), *([lft] * pages))
    return out[..., 0]


def _final_norm_shift_kernel(a_ref, b_ref, g_ref, o_ref, *, shift):
    x = jnp.concatenate([a_ref[shift:, :], b_ref[0:shift, :]], axis=0)
    ms = jnp.mean(x * x, axis=-1, keepdims=True)
    o_ref[...] = (x * lax.rsqrt(ms + NORM_EPS)) * g_ref[...]


def final_norm_prompt(x, g, *, nb, lp, seq, shift):
    d = x.shape[1]
    tr = SEQ_TILE
    nt = seq // tr
    nl = lp // tr
    return pl.pallas_call(
        functools.partial(_final_norm_shift_kernel, shift=shift),
        grid=(nb, nt),
        in_specs=[pl.BlockSpec((tr, d), lambda b, t: (b * nl + t, 0)),
                  pl.BlockSpec((tr, d), lambda b, t: (b * nl + t + 1, 0)),
                  pl.BlockSpec((1, d), lambda b, t: (0, 0))],
        out_specs=pl.BlockSpec((None, tr, d), lambda b, t: (b, t, 0)),
        out_shape=jax.ShapeDtypeStruct((nb, seq, d), F32),
        compiler_params=_cparams(2),
        name="final_norm_prompt",
    )(x, x, g.reshape(1, d))


def _final_norm_kernel(x_ref, g_ref, o_ref):
    x = x_ref[...]
    ms = jnp.mean(x * x, axis=-1, keepdims=True)
    o_ref[...] = (x * lax.rsqrt(ms + NORM_EPS)) * g_ref[...]


def final_norm(x, g):
    m, d = x.shape
    return pl.pallas_call(
        _final_norm_kernel,
        grid=(1,),
        in_specs=[pl.BlockSpec((m, d), lambda i: (0, 0)), pl.BlockSpec((1, d), lambda i: (0, 0))],
        out_specs=pl.BlockSpec((m, d), lambda i: (0, 0)),
        out_shape=jax.ShapeDtypeStruct((m, d), F32),
        compiler_params=_cparams(1),
        name="final_norm",
    )(x, g.reshape(1, d))


def _pack_kernel(offs_ref, w_ref, ws_ref, o_ref, *, small_tile):
    w = jnp.where(pl.program_id(1) == small_tile, ws_ref[...], w_ref[0])
    o_ref[...] = w.T.astype(o_ref.dtype)


def pack_w_in(w_in_t, dims):
    depth, n_out, d = w_in_t.shape
    pd, di, cd, nh_s, ad, nh_a, dm = dims
    tile = SMALL_W
    src = {}
    o = 0
    for name, width in (("pool", pd), ("z", di), ("xbc", cd), ("dt", nh_s), ("q", ad), ("k", ad), ("v", ad),
                        ("f", nh_a), ("gate", n_out - (pd + di + cd + nh_s + 3 * ad + nh_a))):
        src[name] = (o, width)
        o += width
    assert DT_LANE == 0 and nh_s <= F_LANE
    rows = lambda name: w_in_t[:, src[name][0]:src[name][0] + src[name][1]]
    zeros = lambda w: jnp.zeros((depth, w, d), w_in_t.dtype)
    small = jnp.concatenate([rows("dt"), zeros(F_LANE - nh_s), rows("f"), zeros(tile - F_LANE - nh_a)], axis=1)
    cols, offs = {}, []
    for name in ("xbc", "small", "pool", "z", "q", "k", "v", "gate"):
        cols[name] = len(offs) * tile
        if name == "small":
            small_tile = len(offs)
            offs.append(0)
        else:
            start, width = src[name]
            assert width % tile == 0 and start % SUBLANES == 0 and tile % SUBLANES == 0
            offs += [(start + t * tile) // SUBLANES for t in range(width // tile)]
    grid_spec = pltpu.PrefetchScalarGridSpec(
        num_scalar_prefetch=1,
        grid=(depth, len(offs)),
        in_specs=[pl.BlockSpec((pl.Element(1), pl.Element(tile), pl.Element(d)), lambda l, j, o: (l, o[j] * SUBLANES, 0)),
                  pl.BlockSpec((None, tile, d), lambda l, j, o: (l, 0, 0))],
        out_specs=pl.BlockSpec((None, d, tile), lambda l, j, o: (l, 0, j)),
    )
    packed = pl.pallas_call(
        functools.partial(_pack_kernel, small_tile=small_tile),
        grid_spec=grid_spec,
        out_shape=jax.ShapeDtypeStruct((depth, d, len(offs) * tile), BF16),
        compiler_params=_cparams(2),
        name="pack_w_in",
    )(jnp.asarray(offs, jnp.int32), w_in_t, small)
    return packed, cols


def kernel(x_prompt, x_sample, cache_k, cache_v, cache_logf, page_table, state_ssm, state_conv, state_pool,
           meta_tokens, norm_ffn1, ffn1_w_gu, ffn1_w_down, norm_mix, w_in, pool_w, pool_scale, conv_w, conv_b,
           dt_bias, a_log, d_skip, ssm_norm, forget_bias, w_br_pool, w_br_ssm, w_br_attn, w_out, norm_ffn2,
           ffn2_w_gu, ffn2_w_down, final_norm_w):
    nb, seq, dm = x_prompt.shape
    ns = x_sample.shape[0]
    depth = w_in.shape[0]
    n_meta = meta_tokens.shape[0]
    nh_a, hd_a = cache_k.shape[3], cache_k.shape[4]
    ad = nh_a * hd_a
    psz = cache_k.shape[2]
    past_len = page_table.shape[1] * psz
    nh_s, hd_s, n_state = state_ssm.shape[2], state_ssm.shape[3], state_ssm.shape[4]
    di = nh_s * hd_s
    cd = state_conv.shape[3]
    conv_hist = state_conv.shape[2]
    pd = state_pool.shape[3]
    pool_hist = state_pool.shape[2]
    l = seq + n_meta
    lp = -(-l // SEQ_TILE) * SEQ_TILE
    assert x_sample.shape[1] == 1 and seq % SEQ_TILE == 0 and n_meta % 8 == 0 and n_meta < SEQ_TILE
    assert 2 * hd_a == LANES and 2 * hd_s == LANES and n_state == LANES
    assert pool_hist < POOL_HIST_ROWS and conv_hist < CONV_HIST_ROWS

    mp = nb * lp
    m_all = mp + ns * DEC_ROWS
    x = jnp.concatenate(
        [jnp.concatenate([jnp.broadcast_to(meta_tokens[None], (nb, n_meta, dm)), x_prompt,
                          jnp.zeros((nb, lp - l, dm), F32)], axis=1).reshape(mp, dm),
         jnp.pad(x_sample, ((0, 0), (0, DEC_ROWS - 1), (0, 0))).reshape(ns * DEC_ROWS, dm)], axis=0)

    zero_pool_hist = jnp.zeros((nb, POOL_HIST_ROWS, pd), F32)
    zero_conv_hist = jnp.zeros((nb, CONV_HIST_ROWS, cd), F32)
    zero_state = jnp.zeros((nb, di, n_state), F32)
    pool_w_b = pool_w.astype(BF16)
    w_in_p, cols = pack_w_in(jnp.swapaxes(w_in, 1, 2), (pd, di, cd, nh_s, ad, nh_a, dm))

    outs = {k: [] for k in ("kp", "vp", "lfp", "hp", "cp", "pp", "ks", "vs", "lfs", "hs", "cs", "ps")}
    for i in range(depth):
        pw = pool_w_b[i]

        x = matmul_residual(norm_swiglu(x, norm_ffn1[i], ffn1_w_gu, i), ffn1_w_down, i, x, 0.5)
        proj = norm_matmul(x, norm_mix[i], w_in_p, i)

        pool_p, lf_p, ccol, crow = pool_and_forget(
            proj, cols, zero_pool_hist, pw, pool_scale[i], forget_bias[i],
            nb=nb, nt=lp // SEQ_TILE, rows=SEQ_TILE, row0=0, pos0=0, n_valid=l, with_cum=True, out_rows=m_all)
        ssm_p, h_p = ssd_mixer(proj, cols, zero_conv_hist, zero_state, conv_w[i], conv_b[i], dt_bias[i], a_log[i],
                               d_skip[i], ssm_norm[i], nb=nb, nt=lp // SEQ_TILE, rows=SEQ_TILE, row0=0, n_valid=l,
                               out_rows=m_all)
        att_p = prompt_attention(proj, cols, ccol, crow, nb=nb, lp=lp, head_dim=hd_a, out_rows=m_all)

        hist_p = jnp.pad(state_pool[:, i], ((0, 0), (POOL_HIST_ROWS - pool_hist, 0), (0, 0)))
        hist_c = jnp.pad(state_conv[:, i], ((0, 0), (CONV_HIST_ROWS - conv_hist, 0), (0, 0)))
        pool_s, lf_s = pool_and_forget(
            proj, cols, hist_p, pw, pool_scale[i], forget_bias[i],
            nb=ns, nt=1, rows=DEC_ROWS, row0=mp, pos0=past_len, n_valid=1, with_cum=False)
        ssm_s, h_s = ssd_mixer(proj, cols, hist_c, state_ssm[:, i].reshape(ns, di, n_state), conv_w[i], conv_b[i],
                               dt_bias[i], a_log[i], d_skip[i], ssm_norm[i],
                               nb=ns, nt=1, rows=DEC_ROWS, row0=mp, n_valid=1)
        ps0 = proj[mp:].reshape(ns, DEC_ROWS, -1)[:, 0:1]
        heads = lambda name: ps0[:, 0, cols[name]:cols[name] + ad].reshape(ns, nh_a, hd_a)
        att_s = decode_attention(heads("q"), heads("k"), heads("v"), lf_s.reshape(ns, DEC_ROWS, nh_a)[:, 0],
                                 cache_k, cache_v, cache_logf, page_table, i)
        att_s = jnp.pad(att_s.reshape(ns, 1, ad), ((0, 0), (0, DEC_ROWS - 1), (0, 0)))
        att_s = att_s.reshape(ns * DEC_ROWS, ad).astype(BF16)

        place = lambda full, part: lax.dynamic_update_slice(full, part, (mp, 0))
        merged = gated_merge(place(pool_p, pool_s), place(ssm_p, ssm_s), place(att_p, att_s), proj, cols["gate"],
                             w_br_pool, w_br_ssm, w_br_attn, i)
        x = matmul_residual(merged, w_out, i, x, 1.0)
        x = matmul_residual(norm_swiglu(x, norm_ffn2[i], ffn2_w_gu, i), ffn2_w_down, i, x, 0.5)

        def prompt_rows(first, last, name, width):
            return jnp.stack([proj[b * lp + first:b * lp + last, cols[name]:cols[name] + width] for b in range(nb)])

        outs["kp"].append(prompt_rows(0, l, "k", ad).reshape(nb, l, nh_a, hd_a))
        outs["vp"].append(prompt_rows(0, l, "v", ad).reshape(nb, l, nh_a, hd_a))
        outs["lfp"].append(lf_p.reshape(nb, lp, nh_a)[:, :l])
        outs["hp"].append(h_p.reshape(nb, nh_s, hd_s, n_state))
        outs["cp"].append(prompt_rows(l - conv_hist, l, "xbc", cd))
        outs["pp"].append(prompt_rows(l - pool_hist, l, "pool", pd))
        outs["ks"].append(ps0[:, :, cols["k"]:cols["k"] + ad].reshape(ns, 1, nh_a, hd_a))
        outs["vs"].append(ps0[:, :, cols["v"]:cols["v"] + ad].reshape(ns, 1, nh_a, hd_a))
        outs["lfs"].append(lf_s.reshape(ns, DEC_ROWS, nh_a)[:, 0:1])
        outs["hs"].append(h_s.reshape(ns, nh_s, hd_s, n_state))
        outs["cs"].append(jnp.concatenate([state_conv[:, i, 1:], ps0[:, :, cols["xbc"]:cols["xbc"] + cd]], axis=1))
        outs["ps"].append(jnp.concatenate([state_pool[:, i, 1:], ps0[:, :, cols["pool"]:cols["pool"] + pd]], axis=1))

    y_prompt = final_norm_prompt(x, final_norm_w, nb=nb, lp=lp, seq=seq, shift=n_meta)
    y_sample = final_norm(x[mp:], final_norm_w).reshape(ns, DEC_ROWS, dm)[:, 0:1]
    st = lambda k: jnp.stack(outs[k], axis=1)
    return (y_prompt, y_sample, st("kp"), st("vp"), st("lfp"), st("ks"), st("vs"), st("lfs"),
            st("hp"), st("hs"), st("cp"), st("cs"), st("pp"), st("ps"))
```

```python
import functools
import math

import jax
import jax.numpy as jnp
from jax import lax
from jax.experimental import pallas as pl
from jax.experimental.pallas import tpu as pltpu

F32 = jnp.float32
BF16 = jnp.bfloat16

NORM_EPS = 1e-6
NEG_INF = -1e30
LOG2E = 1.4426950408889634
POOL_WINDOWS = (2, 4, 8, 16)
POOL_HIST_ROWS = 16
CONV_TAPS = 4
CONV_HIST_ROWS = 8
SEQ_TILE = 128
DEC_ROWS = 16
LANES = 128
SUBLANES = 8
SMALL_W = 512
DT_LANE = 0
F_LANE = 128
VMEM_LIMIT_BYTES = 56 * 1024 * 1024
ROW_TILES = (1072, 1056, 768, 704, 528, 512, 384, 256, 128)


def _cparams(n_axes):
    return pltpu.CompilerParams(dimension_semantics=("arbitrary",) * n_axes,
                                vmem_limit_bytes=VMEM_LIMIT_BYTES)


def _pick(n, cands):
    for c in cands:
        if n % c == 0:
            return c
    return n


def _sigmoid(x):
    return 1.0 / (1.0 + jnp.exp(-x))


def _softplus(x):
    return jnp.maximum(x, 0.0) + jnp.log(1.0 + jnp.exp(-jnp.abs(x)))


def _dot(a, b):
    return jnp.dot(a, b, preferred_element_type=F32)


def _dot_nt(a, b):
    return lax.dot_general(a, b, (((1,), (1,)), ((), ())), preferred_element_type=F32)


def _split3(x):
    hi = x.astype(BF16).astype(F32)
    r = x - hi
    mid = r.astype(BF16).astype(F32)
    lo = (r - mid).astype(BF16).astype(F32)
    return hi, mid, lo


def _select_dot(sel, x):
    s = sel.astype(BF16)
    hi, mid, lo = _split3(x)
    return (_dot(s, hi.astype(BF16)) + _dot(s, mid.astype(BF16))) + _dot(s, lo.astype(BF16))


def _dot_select(x, sel):
    s = sel.astype(BF16)
    hi, mid, lo = _split3(x)
    return (_dot(hi.astype(BF16), s) + _dot(mid.astype(BF16), s)) + _dot(lo.astype(BF16), s)


def _rms_to_scratch(x_ref, g_ref, xn_ref):
    x = x_ref[...]
    ms = jnp.mean(x * x, axis=-1, keepdims=True)
    xn_ref[...] = ((x * lax.rsqrt(ms + NORM_EPS)) * g_ref[...]).astype(BF16)


def _norm_swiglu_kernel(x_ref, g_ref, wg_ref, wu_ref, o_ref, xn_ref):
    @pl.when(pl.program_id(1) == 0)
    def _():
        _rms_to_scratch(x_ref, g_ref, xn_ref)

    xn = xn_ref[...]
    gate = _dot(xn, wg_ref[...].astype(BF16))
    up = _dot(xn, wu_ref[...].astype(BF16))
    o_ref[...] = ((gate * _sigmoid(gate)) * up).astype(o_ref.dtype)


def norm_swiglu(x, g, w_gu, layer):
    m, d = x.shape
    f = w_gu.shape[2] // 2
    tm = _pick(m, ROW_TILES)
    tn = _pick(f, (512, 256, 128))
    nj = f // tn
    return pl.pallas_call(
        _norm_swiglu_kernel,
        grid=(m // tm, nj),
        in_specs=[pl.BlockSpec((tm, d), lambda i, j: (i, 0)),
                  pl.BlockSpec((1, d), lambda i, j: (0, 0)),
                  pl.BlockSpec((None, d, tn), lambda i, j: (layer, 0, j)),
                  pl.BlockSpec((None, d, tn), lambda i, j: (layer, 0, j + nj))],
        out_specs=pl.BlockSpec((tm, tn), lambda i, j: (i, j)),
        out_shape=jax.ShapeDtypeStruct((m, f), BF16),
        scratch_shapes=[pltpu.VMEM((tm, d), BF16)],
        compiler_params=_cparams(2),
        name="norm_swiglu",
    )(x, g.reshape(1, d), w_gu, w_gu)


def _matmul_residual_kernel(h_ref, w_ref, x_ref, o_ref, *, scale):
    o_ref[...] = x_ref[...] + scale * _dot(h_ref[...], w_ref[...].astype(BF16))


def matmul_residual(h, w, layer, x, scale):
    m, k = h.shape
    n = w.shape[2]
    tm = _pick(m, ROW_TILES)
    tn = _pick(n, (512, 256, 128) if k * 512 * 4 <= 8 * 1024 * 1024 else (256, 128))
    return pl.pallas_call(
        functools.partial(_matmul_residual_kernel, scale=scale),
        grid=(m // tm, n // tn),
        in_specs=[pl.BlockSpec((tm, k), lambda i, j: (i, 0)),
                  pl.BlockSpec((None, k, tn), lambda i, j: (layer, 0, j)),
                  pl.BlockSpec((tm, tn), lambda i, j: (i, j))],
        out_specs=pl.BlockSpec((tm, tn), lambda i, j: (i, j)),
        out_shape=jax.ShapeDtypeStruct((m, n), F32),
        compiler_params=_cparams(2),
        name="matmul_residual",
    )(h, w, x)


def _norm_matmul_kernel(x_ref, g_ref, w_ref, o_ref, xn_ref):
    @pl.when(pl.program_id(1) == 0)
    def _():
        _rms_to_scratch(x_ref, g_ref, xn_ref)

    o_ref[...] = _dot(xn_ref[...], w_ref[...])


def norm_matmul(x, g, w, layer):
    m, d = x.shape
    n = w.shape[2]
    tm = _pick(m, ROW_TILES)
    tn = _pick(n, (1024, 512, 256, 128))
    return pl.pallas_call(
        _norm_matmul_kernel,
        grid=(m // tm, n // tn),
        in_specs=[pl.BlockSpec((tm, d), lambda i, j: (i, 0)),
                  pl.BlockSpec((1, d), lambda i, j: (0, 0)),
                  pl.BlockSpec((None, d, tn), lambda i, j: (layer, 0, j))],
        out_specs=pl.BlockSpec((tm, tn), lambda i, j: (i, j)),
        out_shape=jax.ShapeDtypeStruct((m, n), F32),
        scratch_shapes=[pltpu.VMEM((tm, d), BF16)],
        compiler_params=_cparams(2),
        name="norm_matmul",
    )(x, g.reshape(1, d), w)


def _merge_kernel(p_ref, s_ref, a_ref, g0_ref, g1_ref, g2_ref, wp_ref, ws_ref, wa_ref, o_ref):
    merged = (_sigmoid(g0_ref[...]) * _dot(p_ref[...], wp_ref[...].astype(BF16))
              + _sigmoid(g1_ref[...]) * _dot(s_ref[...], ws_ref[...].astype(BF16))
              + _sigmoid(g2_ref[...]) * _dot(a_ref[...], wa_ref[...].astype(BF16)))
    o_ref[...] = merged.astype(o_ref.dtype)


def gated_merge(pool_out, ssm_out, att, proj, gate_col, w_pool, w_ssm, w_attn, layer):
    m, kb = pool_out.shape
    d = w_pool.shape[2]
    tm = _pick(m, ROW_TILES)
    tn = _pick(d, (512, 256, 128))
    g_blk = gate_col // tn
    nj = d // tn
    branch = pl.BlockSpec((tm, kb), lambda i, j: (i, 0))
    weight = pl.BlockSpec((None, kb, tn), lambda i, j: (layer, 0, j))

    def gate(r):
        return pl.BlockSpec((tm, tn), lambda i, j: (i, g_blk + r * nj + j))

    return pl.pallas_call(
        _merge_kernel,
        grid=(m // tm, nj),
        in_specs=[branch, branch, branch, gate(0), gate(1), gate(2), weight, weight, weight],
        out_specs=pl.BlockSpec((tm, tn), lambda i, j: (i, j)),
        out_shape=jax.ShapeDtypeStruct((m, d), BF16),
        compiler_params=_cparams(2),
        name="gated_merge",
    )(pool_out, ssm_out, att, proj, proj, proj, w_pool, w_ssm, w_attn)


def _prep_kernel(xp_ref, sm_ref, hist_ref, pw_ref, ps_ref, fb_ref, *refs, rows, pos0, n_valid, with_cum):
    if with_cum:
        pool_ref, logf_ref, ccol_ref, crow_ref, full_ref, carry_ref = refs
    else:
        pool_ref, logf_ref, full_ref = refs
    t = pl.program_id(1)
    hr = POOL_HIST_ROWS

    @pl.when(t == 0)
    def _():
        full_ref[0:hr, :] = hist_ref[...]
        if with_cum:
            carry_ref[...] = jnp.zeros_like(carry_ref)

    x = xp_ref[...]
    full_ref[hr:hr + rows, :] = x
    row = t * rows + lax.broadcasted_iota(jnp.int32, (rows, 1), 0)
    gd = x.shape[1] // len(POOL_WINDOWS)
    for g, w in enumerate(POOL_WINDOWS):
        cols = slice(g * gd, (g + 1) * gd)
        s = full_ref[hr:hr + rows, cols]
        for k in range(1, w):
            s = s + full_ref[hr - k:hr - k + rows, cols]
        cnt = jnp.minimum(w, pos0 + row + 1).astype(F32)
        pooled = s / cnt - x[:, cols]
        mixed = _dot(pooled.astype(BF16), pw_ref[g])
        pool_ref[:, cols] = (mixed * ps_ref[:, cols]).astype(pool_ref.dtype)
    full_ref[0:hr, :] = full_ref[rows:rows + hr, :]

    z = sm_ref[:, F_LANE:F_LANE + logf_ref.shape[1]] + fb_ref[...]
    logf = -_softplus(-z)
    logf_ref[...] = logf
    if with_cum:
        lf = jnp.where(row < n_valid, logf, 0.0)
        r_i = lax.broadcasted_iota(jnp.int32, (rows, rows), 0)
        c_i = lax.broadcasted_iota(jnp.int32, (rows, rows), 1)
        cs = _select_dot(r_i >= c_i, lf) + carry_ref[...]
        ccol_ref[...] = cs
        carry_ref[...] = cs[rows - 1:rows, :]
        nh = cs.shape[1]
        wide = jnp.concatenate([cs, jnp.zeros((rows, LANES - nh), F32)], axis=1)
        crow_ref[...] = wide.T[0:nh, :]


def pool_and_forget(proj, cols, hist, pool_w, pool_scale, forget_bias, *, nb, nt, rows, row0, pos0, n_valid,
                    with_cum, out_rows=None):
    mtot = nb * nt * rows
    blk0 = row0 // rows
    pd = pool_w.shape[0] * pool_w.shape[1]
    nh = forget_bias.shape[0]
    xp_blk = cols["pool"] // pd
    sm_blk = cols["small"] // SMALL_W
    in_specs = [pl.BlockSpec((rows, pd), lambda b, t: (blk0 + b * nt + t, xp_blk)),
                pl.BlockSpec((rows, SMALL_W), lambda b, t: (blk0 + b * nt + t, sm_blk)),
                pl.BlockSpec((None, POOL_HIST_ROWS, pd), lambda b, t: (b, 0, 0)),
                pl.BlockSpec(pool_w.shape, lambda b, t: (0, 0, 0)),
                pl.BlockSpec((1, pd), lambda b, t: (0, 0)),
                pl.BlockSpec((1, nh), lambda b, t: (0, 0))]
    out_specs = [pl.BlockSpec((rows, pd), lambda b, t: (b * nt + t, 0)),
                 pl.BlockSpec((rows, nh), lambda b, t: (b * nt + t, 0))]
    out_shape = [jax.ShapeDtypeStruct((out_rows or mtot, pd), BF16), jax.ShapeDtypeStruct((mtot, nh), F32)]
    scratch = [pltpu.VMEM((POOL_HIST_ROWS + rows, pd), F32)]
    if with_cum:
        out_specs += [pl.BlockSpec((rows, nh), lambda b, t: (b * nt + t, 0)),
                      pl.BlockSpec((None, nh, rows), lambda b, t: (b, 0, t))]
        out_shape += [jax.ShapeDtypeStruct((mtot, nh), F32), jax.ShapeDtypeStruct((nb, nh, nt * rows), F32)]
        scratch += [pltpu.VMEM((1, nh), F32)]
    return pl.pallas_call(
        functools.partial(_prep_kernel, rows=rows, pos0=pos0, n_valid=n_valid, with_cum=with_cum),
        grid=(nb, nt),
        in_specs=in_specs, out_specs=out_specs, out_shape=out_shape, scratch_shapes=scratch,
        compiler_params=_cparams(2),
        name="pool_and_forget",
    )(proj, proj, hist, pool_w, pool_scale.reshape(1, pd), forget_bias.reshape(1, nh))


def _ssd_kernel(xz_ref, z_ref, hist_ref, h0_ref, cw_ref, cb_ref, dtb_ref, alog_ref, dskip_ref, nw_ref,
                ex_ref, ext_ref, o_ref, h_ref, full_ref, y_ref, *, rows, n_valid, d_inner, n_state, head_dim):
    t = SEQ_TILE
    c = pl.program_id(1)
    hr = CONV_HIST_ROWS
    cdim = full_ref.shape[1]
    nheads = d_inner // head_dim
    ngroups = (cdim - d_inner) // (2 * n_state)
    pair_w = 2 * head_dim
    pairs_per_group = d_inner // pair_w // ngroups

    @pl.when(c == 0)
    def _():
        full_ref[0:hr, :] = hist_ref[...]
        h_ref[...] = h0_ref[...]

    full_ref[hr:hr + rows, :] = xz_ref[:, 0:cdim]
    if rows < t:
        full_ref[hr + rows:hr + t, :] = jnp.zeros((t - rows, cdim), F32)
    conv = cb_ref[...]
    for k in range(CONV_TAPS):
        off = hr - (CONV_TAPS - 1) + k
        conv = conv + full_ref[off:off + t, :] * cw_ref[k:k + 1, :]
    act = conv * _sigmoid(conv)
    full_ref[0:hr, :] = full_ref[t:t + hr, :]

    dt_raw = xz_ref[:, cdim + DT_LANE:cdim + DT_LANE + nheads]
    if rows < t:
        dt_raw = jnp.concatenate([dt_raw, jnp.zeros((t - rows, nheads), F32)], axis=0)
    row = c * t + lax.broadcasted_iota(jnp.int32, (t, 1), 0)
    dt = jnp.where(row < n_valid, _softplus(dt_raw + dtb_ref[...]), 0.0)
    a = -jnp.exp(alog_ref[...])
    r_i = lax.broadcasted_iota(jnp.int32, (t, t), 0)
    c_i = lax.broadcasted_iota(jnp.int32, (t, t), 1)
    causal = r_i >= c_i
    acum = _select_dot(causal, dt * a)
    lane_pad = jnp.zeros((t, LANES - nheads), F32)
    acum_row = jnp.concatenate([acum, lane_pad], axis=1).T[0:nheads, :]
    dt_row = jnp.concatenate([dt, lane_pad], axis=1).T[0:nheads, :]
    a_last = acum[t - 1:t, :]
    decay_in = jnp.exp(acum)
    to_end = jnp.exp(a_last - acum) * dt
    wide = _dot_select(jnp.concatenate([decay_in, to_end], axis=0), ex_ref[...])
    decay_in_x = wide[0:t, :]
    to_end_x = wide[t:2 * t, :]
    chunk_decay = jnp.broadcast_to(jnp.exp(acum_row[:, t - 1:t]), (nheads, n_state))
    chunk_decay_x = _select_dot(ext_ref[...], chunk_decay)

    xs = act[:, 0:d_inner]
    xs_b = xs.astype(BF16)
    lane = lax.broadcasted_iota(jnp.int32, (t, pair_w), 1)
    first = lane < head_dim
    for g in range(ngroups):
        bm = act[:, d_inner + g * n_state:d_inner + (g + 1) * n_state].astype(BF16)
        cm = act[:, d_inner + (ngroups + g) * n_state:d_inner + (ngroups + g + 1) * n_state].astype(BF16)
        cb = _dot_nt(cm, bm)
        for j in range(pairs_per_group):
            p = g * pairs_per_group + j
            cols = slice(p * pair_w, (p + 1) * pair_w)
            wts = []
            for h in (2 * p, 2 * p + 1):
                seg = acum[:, h:h + 1] - acum_row[h:h + 1, :]
                decay = jnp.exp(jnp.where(causal, seg, -jnp.inf))
                wts.append((cb * decay) * dt_row[h:h + 1, :])
            lhs = jnp.concatenate(wts, axis=1).astype(BF16)
            xp = xs_b[:, cols]
            zero = jnp.zeros_like(xp)
            rhs = jnp.concatenate([jnp.where(first, xp, zero), jnp.where(first, zero, xp)], axis=0)
            y = _dot(lhs, rhs)
            hp = h_ref[p * pair_w:(p + 1) * pair_w, :]
            y = y + _dot_nt(cm, hp.astype(BF16)) * decay_in_x[:, cols]
            y_ref[:, cols] = y + dskip_ref[:, cols] * xs[:, cols]
            upd = (xs[:, cols] * to_end_x[:, cols]).T.astype(BF16)
            h_ref[p * pair_w:(p + 1) * pair_w, :] = chunk_decay_x[p * pair_w:(p + 1) * pair_w, :] * hp + _dot(upd, bm)

    zz = z_ref[...]
    if rows < t:
        zz = jnp.concatenate([zz, jnp.zeros((t - rows, d_inner), F32)], axis=0)
    gated = y_ref[...] * (zz * _sigmoid(zz))
    ms = jnp.mean(gated * gated, axis=-1, keepdims=True)
    out = (gated * lax.rsqrt(ms + NORM_EPS)) * nw_ref[...]
    o_ref[...] = out[0:rows, :].astype(o_ref.dtype)


def ssd_mixer(proj, cols, hist, h0, conv_w, conv_b, dt_bias, a_log, d_skip, ssm_norm, *, nb, nt, rows, row0,
              n_valid, out_rows=None):
    mtot = nb * nt * rows
    blk0 = row0 // rows
    cdim = conv_w.shape[1]
    nheads = dt_bias.shape[0]
    d_inner, n_state = h0.shape[1], h0.shape[2]
    head_dim = d_inner // nheads
    xz_w = cdim + SMALL_W
    expand = jnp.repeat(jnp.eye(nheads, dtype=F32), head_dim, axis=1)
    const2 = lambda b, t: (0, 0)
    return pl.pallas_call(
        functools.partial(_ssd_kernel, rows=rows, n_valid=n_valid, d_inner=d_inner, n_state=n_state,
                          head_dim=head_dim),
        grid=(nb, nt),
        in_specs=[pl.BlockSpec((rows, xz_w), lambda b, t: (blk0 + b * nt + t, cols["xbc"] // xz_w)),
                  pl.BlockSpec((rows, d_inner), lambda b, t: (blk0 + b * nt + t, cols["z"] // d_inner)),
                  pl.BlockSpec((None, CONV_HIST_ROWS, cdim), lambda b, t: (b, 0, 0)),
                  pl.BlockSpec((None, d_inner, n_state), lambda b, t: (b, 0, 0)),
                  pl.BlockSpec((CONV_TAPS, cdim), const2),
                  pl.BlockSpec((1, cdim), const2),
                  pl.BlockSpec((1, nheads), const2),
                  pl.BlockSpec((1, nheads), const2),
                  pl.BlockSpec((1, d_inner), const2),
                  pl.BlockSpec((1, d_inner), const2),
                  pl.BlockSpec((nheads, d_inner), const2),
                  pl.BlockSpec((d_inner, nheads), const2)],
        out_specs=[pl.BlockSpec((rows, d_inner), lambda b, t: (b * nt + t, 0)),
                   pl.BlockSpec((None, d_inner, n_state), lambda b, t: (b, 0, 0))],
        out_shape=[jax.ShapeDtypeStruct((out_rows or mtot, d_inner), BF16),
                   jax.ShapeDtypeStruct((nb, d_inner, n_state), F32)],
        scratch_shapes=[pltpu.VMEM((CONV_HIST_ROWS + SEQ_TILE, cdim), F32),
                        pltpu.VMEM((SEQ_TILE, d_inner), F32)],
        compiler_params=_cparams(2),
        name="ssd_mixer",
    )(proj, proj, hist, h0, conv_w, conv_b.reshape(1, cdim), dt_bias.reshape(1, nheads),
      a_log.reshape(1, nheads), jnp.repeat(d_skip, head_dim).reshape(1, d_inner),
      ssm_norm.reshape(1, d_inner), expand, expand.T)


def _attn_kernel(q_ref, k_ref, v_ref, ccol_ref, crow_ref, o_ref, kaug_ref, vt_ref, acc_ref, s_ref, p_ref, *, tq, head_dim,
                 lp):
    pair = pl.program_id(1)
    qi = pl.program_id(2)
    nh = ccol_ref.shape[1]
    pw = 2 * head_dim
    tk = tq
    n_parts = 3
    one_lane = 2 * n_parts

    @pl.when(qi == 0)
    def _():
        hid = lax.broadcasted_iota(jnp.int32, (nh, pw), 0)
        lid = lax.broadcasted_iota(jnp.int32, (nh, pw), 1)
        sel = jnp.where((hid == 2 * pair) & (lid < n_parts), 1.0,
                        jnp.where((hid == 2 * pair + 1) & (lid >= n_parts) & (lid < one_lane), 1.0, 0.0))
        lane = lax.broadcasted_iota(jnp.int32, (SEQ_TILE, pw), 1)
        part = lane - jnp.where(lane >= n_parts, n_parts, 0)

        def body(i, carry):
            r = pl.multiple_of(i * SEQ_TILE, SEQ_TILE)
            c2 = _dot_select(ccol_ref[pl.ds(r, SEQ_TILE), :], sel) * (-LOG2E)
            hi, mid, lo = _split3(c2)
            bias = jnp.where(part == 0, hi, jnp.where(part == 1, mid, lo))
            bias = jnp.where(lane < one_lane, bias, jnp.where(lane < one_lane + n_parts, 1.0, 0.0))
            kaug_ref[pl.ds(r, SEQ_TILE), 0:pw] = k_ref[pl.ds(r, SEQ_TILE), :].astype(BF16)
            kaug_ref[pl.ds(r, SEQ_TILE), pw:2 * pw] = bias.astype(BF16)
            vt = v_ref[pl.ds(r, SEQ_TILE), :].T
            d_row = lax.broadcasted_iota(jnp.int32, vt.shape, 0)
            vt_ref[0, :, pl.ds(r, SEQ_TILE)] = jnp.where(d_row < head_dim, vt, 0.0).astype(BF16)
            vt_ref[1, :, pl.ds(r, SEQ_TILE)] = jnp.where(d_row < head_dim, 0.0, vt).astype(BF16)
            return carry

        lax.fori_loop(0, lp // SEQ_TILE, body, 0)

    qs = pl.multiple_of(qi * tq, tq)
    qt = (q_ref[...] * (head_dim ** -0.5 * LOG2E)).T
    row = lax.broadcasted_iota(jnp.int32, (pw, tq), 0)
    q_aug = []
    for e in range(2):
        head_rows = (row < head_dim) if e == 0 else (row >= head_dim)
        c_q = crow_ref[pl.ds(2 * pair + e, 1), pl.ds(qs, tq)] * LOG2E
        hi, mid, lo = _split3(c_q)
        ones_rows = (row >= n_parts * e) & (row < n_parts * (e + 1))
        qb = jnp.where(ones_rows, 1.0,
                       jnp.where(row == one_lane, hi, jnp.where(row == one_lane + 1, mid,
                                                                jnp.where(row == one_lane + 2, lo, 0.0))))
        q_aug.append(jnp.concatenate([jnp.where(head_rows, qt, 0.0), qb], axis=0).astype(BF16))

    acc_ref[...] = jnp.zeros_like(acc_ref)
    q_pos = qs + lax.broadcasted_iota(jnp.int32, (1, tq), 1)

    def scores(kt):
        ks = pl.multiple_of(kt * tk, tk)
        ka = kaug_ref[pl.ds(ks, tk), :]
        return [_dot(ka, q_aug[e]) for e in range(2)]

    def softmax(kt, carry, masked):
        ks = pl.multiple_of(kt * tk, tk)
        new = []
        for e in range(2):
            m_old, l_old = carry[3 * e], carry[3 * e + 1]
            s = s_ref[e]
            if masked:
                k_pos = ks + lax.broadcasted_iota(jnp.int32, (tk, 1), 0)
                s = jnp.where(k_pos <= q_pos, s, NEG_INF)
            m_new = jnp.maximum(m_old, jnp.max(s, axis=0, keepdims=True))
            alpha = jnp.exp2(m_old - m_new)
            p = jnp.exp2(s - m_new)
            p_ref[e * tk:(e + 1) * tk, :] = p.astype(BF16)
            new += [m_new, alpha * l_old + jnp.sum(p, axis=0, keepdims=True), alpha]
        return tuple(new)

    def accumulate(kt, carry):
        ks = pl.multiple_of(kt * tk, tk)
        vt = jnp.concatenate([vt_ref[0, :, pl.ds(ks, tk)], vt_ref[1, :, pl.ds(ks, tk)]], axis=1)
        alpha = jnp.where(row < head_dim, carry[2], carry[5])
        acc_ref[...] = alpha * acc_ref[...] + _dot(vt, p_ref[...])

    def body(kt, carry):
        s_next = scores(kt + 1)
        accumulate(jnp.maximum(kt - 1, 0), carry)
        carry = softmax(kt, carry, False)
        s_ref[0] = s_next[0]
        s_ref[1] = s_next[1]
        return carry

    s_first = scores(0)
    s_ref[0] = s_first[0]
    s_ref[1] = s_first[1]
    p_ref[...] = jnp.zeros_like(p_ref)
    init = (jnp.full((1, tq), NEG_INF, F32), jnp.zeros((1, tq), F32), jnp.ones((1, tq), F32)) * 2
    carry = lax.fori_loop(0, qi, body, init)
    accumulate(jnp.maximum(qi - 1, 0), carry)
    carry = softmax(qi, carry, True)
    accumulate(qi, carry)
    out_t = acc_ref[...] / jnp.where(row < head_dim, carry[1], carry[4])
    o_ref[...] = out_t.T.astype(o_ref.dtype)


def prompt_attention(proj, cols, ccol, crow, *, nb, lp, head_dim, out_rows):
    mtot = out_rows
    nh = ccol.shape[1]
    pw = 2 * head_dim
    tq = _pick(lp, (384, 256, 128))
    nq = lp // tq
    return pl.pallas_call(
        functools.partial(_attn_kernel, tq=tq, head_dim=head_dim, lp=lp),
        grid=(nb, nh // 2, nq),
        in_specs=[pl.BlockSpec((tq, pw), lambda b, p, i: (b * nq + i, cols["q"] // pw + p)),
                  pl.BlockSpec((lp, pw), lambda b, p, i: (b, cols["k"] // pw + p)),
                  pl.BlockSpec((lp, pw), lambda b, p, i: (b, cols["v"] // pw + p)),
                  pl.BlockSpec((lp, nh), lambda b, p, i: (b, 0)),
                  pl.BlockSpec((None, nh, lp), lambda b, p, i: (b, 0, 0))],
        out_specs=pl.BlockSpec((tq, pw), lambda b, p, i: (b * nq + i, p)),
        out_shape=jax.ShapeDtypeStruct((mtot, nh * head_dim), BF16),
        scratch_shapes=[pltpu.VMEM((lp, 2 * pw), BF16), pltpu.VMEM((2, pw, lp), BF16), pltpu.VMEM((pw, tq), F32),
                        pltpu.VMEM((2, tq, tq), F32), pltpu.VMEM((2 * tq, tq), BF16)],
        compiler_params=_cparams(3),
        name="prompt_attention",
    )(proj, proj, proj, ccol, crow)


def _decode_attn_kernel(pt_ref, q_ref, kn_ref, vn_ref, lfn_ref, *refs, pages):
    k_refs = refs[0:pages]
    v_refs = refs[pages:2 * pages]
    lf_refs = refs[2 * pages:3 * pages]
    o_ref, qb_ref, s_ref, p_ref, a_ref, m_ref, l_ref, carry_ref, acc_ref = refs[3 * pages:]
    s_id = pl.program_id(1)
    nh, hd, psz = k_refs[0].shape

    @pl.when(s_id == 0)
    def _():
        qb_ref[...] = q_ref[...] * hd ** -0.5
        lane = lax.broadcasted_iota(jnp.int32, (1, psz), 1)
        for h in range(nh):
            s_ref[h:h + 1, :] = jnp.sum(kn_ref[h] * qb_ref[h], axis=0, keepdims=True)
            acc_ref[h] = jnp.where(lane == 0, vn_ref[h], 0.0)
        m_ref[...] = s_ref[:, 0:1]
        l_ref[...] = jnp.ones_like(l_ref)
        carry_ref[...] = lfn_ref[:, 0:1]

    r_i = lax.broadcasted_iota(jnp.int32, (psz, psz), 0)
    c_i = lax.broadcasted_iota(jnp.int32, (psz, psz), 1)
    later = (r_i > c_i).astype(BF16)
    for i in range(pages):
        for h in range(nh):
            s_ref[h:h + 1, :] = jnp.sum(k_refs[i][h] * qb_ref[h], axis=0, keepdims=True)
        lf = lf_refs[i][...]
        hi, mid, lo = _split3(lf)
        within = (_dot(hi.astype(BF16), later) + _dot(mid.astype(BF16), later)) + _dot(lo.astype(BF16), later)
        carry = carry_ref[...]
        s = s_ref[...] + (within + carry)
        carry_ref[...] = carry + jnp.sum(lf, axis=1, keepdims=True)
        m_old = m_ref[...]
        m_new = jnp.maximum(m_old, jnp.max(s, axis=1, keepdims=True))
        alpha = jnp.exp(m_old - m_new)
        p = jnp.exp(s - m_new)
        l_ref[...] = alpha * l_ref[...] + jnp.sum(p, axis=1, keepdims=True)
        m_ref[...] = m_new
        p_ref[...] = p
        a_ref[...] = jnp.broadcast_to(alpha, (nh, psz))
        for h in range(nh):
            acc_ref[h] = a_ref[h:h + 1, :] * acc_ref[h] + v_refs[i][h] * p_ref[h:h + 1, :]

    @pl.when(s_id == pl.num_programs(1) - 1)
    def _():
        for h in range(nh):
            o_h = jnp.sum(acc_ref[h], axis=1, keepdims=True) / l_ref[h:h + 1, :]
            o_ref[h] = jnp.broadcast_to(o_h, (hd, psz))


def decode_attention(q, k_new, v_new, logf_new, cache_k, cache_v, cache_logf, page_table, layer):
    nb, n_pages = page_table.shape
    _, _, psz, nh, hd = cache_k.shape
    pages = _pick(n_pages, (8, 4, 2, 1))
    kt = jnp.transpose(cache_k, (0, 1, 3, 4, 2))
    vt = jnp.transpose(cache_v, (0, 1, 3, 4, 2))
    lft = jnp.transpose(cache_logf, (0, 1, 3, 2))
    rep = lambda x: jnp.broadcast_to(x[..., None], x.shape + (psz,))

    def page(i, dims):
        def index(b, s, pt):
            return (pt[b, n_pages - 1 - (s * pages + i)], layer) + (0,) * len(dims)
        return pl.BlockSpec((None, None) + dims, index)

    tok = pl.BlockSpec((None, nh, hd, psz), lambda b, s, pt: (b, 0, 0, 0))
    grid_spec = pltpu.PrefetchScalarGridSpec(
        num_scalar_prefetch=1,
        grid=(nb, n_pages // pages),
        in_specs=([tok, tok, tok, pl.BlockSpec((None, nh, psz), lambda b, s, pt: (b, 0, 0))]
                  + [page(i, (nh, hd, psz)) for i in range(pages)]
                  + [page(i, (nh, hd, psz)) for i in range(pages)]
                  + [page(i, (nh, psz)) for i in range(pages)]),
        out_specs=tok,
        scratch_shapes=[pltpu.VMEM((nh, hd, psz), F32), pltpu.VMEM((nh, psz), F32), pltpu.VMEM((nh, psz), F32),
                        pltpu.VMEM((nh, psz), F32), pltpu.VMEM((nh, 1), F32), pltpu.VMEM((nh, 1), F32),
                        pltpu.VMEM((nh, 1), F32), pltpu.VMEM((nh, hd, psz), F32)],
    )
    out = pl.pallas_call(
        functools.partial(_decode_attn_kernel, pages=pages),
        grid_spec=grid_spec,
        out_shape=jax.ShapeDtypeStruct((nb, nh, hd, psz), F32),
        compiler_params=_cparams(2),
        name="decode_attention",
    )(page_table, rep(q), rep(k_new), rep(v_new), rep(logf_new),
      *([kt] * pages), *([vt] * pages), *([lft] * pages))
    return out[..., 0]


def _final_norm_shift_kernel(a_ref, b_ref, g_ref, o_ref, *, shift):
    x = jnp.concatenate([a_ref[shift:, :], b_ref[0:shift, :]], axis=0)
    ms = jnp.mean(x * x, axis=-1, keepdims=True)
    o_ref[...] = (x * lax.rsqrt(ms + NORM_EPS)) * g_ref[...]


def final_norm_prompt(x, g, *, nb, lp, seq, shift):
    d = x.shape[1]
    tr = SEQ_TILE
    nt = seq // tr
    nl = lp // tr
    return pl.pallas_call(
        functools.partial(_final_norm_shift_kernel, shift=shift),
        grid=(nb, nt),
        in_specs=[pl.BlockSpec((tr, d), lambda b, t: (b * nl + t, 0)),
                  pl.BlockSpec((tr, d), lambda b, t: (b * nl + t + 1, 0)),
                  pl.BlockSpec((1, d), lambda b, t: (0, 0))],
        out_specs=pl.BlockSpec((None, tr, d), lambda b, t: (b, t, 0)),
        out_shape=jax.ShapeDtypeStruct((nb, seq, d), F32),
        compiler_params=_cparams(2),
        name="final_norm_prompt",
    )(x, x, g.reshape(1, d))


def _final_norm_kernel(x_ref, g_ref, o_ref):
    x = x_ref[...]
    ms = jnp.mean(x * x, axis=-1, keepdims=True)
    o_ref[...] = (x * lax.rsqrt(ms + NORM_EPS)) * g_ref[...]


def final_norm(x, g):
    m, d = x.shape
    return pl.pallas_call(
        _final_norm_kernel,
        grid=(1,),
        in_specs=[pl.BlockSpec((m, d), lambda i: (0, 0)), pl.BlockSpec((1, d), lambda i: (0, 0))],
        out_specs=pl.BlockSpec((m, d), lambda i: (0, 0)),
        out_shape=jax.ShapeDtypeStruct((m, d), F32),
        compiler_params=_cparams(1),
        name="final_norm",
    )(x, g.reshape(1, d))


def _pack_kernel(offs_ref, w_ref, ws_ref, o_ref, *, small_tile):
    w = jnp.where(pl.program_id(1) == small_tile, ws_ref[...], w_ref[0])
    o_ref[...] = w.T.astype(o_ref.dtype)


def pack_w_in(w_in_t, dims):
    depth, n_out, d = w_in_t.shape
    pd, di, cd, nh_s, ad, nh_a, dm = dims
    tile = SMALL_W
    src = {}
    o = 0
    for name, width in (("pool", pd), ("z", di), ("xbc", cd), ("dt", nh_s), ("q", ad), ("k", ad), ("v", ad),
                        ("f", nh_a), ("gate", n_out - (pd + di + cd + nh_s + 3 * ad + nh_a))):
        src[name] = (o, width)
        o += width
    assert DT_LANE == 0 and nh_s <= F_LANE
    rows = lambda name: w_in_t[:, src[name][0]:src[name][0] + src[name][1]]
    zeros = lambda w: jnp.zeros((depth, w, d), w_in_t.dtype)
    small = jnp.concatenate([rows("dt"), zeros(F_LANE - nh_s), rows("f"), zeros(tile - F_LANE - nh_a)], axis=1)
    cols, offs = {}, []
    for name in ("xbc", "small", "pool", "z", "q", "k", "v", "gate"):
        cols[name] = len(offs) * tile
        if name == "small":
            small_tile = len(offs)
            offs.append(0)
        else:
            start, width = src[name]
            assert width % tile == 0 and start % SUBLANES == 0 and tile % SUBLANES == 0
            offs += [(start + t * tile) // SUBLANES for t in range(width // tile)]
    grid_spec = pltpu.PrefetchScalarGridSpec(
        num_scalar_prefetch=1,
        grid=(depth, len(offs)),
        in_specs=[pl.BlockSpec((pl.Element(1), pl.Element(tile), pl.Element(d)), lambda l, j, o: (l, o[j] * SUBLANES, 0)),
                  pl.BlockSpec((None, tile, d), lambda l, j, o: (l, 0, 0))],
        out_specs=pl.BlockSpec((None, d, tile), lambda l, j, o: (l, 0, j)),
    )
    packed = pl.pallas_call(
        functools.partial(_pack_kernel, small_tile=small_tile),
        grid_spec=grid_spec,
        out_shape=jax.ShapeDtypeStruct((depth, d, len(offs) * tile), BF16),
        compiler_params=_cparams(2),
        name="pack_w_in",
    )(jnp.asarray(offs, jnp.int32), w_in_t, small)
    return packed, cols


def kernel(x_prompt, x_sample, cache_k, cache_v, cache_logf, page_table, state_ssm, state_conv, state_pool,
           meta_tokens, norm_ffn1, ffn1_w_gu, ffn1_w_down, norm_mix, w_in, pool_w, pool_scale, conv_w, conv_b,
           dt_bias, a_log, d_skip, ssm_norm, forget_bias, w_br_pool, w_br_ssm, w_br_attn, w_out, norm_ffn2,
           ffn2_w_gu, ffn2_w_down, final_norm_w):
    nb, seq, dm = x_prompt.shape
    ns = x_sample.shape[0]
    depth = w_in.shape[0]
    n_meta = meta_tokens.shape[0]
    nh_a, hd_a = cache_k.shape[3], cache_k.shape[4]
    ad = nh_a * hd_a
    psz = cache_k.shape[2]
    past_len = page_table.shape[1] * psz
    nh_s, hd_s, n_state = state_ssm.shape[2], state_ssm.shape[3], state_ssm.shape[4]
    di = nh_s * hd_s
    cd = state_conv.shape[3]
    conv_hist = state_conv.shape[2]
    pd = state_pool.shape[3]
    pool_hist = state_pool.shape[2]
    l = seq + n_meta
    lp = -(-l // SEQ_TILE) * SEQ_TILE
    assert x_sample.shape[1] == 1 and seq % SEQ_TILE == 0 and n_meta % 8 == 0 and n_meta < SEQ_TILE
    assert 2 * hd_a == LANES and 2 * hd_s == LANES and n_state == LANES
    assert pool_hist < POOL_HIST_ROWS and conv_hist < CONV_HIST_ROWS

    mp = nb * lp
    m_all = mp + ns * DEC_ROWS
    x = jnp.concatenate(
        [jnp.concatenate([jnp.broadcast_to(meta_tokens[None], (nb, n_meta, dm)), x_prompt,
                          jnp.zeros((nb, lp - l, dm), F32)], axis=1).reshape(mp, dm),
         jnp.pad(x_sample, ((0, 0), (0, DEC_ROWS - 1), (0, 0))).reshape(ns * DEC_ROWS, dm)], axis=0)

    zero_pool_hist = jnp.zeros((nb, POOL_HIST_ROWS, pd), F32)
    zero_conv_hist = jnp.zeros((nb, CONV_HIST_ROWS, cd), F32)
    zero_state = jnp.zeros((nb, di, n_state), F32)
    pool_w_b = pool_w.astype(BF16)
    w_in_p, cols = pack_w_in(jnp.swapaxes(w_in, 1, 2), (pd, di, cd, nh_s, ad, nh_a, dm))

    outs = {k: [] for k in ("kp", "vp", "lfp", "hp", "cp", "pp", "ks", "vs", "lfs", "hs", "cs", "ps")}
    for i in range(depth):
        pw = pool_w_b[i]

        x = matmul_residual(norm_swiglu(x, norm_ffn1[i], ffn1_w_gu, i), ffn1_w_down, i, x, 0.5)
        proj = norm_matmul(x, norm_mix[i], w_in_p, i)

        pool_p, lf_p, ccol, crow = pool_and_forget(
            proj, cols, zero_pool_hist, pw, pool_scale[i], forget_bias[i],
            nb=nb, nt=lp // SEQ_TILE, rows=SEQ_TILE, row0=0, pos0=0, n_valid=l, with_cum=True, out_rows=m_all)
        ssm_p, h_p = ssd_mixer(proj, cols, zero_conv_hist, zero_state, conv_w[i], conv_b[i], dt_bias[i], a_log[i],
                               d_skip[i], ssm_norm[i], nb=nb, nt=lp // SEQ_TILE, rows=SEQ_TILE, row0=0, n_valid=l,
                               out_rows=m_all)
        att_p = prompt_attention(proj, cols, ccol, crow, nb=nb, lp=lp, head_dim=hd_a, out_rows=m_all)

        hist_p = jnp.pad(state_pool[:, i], ((0, 0), (POOL_HIST_ROWS - pool_hist, 0), (0, 0)))
        hist_c = jnp.pad(state_conv[:, i], ((0, 0), (CONV_HIST_ROWS - conv_hist, 0), (0, 0)))
        pool_s, lf_s = pool_and_forget(
            proj, cols, hist_p, pw, pool_scale[i], forget_bias[i],
            nb=ns, nt=1, rows=DEC_ROWS, row0=mp, pos0=past_len, n_valid=1, with_cum=False)
        ssm_s, h_s = ssd_mixer(proj, cols, hist_c, state_ssm[:, i].reshape(ns, di, n_state), conv_w[i], conv_b[i],
                               dt_bias[i], a_log[i], d_skip[i], ssm_norm[i],
                               nb=ns, nt=1, rows=DEC_ROWS, row0=mp, n_valid=1)
        ps0 = proj[mp:].reshape(ns, DEC_ROWS, -1)[:, 0:1]
        heads = lambda name: ps0[:, 0, cols[name]:cols[name] + ad].reshape(ns, nh_a, hd_a)
        att_s = decode_attention(heads("q"), heads("k"), heads("v"), lf_s.reshape(ns, DEC_ROWS, nh_a)[:, 0],
                                 cache_k, cache_v, cache_logf, page_table, i)
        att_s = jnp.pad(att_s.reshape(ns, 1, ad), ((0, 0), (0, DEC_ROWS - 1), (0, 0)))
        att_s = att_s.reshape(ns * DEC_ROWS, ad).astype(BF16)

        place = lambda full, part: lax.dynamic_update_slice(full, part, (mp, 0))
        merged = gated_merge(place(pool_p, pool_s), place(ssm_p, ssm_s), place(att_p, att_s), proj, cols["gate"],
                             w_br_pool, w_br_ssm, w_br_attn, i)
        x = matmul_residual(merged, w_out, i, x, 1.0)
        x = matmul_residual(norm_swiglu(x, norm_ffn2[i], ffn2_w_gu, i), ffn2_w_down, i, x, 0.5)

        def prompt_rows(first, last, name, width):
            return jnp.stack([proj[b * lp + first:b * lp + last, cols[name]:cols[name] + width] for b in range(nb)])

        outs["kp"].append(prompt_rows(0, l, "k", ad).reshape(nb, l, nh_a, hd_a))
        outs["vp"].append(prompt_rows(0, l, "v", ad).reshape(nb, l, nh_a, hd_a))
        outs["lfp"].append(lf_p.reshape(nb, lp, nh_a)[:, :l])
        outs["hp"].append(h_p.reshape(nb, nh_s, hd_s, n_state))
        outs["cp"].append(prompt_rows(l - conv_hist, l, "xbc", cd))
        outs["pp"].append(prompt_rows(l - pool_hist, l, "pool", pd))
        outs["ks"].append(ps0[:, :, cols["k"]:cols["k"] + ad].reshape(ns, 1, nh_a, hd_a))
        outs["vs"].append(ps0[:, :, cols["v"]:cols["v"] + ad].reshape(ns, 1, nh_a, hd_a))
        outs["lfs"].append(lf_s.reshape(ns, DEC_ROWS, nh_a)[:, 0:1])
        outs["hs"].append(h_s.reshape(ns, nh_s, hd_s, n_state))
        outs["cs"].append(jnp.concatenate([state_conv[:, i, 1:], ps0[:, :, cols["xbc"]:cols["xbc"] + cd]], axis=1))
        outs["ps"].append(jnp.concatenate([state_pool[:, i, 1:], ps0[:, :, cols["pool"]:cols["pool"] + pd]], axis=1))

    y_prompt = final_norm_prompt(x, final_norm_w, nb=nb, lp=lp, seq=seq, shift=n_meta)
    y_sample = final_norm(x[mp:], final_norm_w).reshape(ns, DEC_ROWS, dm)[:, 0:1]
    st = lambda k: jnp.stack(outs[k], axis=1)
    return (y_prompt, y_sample, st("kp"), st("vp"), st("lfp"), st("ks"), st("vs"), st("lfs"),
            st("hp"), st("hs"), st("cp"), st("cs"), st("pp"), st("ps"))
```

```python
import functools
import math

import jax
import jax.numpy as jnp
from jax import lax
from jax.experimental import pallas as pl
from jax.experimental.pallas import tpu as pltpu

F32 = jnp.float32
BF16 = jnp.bfloat16

NORM_EPS = 1e-6
NEG_INF = -1e30
LOG2E = 1.4426950408889634
POOL_WINDOWS = (2, 4, 8, 16)
POOL_HIST_ROWS = 16
CONV_TAPS = 4
CONV_HIST_ROWS = 8
SEQ_TILE = 128
DEC_ROWS = 16
LANES = 128
SUBLANES = 8
SMALL_W = 512
DT_LANE = 0
F_LANE = 128
VMEM_LIMIT_BYTES = 56 * 1024 * 1024
ROW_TILES = (1072, 1056, 768, 704, 528, 512, 384, 256, 128)


def _cparams(n_axes):
    return pltpu.CompilerParams(dimension_semantics=("arbitrary",) * n_axes,
                                vmem_limit_bytes=VMEM_LIMIT_BYTES)


def _pick(n, cands):
    for c in cands:
        if n % c == 0:
            return c
    return n


def _sigmoid(x):
    return 1.0 / (1.0 + jnp.exp(-x))


def _softplus(x):
    return jnp.maximum(x, 0.0) + jnp.log(1.0 + jnp.exp(-jnp.abs(x)))


def _dot(a, b):
    return jnp.dot(a, b, preferred_element_type=F32)


def _dot_nt(a, b):
    return lax.dot_general(a, b, (((1,), (1,)), ((), ())), preferred_element_type=F32)


def _split3(x):
    hi = x.astype(BF16).astype(F32)
    r = x - hi
    mid = r.astype(BF16).astype(F32)
    lo = (r - mid).astype(BF16).astype(F32)
    return hi, mid, lo


def _select_dot(sel, x):
    s = sel.astype(BF16)
    hi, mid, lo = _split3(x)
    return (_dot(s, hi.astype(BF16)) + _dot(s, mid.astype(BF16))) + _dot(s, lo.astype(BF16))


def _dot_select(x, sel):
    s = sel.astype(BF16)
    hi, mid, lo = _split3(x)
    return (_dot(hi.astype(BF16), s) + _dot(mid.astype(BF16), s)) + _dot(lo.astype(BF16), s)


def _rms_to_scratch(x_ref, g_ref, xn_ref):
    x = x_ref[...]
    ms = jnp.mean(x * x, axis=-1, keepdims=True)
    xn_ref[...] = ((x * lax.rsqrt(ms + NORM_EPS)) * g_ref[...]).astype(BF16)


def _norm_swiglu_kernel(x_ref, g_ref, wg_ref, wu_ref, o_ref, xn_ref):
    @pl.when(pl.program_id(1) == 0)
    def _():
        _rms_to_scratch(x_ref, g_ref, xn_ref)

    xn = xn_ref[...]
    gate = _dot(xn, wg_ref[...].astype(BF16))
    up = _dot(xn, wu_ref[...].astype(BF16))
    o_ref[...] = ((gate * _sigmoid(gate)) * up).astype(o_ref.dtype)


def norm_swiglu(x, g, w_gu, layer):
    m, d = x.shape
    f = w_gu.shape[2] // 2
    tm = _pick(m, ROW_TILES)
    tn = _pick(f, (512, 256, 128))
    nj = f // tn
    return pl.pallas_call(
        _norm_swiglu_kernel,
        grid=(m // tm, nj),
        in_specs=[pl.BlockSpec((tm, d), lambda i, j: (i, 0)),
                  pl.BlockSpec((1, d), lambda i, j: (0, 0)),
                  pl.BlockSpec((None, d, tn), lambda i, j: (layer, 0, j)),
                  pl.BlockSpec((None, d, tn), lambda i, j: (layer, 0, j + nj))],
        out_specs=pl.BlockSpec((tm, tn), lambda i, j: (i, j)),
        out_shape=jax.ShapeDtypeStruct((m, f), BF16),
        scratch_shapes=[pltpu.VMEM((tm, d), BF16)],
        compiler_params=_cparams(2),
        name="norm_swiglu",
    )(x, g.reshape(1, d), w_gu, w_gu)


def _matmul_residual_kernel(h_ref, w_ref, x_ref, o_ref, *, scale):
    o_ref[...] = x_ref[...] + scale * _dot(h_ref[...], w_ref[...].astype(BF16))


def matmul_residual(h, w, layer, x, scale):
    m, k = h.shape
    n = w.shape[2]
    tm = _pick(m, ROW_TILES)
    tn = _pick(n, (512, 256, 128) if k * 512 * 4 <= 8 * 1024 * 1024 else (256, 128))
    return pl.pallas_call(
        functools.partial(_matmul_residual_kernel, scale=scale),
        grid=(m // tm, n // tn),
        in_specs=[pl.BlockSpec((tm, k), lambda i, j: (i, 0)),
                  pl.BlockSpec((None, k, tn), lambda i, j: (layer, 0, j)),
                  pl.BlockSpec((tm, tn), lambda i, j: (i, j))],
        out_specs=pl.BlockSpec((tm, tn), lambda i, j: (i, j)),
        out_shape=jax.ShapeDtypeStruct((m, n), F32),
        compiler_params=_cparams(2),
        name="matmul_residual",
    )(h, w, x)


def _norm_matmul_kernel(x_ref, g_ref, w_ref, o_ref, xn_ref):
    @pl.when(pl.program_id(1) == 0)
    def _():
        _rms_to_scratch(x_ref, g_ref, xn_ref)

    o_ref[...] = _dot(xn_ref[...], w_ref[...])


def norm_matmul(x, g, w, layer):
    m, d = x.shape
    n = w.shape[2]
    tm = _pick(m, ROW_TILES)
    tn = _pick(n, (1024, 512, 256, 128))
    return pl.pallas_call(
        _norm_matmul_kernel,
        grid=(m // tm, n // tn),
        in_specs=[pl.BlockSpec((tm, d), lambda i, j: (i, 0)),
                  pl.BlockSpec((1, d), lambda i, j: (0, 0)),
                  pl.BlockSpec((None, d, tn), lambda i, j: (layer, 0, j))],
        out_specs=pl.BlockSpec((tm, tn), lambda i, j: (i, j)),
        out_shape=jax.ShapeDtypeStruct((m, n), F32),
        scratch_shapes=[pltpu.VMEM((tm, d), BF16)],
        compiler_params=_cparams(2),
        name="norm_matmul",
    )(x, g.reshape(1, d), w)


def _merge_kernel(p_ref, s_ref, a_ref, g0_ref, g1_ref, g2_ref, wp_ref, ws_ref, wa_ref, o_ref):
    merged = (_sigmoid(g0_ref[...]) * _dot(p_ref[...], wp_ref[...].astype(BF16))
              + _sigmoid(g1_ref[...]) * _dot(s_ref[...], ws_ref[...].astype(BF16))
              + _sigmoid(g2_ref[...]) * _dot(a_ref[...], wa_ref[...].astype(BF16)))
    o_ref[...] = merged.astype(o_ref.dtype)


def gated_merge(pool_out, ssm_out, att, proj, gate_col, w_pool, w_ssm, w_attn, layer):
    m, kb = pool_out.shape
    d = w_pool.shape[2]
    tm = _pick(m, ROW_TILES)
    tn = _pick(d, (512, 256, 128))
    g_blk = gate_col // tn
    nj = d // tn
    branch = pl.BlockSpec((tm, kb), lambda i, j: (i, 0))
    weight = pl.BlockSpec((None, kb, tn), lambda i, j: (layer, 0, j))

    def gate(r):
        return pl.BlockSpec((tm, tn), lambda i, j: (i, g_blk + r * nj + j))

    return pl.pallas_call(
        _merge_kernel,
        grid=(m // tm, nj),
        in_specs=[branch, branch, branch, gate(0), gate(1), gate(2), weight, weight, weight],
        out_specs=pl.BlockSpec((tm, tn), lambda i, j: (i, j)),
        out_shape=jax.ShapeDtypeStruct((m, d), BF16),
        compiler_params=_cparams(2),
        name="gated_merge",
    )(pool_out, ssm_out, att, proj, proj, proj, w_pool, w_ssm, w_attn)


def _prep_kernel(xp_ref, sm_ref, hist_ref, pw_ref, ps_ref, fb_ref, *refs, rows, pos0, n_valid, with_cum):
    if with_cum:
        pool_ref, logf_ref, ccol_ref, crow_ref, full_ref, carry_ref = refs
    else:
        pool_ref, logf_ref, full_ref = refs
    t = pl.program_id(1)
    hr = POOL_HIST_ROWS

    @pl.when(t == 0)
    def _():
        full_ref[0:hr, :] = hist_ref[...]
        if with_cum:
            carry_ref[...] = jnp.zeros_like(carry_ref)

    x = xp_ref[...]
    full_ref[hr:hr + rows, :] = x
    row = t * rows + lax.broadcasted_iota(jnp.int32, (rows, 1), 0)
    gd = x.shape[1] // len(POOL_WINDOWS)
    for g, w in enumerate(POOL_WINDOWS):
        cols = slice(g * gd, (g + 1) * gd)
        s = full_ref[hr:hr + rows, cols]
        for k in range(1, w):
            s = s + full_ref[hr - k:hr - k + rows, cols]
        cnt = jnp.minimum(w, pos0 + row + 1).astype(F32)
        pooled = s / cnt - x[:, cols]
        mixed = _dot(pooled.astype(BF16), pw_ref[g])
        pool_ref[:, cols] = (mixed * ps_ref[:, cols]).astype(pool_ref.dtype)
    full_ref[0:hr, :] = full_ref[rows:rows + hr, :]

    z = sm_ref[:, F_LANE:F_LANE + logf_ref.shape[1]] + fb_ref[...]
    logf = -_softplus(-z)
    logf_ref[...] = logf
    if with_cum:
        lf = jnp.where(row < n_valid, logf, 0.0)
        r_i = lax.broadcasted_iota(jnp.int32, (rows, rows), 0)
        c_i = lax.broadcasted_iota(jnp.int32, (rows, rows), 1)
        cs = _select_dot(r_i >= c_i, lf) + carry_ref[...]
        ccol_ref[...] = cs
        carry_ref[...] = cs[rows - 1:rows, :]
        nh = cs.shape[1]
        wide = jnp.concatenate([cs, jnp.zeros((rows, LANES - nh), F32)], axis=1)
        crow_ref[...] = wide.T[0:nh, :]


def pool_and_forget(proj, cols, hist, pool_w, pool_scale, forget_bias, *, nb, nt, rows, row0, pos0, n_valid,
                    with_cum, out_rows=None):
    mtot = nb * nt * rows
    blk0 = row0 // rows
    pd = pool_w.shape[0] * pool_w.shape[1]
    nh = forget_bias.shape[0]
    xp_blk = cols["pool"] // pd
    sm_blk = cols["small"] // SMALL_W
    in_specs = [pl.BlockSpec((rows, pd), lambda b, t: (blk0 + b * nt + t, xp_blk)),
                pl.BlockSpec((rows, SMALL_W), lambda b, t: (blk0 + b * nt + t, sm_blk)),
                pl.BlockSpec((None, POOL_HIST_ROWS, pd), lambda b, t: (b, 0, 0)),
                pl.BlockSpec(pool_w.shape, lambda b, t: (0, 0, 0)),
                pl.BlockSpec((1, pd), lambda b, t: (0, 0)),
                pl.BlockSpec((1, nh), lambda b, t: (0, 0))]
    out_specs = [pl.BlockSpec((rows, pd), lambda b, t: (b * nt + t, 0)),
                 pl.BlockSpec((rows, nh), lambda b, t: (b * nt + t, 0))]
    out_shape = [jax.ShapeDtypeStruct((out_rows or mtot, pd), BF16), jax.ShapeDtypeStruct((mtot, nh), F32)]
    scratch = [pltpu.VMEM((POOL_HIST_ROWS + rows, pd), F32)]
    if with_cum:
        out_specs += [pl.BlockSpec((rows, nh), lambda b, t: (b * nt + t, 0)),
                      pl.BlockSpec((None, nh, rows), lambda b, t: (b, 0, t))]
        out_shape += [jax.ShapeDtypeStruct((mtot, nh), F32), jax.ShapeDtypeStruct((nb, nh, nt * rows), F32)]
        scratch += [pltpu.VMEM((1, nh), F32)]
    return pl.pallas_call(
        functools.partial(_prep_kernel, rows=rows, pos0=pos0, n_valid=n_valid, with_cum=with_cum),
        grid=(nb, nt),
        in_specs=in_specs, out_specs=out_specs, out_shape=out_shape, scratch_shapes=scratch,
        compiler_params=_cparams(2),
        name="pool_and_forget",
    )(proj, proj, hist, pool_w, pool_scale.reshape(1, pd), forget_bias.reshape(1, nh))


def _ssd_kernel(xz_ref, z_ref, hist_ref, h0_ref, cw_ref, cb_ref, dtb_ref, alog_ref, dskip_ref, nw_ref,
                ex_ref, ext_ref, o_ref, h_ref, full_ref, y_ref, *, rows, n_valid, d_inner, n_state, head_dim):
    t = SEQ_TILE
    c = pl.program_id(1)
    hr = CONV_HIST_ROWS
    cdim = full_ref.shape[1]
    nheads = d_inner // head_dim
    ngroups = (cdim - d_inner) // (2 * n_state)
    pair_w = 2 * head_dim
    pairs_per_group = d_inner // pair_w // ngroups

    @pl.when(c == 0)
    def _():
        full_ref[0:hr, :] = hist_ref[...]
        h_ref[...] = h0_ref[...]

    full_ref[hr:hr + rows, :] = xz_ref[:, 0:cdim]
    if rows < t:
        full_ref[hr + rows:hr + t, :] = jnp.zeros((t - rows, cdim), F32)
    conv = cb_ref[...]
    for k in range(CONV_TAPS):
        off = hr - (CONV_TAPS - 1) + k
        conv = conv + full_ref[off:off + t, :] * cw_ref[k:k + 1, :]
    act = conv * _sigmoid(conv)
    full_ref[0:hr, :] = full_ref[t:t + hr, :]

    dt_raw = xz_ref[:, cdim + DT_LANE:cdim + DT_LANE + nheads]
    if rows < t:
        dt_raw = jnp.concatenate([dt_raw, jnp.zeros((t - rows, nheads), F32)], axis=0)
    row = c * t + lax.broadcasted_iota(jnp.int32, (t, 1), 0)
    dt = jnp.where(row < n_valid, _softplus(dt_raw + dtb_ref[...]), 0.0)
    a = -jnp.exp(alog_ref[...])
    r_i = lax.broadcasted_iota(jnp.int32, (t, t), 0)
    c_i = lax.broadcasted_iota(jnp.int32, (t, t), 1)
    causal = r_i >= c_i
    acum = _select_dot(causal, dt * a)
    lane_pad = jnp.zeros((t, LANES - nheads), F32)
    acum_row = jnp.concatenate([acum, lane_pad], axis=1).T[0:nheads, :]
    dt_row = jnp.concatenate([dt, lane_pad], axis=1).T[0:nheads, :]
    a_last = acum[t - 1:t, :]
    decay_in = jnp.exp(acum)
    to_end = jnp.exp(a_last - acum) * dt
    wide = _dot_select(jnp.concatenate([decay_in, to_end], axis=0), ex_ref[...])
    decay_in_x = wide[0:t, :]
    to_end_x = wide[t:2 * t, :]
    chunk_decay = jnp.broadcast_to(jnp.exp(acum_row[:, t - 1:t]), (nheads, n_state))
    chunk_decay_x = _select_dot(ext_ref[...], chunk_decay)

    xs = act[:, 0:d_inner]
    xs_b = xs.astype(BF16)
    lane = lax.broadcasted_iota(jnp.int32, (t, pair_w), 1)
    first = lane < head_dim
    for g in range(ngroups):
        bm = act[:, d_inner + g * n_state:d_inner + (g + 1) * n_state].astype(BF16)
        cm = act[:, d_inner + (ngroups + g) * n_state:d_inner + (ngroups + g + 1) * n_state].astype(BF16)
        cb = _dot_nt(cm, bm)
        for j in range(pairs_per_group):
            p = g * pairs_per_group + j
            cols = slice(p * pair_w, (p + 1) * pair_w)
            wts = []
            for h in (2 * p, 2 * p + 1):
                seg = acum[:, h:h + 1] - acum_row[h:h + 1, :]
                decay = jnp.exp(jnp.where(causal, seg, -jnp.inf))
                wts.append((cb * decay) * dt_row[h:h + 1, :])
            lhs = jnp.concatenate(wts, axis=1).astype(BF16)
            xp = xs_b[:, cols]
            zero = jnp.zeros_like(xp)
            rhs = jnp.concatenate([jnp.where(first, xp, zero), jnp.where(first, zero, xp)], axis=0)
            y = _dot(lhs, rhs)
            hp = h_ref[p * pair_w:(p + 1) * pair_w, :]
            y = y + _dot_nt(cm, hp.astype(BF16)) * decay_in_x[:, cols]
            y_ref[:, cols] = y + dskip_ref[:, cols] * xs[:, cols]
            upd = (xs[:, cols] * to_end_x[:, cols]).T.astype(BF16)
            h_ref[p * pair_w:(p + 1) * pair_w, :] = chunk_decay_x[p * pair_w:(p + 1) * pair_w, :] * hp + _dot(upd, bm)

    zz = z_ref[...]
    if rows < t:
        zz = jnp.concatenate([zz, jnp.zeros((t - rows, d_inner), F32)], axis=0)
    gated = y_ref[...] * (zz * _sigmoid(zz))
    ms = jnp.mean(gated * gated, axis=-1, keepdims=True)
    out = (gated * lax.rsqrt(ms + NORM_EPS)) * nw_ref[...]
    o_ref[...] = out[0:rows, :].astype(o_ref.dtype)


def ssd_mixer(proj, cols, hist, h0, conv_w, conv_b, dt_bias, a_log, d_skip, ssm_norm, *, nb, nt, rows, row0,
              n_valid, out_rows=None):
    mtot = nb * nt * rows
    blk0 = row0 // rows
    cdim = conv_w.shape[1]
    nheads = dt_bias.shape[0]
    d_inner, n_state = h0.shape[1], h0.shape[2]
    head_dim = d_inner // nheads
    xz_w = cdim + SMALL_W
    expand = jnp.repeat(jnp.eye(nheads, dtype=F32), head_dim, axis=1)
    const2 = lambda b, t: (0, 0)
    return pl.pallas_call(
        functools.partial(_ssd_kernel, rows=rows, n_valid=n_valid, d_inner=d_inner, n_state=n_state,
                          head_dim=head_dim),
        grid=(nb, nt),
        in_specs=[pl.BlockSpec((rows, xz_w), lambda b, t: (blk0 + b * nt + t, cols["xbc"] // xz_w)),
                  pl.BlockSpec((rows, d_inner), lambda b, t: (blk0 + b * nt + t, cols["z"] // d_inner)),
                  pl.BlockSpec((None, CONV_HIST_ROWS, cdim), lambda b, t: (b, 0, 0)),
                  pl.BlockSpec((None, d_inner, n_state), lambda b, t: (b, 0, 0)),
                  pl.BlockSpec((CONV_TAPS, cdim), const2),
                  pl.BlockSpec((1, cdim), const2),
                  pl.BlockSpec((1, nheads), const2),
                  pl.BlockSpec((1, nheads), const2),
                  pl.BlockSpec((1, d_inner), const2),
                  pl.BlockSpec((1, d_inner), const2),
                  pl.BlockSpec((nheads, d_inner), const2),
                  pl.BlockSpec((d_inner, nheads), const2)],
        out_specs=[pl.BlockSpec((rows, d_inner), lambda b, t: (b * nt + t, 0)),
                   pl.BlockSpec((None, d_inner, n_state), lambda b, t: (b, 0, 0))],
        out_shape=[jax.ShapeDtypeStruct((out_rows or mtot, d_inner), BF16),
                   jax.ShapeDtypeStruct((nb, d_inner, n_state), F32)],
        scratch_shapes=[pltpu.VMEM((CONV_HIST_ROWS + SEQ_TILE, cdim), F32),
                        pltpu.VMEM((SEQ_TILE, d_inner), F32)],
        compiler_params=_cparams(2),
        name="ssd_mixer",
    )(proj, proj, hist, h0, conv_w, conv_b.reshape(1, cdim), dt_bias.reshape(1, nheads),
      a_log.reshape(1, nheads), jnp.repeat(d_skip, head_dim).reshape(1, d_inner),
      ssm_norm.reshape(1, d_inner), expand, expand.T)


def _attn_kernel(q_ref, k_ref, v_ref, ccol_ref, crow_ref, o_ref, kaug_ref, vt_ref, acc_ref, sa_ref, sb_ref, p_ref, *, tq,
                 head_dim, lp):
    pair = pl.program_id(1)
    qi = pl.program_id(2)
    nh = ccol_ref.shape[1]
    pw = 2 * head_dim
    tk = tq
    n_parts = 3
    one_lane = 2 * n_parts

    @pl.when(qi == 0)
    def _():
        hid = lax.broadcasted_iota(jnp.int32, (nh, pw), 0)
        lid = lax.broadcasted_iota(jnp.int32, (nh, pw), 1)
        sel = jnp.where((hid == 2 * pair) & (lid < n_parts), 1.0,
                        jnp.where((hid == 2 * pair + 1) & (lid >= n_parts) & (lid < one_lane), 1.0, 0.0))
        lane = lax.broadcasted_iota(jnp.int32, (SEQ_TILE, pw), 1)
        part = lane - jnp.where(lane >= n_parts, n_parts, 0)

        def body(i, carry):
            r = pl.multiple_of(i * SEQ_TILE, SEQ_TILE)
            c2 = _dot_select(ccol_ref[pl.ds(r, SEQ_TILE), :], sel) * (-LOG2E)
            hi, mid, lo = _split3(c2)
            bias = jnp.where(part == 0, hi, jnp.where(part == 1, mid, lo))
            bias = jnp.where(lane < one_lane, bias, jnp.where(lane < one_lane + n_parts, 1.0, 0.0))
            kaug_ref[pl.ds(r, SEQ_TILE), 0:pw] = k_ref[pl.ds(r, SEQ_TILE), :].astype(BF16)
            kaug_ref[pl.ds(r, SEQ_TILE), pw:2 * pw] = bias.astype(BF16)
            vt = v_ref[pl.ds(r, SEQ_TILE), :].T
            d_row = lax.broadcasted_iota(jnp.int32, vt.shape, 0)
            vt_ref[0, :, pl.ds(r, SEQ_TILE)] = jnp.where(d_row < head_dim, vt, 0.0).astype(BF16)
            vt_ref[1, :, pl.ds(r, SEQ_TILE)] = jnp.where(d_row < head_dim, 0.0, vt).astype(BF16)
            return carry

        lax.fori_loop(0, lp // SEQ_TILE, body, 0)

    qs = pl.multiple_of(qi * tq, tq)
    qt = (q_ref[...] * (head_dim ** -0.5 * LOG2E)).T
    row = lax.broadcasted_iota(jnp.int32, (pw, tq), 0)
    q_aug = []
    for e in range(2):
        head_rows = (row < head_dim) if e == 0 else (row >= head_dim)
        c_q = crow_ref[pl.ds(2 * pair + e, 1), pl.ds(qs, tq)] * LOG2E
        hi, mid, lo = _split3(c_q)
        ones_rows = (row >= n_parts * e) & (row < n_parts * (e + 1))
        qb = jnp.where(ones_rows, 1.0,
                       jnp.where(row == one_lane, hi, jnp.where(row == one_lane + 1, mid,
                                                                jnp.where(row == one_lane + 2, lo, 0.0))))
        q_aug.append(jnp.concatenate([jnp.where(head_rows, qt, 0.0), qb], axis=0).astype(BF16))

    acc_ref[...] = jnp.zeros_like(acc_ref)
    q_pos = qs + lax.broadcasted_iota(jnp.int32, (1, tq), 1)

    def scores(kt):
        ks = pl.multiple_of(kt * tk, tk)
        ka = kaug_ref[pl.ds(ks, tk), :]
        return [_dot(ka, q_aug[e]) for e in range(2)]

    def softmax(kt, s_ref, carry, masked):
        ks = pl.multiple_of(kt * tk, tk)
        new = []
        for e in range(2):
            m_old, l_old = carry[3 * e], carry[3 * e + 1]
            s = s_ref[e]
            if masked:
                k_pos = ks + lax.broadcasted_iota(jnp.int32, (tk, 1), 0)
                s = jnp.where(k_pos <= q_pos, s, NEG_INF)
            m_new = jnp.maximum(m_old, jnp.max(s, axis=0, keepdims=True))
            alpha = jnp.exp2(m_old - m_new)
            p = jnp.exp2(s - m_new)
            p_ref[e * tk:(e + 1) * tk, :] = p.astype(BF16)
            new += [m_new, alpha * l_old + jnp.sum(p, axis=0, keepdims=True), alpha]
        return tuple(new)

    def accumulate(kt, carry):
        ks = pl.multiple_of(kt * tk, tk)
        vt = jnp.concatenate([vt_ref[0, :, pl.ds(ks, tk)], vt_ref[1, :, pl.ds(ks, tk)]], axis=1)
        alpha = jnp.where(row < head_dim, carry[2], carry[5])
        acc_ref[...] = alpha * acc_ref[...] + _dot(vt, p_ref[...])

    def stage(kt, cur_ref, nxt_ref, carry):
        s_next = scores(kt + 1)
        accumulate(jnp.maximum(kt - 1, 0), carry)
        carry = softmax(kt, cur_ref, carry, False)
        nxt_ref[0] = s_next[0]
        nxt_ref[1] = s_next[1]
        return carry

    def pair_body(j, carry):
        carry = stage(2 * j, sa_ref, sb_ref, carry)
        return stage(2 * j + 1, sb_ref, sa_ref, carry)

    def finish(diag_ref, carry):
        accumulate(jnp.maximum(qi - 1, 0), carry)
        carry = softmax(qi, diag_ref, carry, True)
        accumulate(qi, carry)
        out_t = acc_ref[...] / jnp.where(row < head_dim, carry[1], carry[4])
        o_ref[...] = out_t.T.astype(o_ref.dtype)

    s_first = scores(0)
    sa_ref[0] = s_first[0]
    sa_ref[1] = s_first[1]
    p_ref[...] = jnp.zeros_like(p_ref)
    init = (jnp.full((1, tq), NEG_INF, F32), jnp.zeros((1, tq), F32), jnp.ones((1, tq), F32)) * 2
    n_pairs = lax.shift_right_logical(qi, 1)
    carry = lax.fori_loop(0, n_pairs, pair_body, init)
    odd = jnp.bitwise_and(qi, 1)

    @pl.when(odd == 0)
    def _():
        finish(sa_ref, carry)

    @pl.when(odd == 1)
    def _():
        finish(sb_ref, stage(qi - 1, sa_ref, sb_ref, carry))


def prompt_attention(proj, cols, ccol, crow, *, nb, lp, head_dim, out_rows):
    mtot = out_rows
    nh = ccol.shape[1]
    pw = 2 * head_dim
    tq = _pick(lp, (384, 256, 128))
    nq = lp // tq
    return pl.pallas_call(
        functools.partial(_attn_kernel, tq=tq, head_dim=head_dim, lp=lp),
        grid=(nb, nh // 2, nq),
        in_specs=[pl.BlockSpec((tq, pw), lambda b, p, i: (b * nq + i, cols["q"] // pw + p)),
                  pl.BlockSpec((lp, pw), lambda b, p, i: (b, cols["k"] // pw + p)),
                  pl.BlockSpec((lp, pw), lambda b, p, i: (b, cols["v"] // pw + p)),
                  pl.BlockSpec((lp, nh), lambda b, p, i: (b, 0)),
                  pl.BlockSpec((None, nh, lp), lambda b, p, i: (b, 0, 0))],
        out_specs=pl.BlockSpec((tq, pw), lambda b, p, i: (b * nq + i, p)),
        out_shape=jax.ShapeDtypeStruct((mtot, nh * head_dim), BF16),
        scratch_shapes=[pltpu.VMEM((lp, 2 * pw), BF16), pltpu.VMEM((2, pw, lp), BF16), pltpu.VMEM((pw, tq), F32),
                        pltpu.VMEM((2, tq, tq), F32), pltpu.VMEM((2, tq, tq), F32), pltpu.VMEM((2 * tq, tq), BF16)],
        compiler_params=_cparams(3),
        name="prompt_attention",
    )(proj, proj, proj, ccol, crow)


def _decode_attn_kernel(pt_ref, q_ref, kn_ref, vn_ref, lfn_ref, *refs, pages):
    k_refs = refs[0:pages]
    v_refs = refs[pages:2 * pages]
    lf_refs = refs[2 * pages:3 * pages]
    o_ref, qb_ref, s_ref, p_ref, a_ref, m_ref, l_ref, carry_ref, acc_ref = refs[3 * pages:]
    s_id = pl.program_id(1)
    nh, hd, psz = k_refs[0].shape

    @pl.when(s_id == 0)
    def _():
        qb_ref[...] = q_ref[...] * hd ** -0.5
        lane = lax.broadcasted_iota(jnp.int32, (1, psz), 1)
        for h in range(nh):
            s_ref[h:h + 1, :] = jnp.sum(kn_ref[h] * qb_ref[h], axis=0, keepdims=True)
            acc_ref[h] = jnp.where(lane == 0, vn_ref[h], 0.0)
        m_ref[...] = s_ref[:, 0:1]
        l_ref[...] = jnp.ones_like(l_ref)
        carry_ref[...] = lfn_ref[:, 0:1]

    r_i = lax.broadcasted_iota(jnp.int32, (psz, psz), 0)
    c_i = lax.broadcasted_iota(jnp.int32, (psz, psz), 1)
    later = (r_i > c_i).astype(BF16)
    for i in range(pages):
        for h in range(nh):
            s_ref[h:h + 1, :] = jnp.sum(k_refs[i][h] * qb_ref[h], axis=0, keepdims=True)
        lf = lf_refs[i][...]
        hi, mid, lo = _split3(lf)
        within = (_dot(hi.astype(BF16), later) + _dot(mid.astype(BF16), later)) + _dot(lo.astype(BF16), later)
        carry = carry_ref[...]
        s = s_ref[...] + (within + carry)
        carry_ref[...] = carry + jnp.sum(lf, axis=1, keepdims=True)
        m_old = m_ref[...]
        m_new = jnp.maximum(m_old, jnp.max(s, axis=1, keepdims=True))
        alpha = jnp.exp(m_old - m_new)
        p = jnp.exp(s - m_new)
        l_ref[...] = alpha * l_ref[...] + jnp.sum(p, axis=1, keepdims=True)
        m_ref[...] = m_new
        p_ref[...] = p
        a_ref[...] = jnp.broadcast_to(alpha, (nh, psz))
        for h in range(nh):
            acc_ref[h] = a_ref[h:h + 1, :] * acc_ref[h] + v_refs[i][h] * p_ref[h:h + 1, :]

    @pl.when(s_id == pl.num_programs(1) - 1)
    def _():
        for h in range(nh):
            o_h = jnp.sum(acc_ref[h], axis=1, keepdims=True) / l_ref[h:h + 1, :]
            o_ref[h] = jnp.broadcast_to(o_h, (hd, psz))


def decode_attention(q, k_new, v_new, logf_new, cache_k, cache_v, cache_logf, page_table, layer):
    nb, n_pages = page_table.shape
    _, _, psz, nh, hd = cache_k.shape
    pages = _pick(n_pages, (8, 4, 2, 1))
    kt = jnp.transpose(cache_k, (0, 1, 3, 4, 2))
    vt = jnp.transpose(cache_v, (0, 1, 3, 4, 2))
    lft = jnp.transpose(cache_logf, (0, 1, 3, 2))
    rep = lambda x: jnp.broadcast_to(x[..., None], x.shape + (psz,))

    def page(i, dims):
        def index(b, s, pt):
            return (pt[b, n_pages - 1 - (s * pages + i)], layer) + (0,) * len(dims)
        return pl.BlockSpec((None, None) + dims, index)

    tok = pl.BlockSpec((None, nh, hd, psz), lambda b, s, pt: (b, 0, 0, 0))
    grid_spec = pltpu.PrefetchScalarGridSpec(
        num_scalar_prefetch=1,
        grid=(nb, n_pages // pages),
        in_specs=([tok, tok, tok, pl.BlockSpec((None, nh, psz), lambda b, s, pt: (b, 0, 0))]
                  + [page(i, (nh, hd, psz)) for i in range(pages)]
                  + [page(i, (nh, hd, psz)) for i in range(pages)]
                  + [page(i, (nh, psz)) for i in range(pages)]),
        out_specs=tok,
        scratch_shapes=[pltpu.VMEM((nh, hd, psz), F32), pltpu.VMEM((nh, psz), F32), pltpu.VMEM((nh, psz), F32),
                        pltpu.VMEM((nh, psz), F32), pltpu.VMEM((nh, 1), F32), pltpu.VMEM((nh, 1), F32),
                        pltpu.VMEM((nh, 1), F32), pltpu.VMEM((nh, hd, psz), F32)],
    )
    out = pl.pallas_call(
        functools.partial(_decode_attn_kernel, pages=pages),
        grid_spec=grid_spec,
        out_shape=jax.ShapeDtypeStruct((nb, nh, hd, psz), F32),
        compiler_params=_cparams(2),
        name="decode_attention",
    )(page_table, rep(q), rep(k_new), rep(v_new), rep(logf_new),
      *([kt] * pages), *([vt] * pages), *([lft] * pages))
    return out[..., 0]


def _final_norm_shift_kernel(a_ref, b_ref, g_ref, o_ref, *, shift):
    x = jnp.concatenate([a_ref[shift:, :], b_ref[0:shift, :]], axis=0)
    ms = jnp.mean(x * x, axis=-1, keepdims=True)
    o_ref[...] = (x * lax.rsqrt(ms + NORM_EPS)) * g_ref[...]


def final_norm_prompt(x, g, *, nb, lp, seq, shift):
    d = x.shape[1]
    tr = SEQ_TILE
    nt = seq // tr
    nl = lp // tr
    return pl.pallas_call(
        functools.partial(_final_norm_shift_kernel, shift=shift),
        grid=(nb, nt),
        in_specs=[pl.BlockSpec((tr, d), lambda b, t: (b * nl + t, 0)),
                  pl.BlockSpec((tr, d), lambda b, t: (b * nl + t + 1, 0)),
                  pl.BlockSpec((1, d), lambda b, t: (0, 0))],
        out_specs=pl.BlockSpec((None, tr, d), lambda b, t: (b, t, 0)),
        out_shape=jax.ShapeDtypeStruct((nb, seq, d), F32),
        compiler_params=_cparams(2),
        name="final_norm_prompt",
    )(x, x, g.reshape(1, d))


def _final_norm_kernel(x_ref, g_ref, o_ref):
    x = x_ref[...]
    ms = jnp.mean(x * x, axis=-1, keepdims=True)
    o_ref[...] = (x * lax.rsqrt(ms + NORM_EPS)) * g_ref[...]


def final_norm(x, g):
    m, d = x.shape
    return pl.pallas_call(
        _final_norm_kernel,
        grid=(1,),
        in_specs=[pl.BlockSpec((m, d), lambda i: (0, 0)), pl.BlockSpec((1, d), lambda i: (0, 0))],
        out_specs=pl.BlockSpec((m, d), lambda i: (0, 0)),
        out_shape=jax.ShapeDtypeStruct((m, d), F32),
        compiler_params=_cparams(1),
        name="final_norm",
    )(x, g.reshape(1, d))


def _pack_kernel(offs_ref, w_ref, ws_ref, o_ref, *, small_tile):
    w = jnp.where(pl.program_id(1) == small_tile, ws_ref[...], w_ref[0])
    o_ref[...] = w.T.astype(o_ref.dtype)


def pack_w_in(w_in_t, dims):
    depth, n_out, d = w_in_t.shape
    pd, di, cd, nh_s, ad, nh_a, dm = dims
    tile = SMALL_W
    src = {}
    o = 0
    for name, width in (("pool", pd), ("z", di), ("xbc", cd), ("dt", nh_s), ("q", ad), ("k", ad), ("v", ad),
                        ("f", nh_a), ("gate", n_out - (pd + di + cd + nh_s + 3 * ad + nh_a))):
        src[name] = (o, width)
        o += width
    assert DT_LANE == 0 and nh_s <= F_LANE
    rows = lambda name: w_in_t[:, src[name][0]:src[name][0] + src[name][1]]
    zeros = lambda w: jnp.zeros((depth, w, d), w_in_t.dtype)
    small = jnp.concatenate([rows("dt"), zeros(F_LANE - nh_s), rows("f"), zeros(tile - F_LANE - nh_a)], axis=1)
    cols, offs = {}, []
    for name in ("xbc", "small", "pool", "z", "q", "k", "v", "gate"):
        cols[name] = len(offs) * tile
        if name == "small":
            small_tile = len(offs)
            offs.append(0)
        else:
            start, width = src[name]
            assert width % tile == 0 and start % SUBLANES == 0 and tile % SUBLANES == 0
            offs += [(start + t * tile) // SUBLANES for t in range(width // tile)]
    grid_spec = pltpu.PrefetchScalarGridSpec(
        num_scalar_prefetch=1,
        grid=(depth, len(offs)),
        in_specs=[pl.BlockSpec((pl.Element(1), pl.Element(tile), pl.Element(d)), lambda l, j, o: (l, o[j] * SUBLANES, 0)),
                  pl.BlockSpec((None, tile, d), lambda l, j, o: (l, 0, 0))],
        out_specs=pl.BlockSpec((None, d, tile), lambda l, j, o: (l, 0, j)),
    )
    packed = pl.pallas_call(
        functools.partial(_pack_kernel, small_tile=small_tile),
        grid_spec=grid_spec,
        out_shape=jax.ShapeDtypeStruct((depth, d, len(offs) * tile), BF16),
        compiler_params=_cparams(2),
        name="pack_w_in",
    )(jnp.asarray(offs, jnp.int32), w_in_t, small)
    return packed, cols


def kernel(x_prompt, x_sample, cache_k, cache_v, cache_logf, page_table, state_ssm, state_conv, state_pool,
           meta_tokens, norm_ffn1, ffn1_w_gu, ffn1_w_down, norm_mix, w_in, pool_w, pool_scale, conv_w, conv_b,
           dt_bias, a_log, d_skip, ssm_norm, forget_bias, w_br_pool, w_br_ssm, w_br_attn, w_out, norm_ffn2,
           ffn2_w_gu, ffn2_w_down, final_norm_w):
    nb, seq, dm = x_prompt.shape
    ns = x_sample.shape[0]
    depth = w_in.shape[0]
    n_meta = meta_tokens.shape[0]
    nh_a, hd_a = cache_k.shape[3], cache_k.shape[4]
    ad = nh_a * hd_a
    psz = cache_k.shape[2]
    past_len = page_table.shape[1] * psz
    nh_s, hd_s, n_state = state_ssm.shape[2], state_ssm.shape[3], state_ssm.shape[4]
    di = nh_s * hd_s
    cd = state_conv.shape[3]
    conv_hist = state_conv.shape[2]
    pd = state_pool.shape[3]
    pool_hist = state_pool.shape[2]
    l = seq + n_meta
    lp = -(-l // SEQ_TILE) * SEQ_TILE
    assert x_sample.shape[1] == 1 and seq % SEQ_TILE == 0 and n_meta % 8 == 0 and n_meta < SEQ_TILE
    assert 2 * hd_a == LANES and 2 * hd_s == LANES and n_state == LANES
    assert pool_hist < POOL_HIST_ROWS and conv_hist < CONV_HIST_ROWS

    mp = nb * lp
    m_all = mp + ns * DEC_ROWS
    x = jnp.concatenate(
        [jnp.concatenate([jnp.broadcast_to(meta_tokens[None], (nb, n_meta, dm)), x_prompt,
                          jnp.zeros((nb, lp - l, dm), F32)], axis=1).reshape(mp, dm),
         jnp.pad(x_sample, ((0, 0), (0, DEC_ROWS - 1), (0, 0))).reshape(ns * DEC_ROWS, dm)], axis=0)

    zero_pool_hist = jnp.zeros((nb, POOL_HIST_ROWS, pd), F32)
    zero_conv_hist = jnp.zeros((nb, CONV_HIST_ROWS, cd), F32)
    zero_state = jnp.zeros((nb, di, n_state), F32)
    pool_w_b = pool_w.astype(BF16)
    w_in_p, cols = pack_w_in(jnp.swapaxes(w_in, 1, 2), (pd, di, cd, nh_s, ad, nh_a, dm))

    outs = {k: [] for k in ("kp", "vp", "lfp", "hp", "cp", "pp", "ks", "vs", "lfs", "hs", "cs", "ps")}
    for i in range(depth):
        pw = pool_w_b[i]

        x = matmul_residual(norm_swiglu(x, norm_ffn1[i], ffn1_w_gu, i), ffn1_w_down, i, x, 0.5)
        proj = norm_matmul(x, norm_mix[i], w_in_p, i)

        pool_p, lf_p, ccol, crow = pool_and_forget(
            proj, cols, zero_pool_hist, pw, pool_scale[i], forget_bias[i],
            nb=nb, nt=lp // SEQ_TILE, rows=SEQ_TILE, row0=0, pos0=0, n_valid=l, with_cum=True, out_rows=m_all)
        ssm_p, h_p = ssd_mixer(proj, cols, zero_conv_hist, zero_state, conv_w[i], conv_b[i], dt_bias[i], a_log[i],
                               d_skip[i], ssm_norm[i], nb=nb, nt=lp // SEQ_TILE, rows=SEQ_TILE, row0=0, n_valid=l,
                               out_rows=m_all)
        att_p = prompt_attention(proj, cols, ccol, crow, nb=nb, lp=lp, head_dim=hd_a, out_rows=m_all)

        hist_p = jnp.pad(state_pool[:, i], ((0, 0), (POOL_HIST_ROWS - pool_hist, 0), (0, 0)))
        hist_c = jnp.pad(state_conv[:, i], ((0, 0), (CONV_HIST_ROWS - conv_hist, 0), (0, 0)))
        pool_s, lf_s = pool_and_forget(
            proj, cols, hist_p, pw, pool_scale[i], forget_bias[i],
            nb=ns, nt=1, rows=DEC_ROWS, row0=mp, pos0=past_len, n_valid=1, with_cum=False)
        ssm_s, h_s = ssd_mixer(proj, cols, hist_c, state_ssm[:, i].reshape(ns, di, n_state), conv_w[i], conv_b[i],
                               dt_bias[i], a_log[i], d_skip[i], ssm_norm[i],
                               nb=ns, nt=1, rows=DEC_ROWS, row0=mp, n_valid=1)
        ps0 = proj[mp:].reshape(ns, DEC_ROWS, -1)[:, 0:1]
        heads = lambda name: ps0[:, 0, cols[name]:cols[name] + ad].reshape(ns, nh_a, hd_a)
        att_s = decode_attention(heads("q"), heads("k"), heads("v"), lf_s.reshape(ns, DEC_ROWS, nh_a)[:, 0],
                                 cache_k, cache_v, cache_logf, page_table, i)
        att_s = jnp.pad(att_s.reshape(ns, 1, ad), ((0, 0), (0, DEC_ROWS - 1), (0, 0)))
        att_s = att_s.reshape(ns * DEC_ROWS, ad).astype(BF16)

        place = lambda full, part: lax.dynamic_update_slice(full, part, (mp, 0))
        merged = gated_merge(place(pool_p, pool_s), place(ssm_p, ssm_s), place(att_p, att_s), proj, cols["gate"],
                             w_br_pool, w_br_ssm, w_br_attn, i)
        x = matmul_residual(merged, w_out, i, x, 1.0)
        x = matmul_residual(norm_swiglu(x, norm_ffn2[i], ffn2_w_gu, i), ffn2_w_down, i, x, 0.5)

        def prompt_rows(first, last, name, width):
            return jnp.stack([proj[b * lp + first:b * lp + last, cols[name]:cols[name] + width] for b in range(nb)])

        outs["kp"].append(prompt_rows(0, l, "k", ad).reshape(nb, l, nh_a, hd_a))
        outs["vp"].append(prompt_rows(0, l, "v", ad).reshape(nb, l, nh_a, hd_a))
        outs["lfp"].append(lf_p.reshape(nb, lp, nh_a)[:, :l])
        outs["hp"].append(h_p.reshape(nb, nh_s, hd_s, n_state))
        outs["cp"].append(prompt_rows(l - conv_hist, l, "xbc", cd))
        outs["pp"].append(prompt_rows(l - pool_hist, l, "pool", pd))
        outs["ks"].append(ps0[:, :, cols["k"]:cols["k"] + ad].reshape(ns, 1, nh_a, hd_a))
        outs["vs"].append(ps0[:, :, cols["v"]:cols["v"] + ad].reshape(ns, 1, nh_a, hd_a))
        outs["lfs"].append(lf_s.reshape(ns, DEC_ROWS, nh_a)[:, 0:1])
        outs["hs"].append(h_s.reshape(ns, nh_s, hd_s, n_state))
        outs["cs"].append(jnp.concatenate([state_conv[:, i, 1:], ps0[:, :, cols["xbc"]:cols["xbc"] + cd]], axis=1))
        outs["ps"].append(jnp.concatenate([state_pool[:, i, 1:], ps0[:, :, cols["pool"]:cols["pool"] + pd]], axis=1))

    y_prompt = final_norm_prompt(x, final_norm_w, nb=nb, lp=lp, seq=seq, shift=n_meta)
    y_sample = final_norm(x[mp:], final_norm_w).reshape(ns, DEC_ROWS, dm)[:, 0:1]
    st = lambda k: jnp.stack(outs[k], axis=1)
    return (y_prompt, y_sample, st("kp"), st("vp"), st("lfp"), st("ks"), st("vs"), st("lfs"),
            st("hp"), st("hs"), st("cp"), st("cs"), st("pp"), st("ps"))
```

```python
import functools
import math

import jax
import jax.numpy as jnp
from jax import lax
from jax.experimental import pallas as pl
from jax.experimental.pallas import tpu as pltpu

F32 = jnp.float32
BF16 = jnp.bfloat16

NORM_EPS = 1e-6
NEG_INF = -1e30
LOG2E = 1.4426950408889634
POOL_WINDOWS = (2, 4, 8, 16)
POOL_HIST_ROWS = 16
CONV_TAPS = 4
CONV_HIST_ROWS = 8
SEQ_TILE = 128
DEC_ROWS = 16
LANES = 128
SUBLANES = 8
SMALL_W = 512
DT_LANE = 0
F_LANE = 128
VMEM_LIMIT_BYTES = 56 * 1024 * 1024
ROW_TILES = (1072, 1056, 768, 704, 528, 512, 384, 256, 128)


def _cparams(n_axes):
    return pltpu.CompilerParams(dimension_semantics=("arbitrary",) * n_axes,
                                vmem_limit_bytes=VMEM_LIMIT_BYTES)


def _pick(n, cands):
    for c in cands:
        if n % c == 0:
            return c
    return n


def _sigmoid(x):
    return 1.0 / (1.0 + jnp.exp(-x))


def _softplus(x):
    return jnp.maximum(x, 0.0) + jnp.log(1.0 + jnp.exp(-jnp.abs(x)))


def _dot(a, b):
    return jnp.dot(a, b, preferred_element_type=F32)


def _dot_nt(a, b):
    return lax.dot_general(a, b, (((1,), (1,)), ((), ())), preferred_element_type=F32)


def _split3(x):
    hi = x.astype(BF16).astype(F32)
    r = x - hi
    mid = r.astype(BF16).astype(F32)
    lo = (r - mid).astype(BF16).astype(F32)
    return hi, mid, lo


def _select_dot(sel, x):
    s = sel.astype(BF16)
    hi, mid, lo = _split3(x)
    return (_dot(s, hi.astype(BF16)) + _dot(s, mid.astype(BF16))) + _dot(s, lo.astype(BF16))


def _dot_select(x, sel):
    s = sel.astype(BF16)
    hi, mid, lo = _split3(x)
    return (_dot(hi.astype(BF16), s) + _dot(mid.astype(BF16), s)) + _dot(lo.astype(BF16), s)


def _rms_to_scratch(x_ref, g_ref, xn_ref):
    x = x_ref[...]
    ms = jnp.mean(x * x, axis=-1, keepdims=True)
    xn_ref[...] = ((x * lax.rsqrt(ms + NORM_EPS)) * g_ref[...]).astype(BF16)


def _norm_swiglu_kernel(x_ref, g_ref, wg_ref, wu_ref, o_ref, xn_ref):
    @pl.when(pl.program_id(1) == 0)
    def _():
        _rms_to_scratch(x_ref, g_ref, xn_ref)

    xn = xn_ref[...]
    gate = _dot(xn, wg_ref[...].astype(BF16))
    up = _dot(xn, wu_ref[...].astype(BF16))
    o_ref[...] = ((gate * _sigmoid(gate)) * up).astype(o_ref.dtype)


def norm_swiglu(x, g, w_gu, layer):
    m, d = x.shape
    f = w_gu.shape[2] // 2
    tm = _pick(m, ROW_TILES)
    tn = _pick(f, (512, 256, 128))
    nj = f // tn
    return pl.pallas_call(
        _norm_swiglu_kernel,
        grid=(m // tm, nj),
        in_specs=[pl.BlockSpec((tm, d), lambda i, j: (i, 0)),
                  pl.BlockSpec((1, d), lambda i, j: (0, 0)),
                  pl.BlockSpec((None, d, tn), lambda i, j: (layer, 0, j)),
                  pl.BlockSpec((None, d, tn), lambda i, j: (layer, 0, j + nj))],
        out_specs=pl.BlockSpec((tm, tn), lambda i, j: (i, j)),
        out_shape=jax.ShapeDtypeStruct((m, f), BF16),
        scratch_shapes=[pltpu.VMEM((tm, d), BF16)],
        compiler_params=_cparams(2),
        name="norm_swiglu",
    )(x, g.reshape(1, d), w_gu, w_gu)


def _matmul_residual_kernel(h_ref, w_ref, x_ref, o_ref, *, scale):
    o_ref[...] = x_ref[...] + scale * _dot(h_ref[...], w_ref[...].astype(BF16))


def matmul_residual(h, w, layer, x, scale):
    m, k = h.shape
    n = w.shape[2]
    tm = _pick(m, ROW_TILES)
    tn = _pick(n, (512, 256, 128) if k * 512 * w.dtype.itemsize <= 8 * 1024 * 1024 else (256, 128))
    return pl.pallas_call(
        functools.partial(_matmul_residual_kernel, scale=scale),
        grid=(m // tm, n // tn),
        in_specs=[pl.BlockSpec((tm, k), lambda i, j: (i, 0)),
                  pl.BlockSpec((None, k, tn), lambda i, j: (layer, 0, j)),
                  pl.BlockSpec((tm, tn), lambda i, j: (i, j))],
        out_specs=pl.BlockSpec((tm, tn), lambda i, j: (i, j)),
        out_shape=jax.ShapeDtypeStruct((m, n), F32),
        compiler_params=_cparams(2),
        name="matmul_residual",
    )(h, w, x)


def _norm_matmul_kernel(x_ref, g_ref, w_ref, o_ref, xn_ref):
    @pl.when(pl.program_id(1) == 0)
    def _():
        _rms_to_scratch(x_ref, g_ref, xn_ref)

    o_ref[...] = _dot(xn_ref[...], w_ref[...])


def norm_matmul(x, g, w, layer):
    m, d = x.shape
    n = w.shape[2]
    tm = _pick(m, ROW_TILES)
    tn = _pick(n, (1024, 512, 256, 128))
    return pl.pallas_call(
        _norm_matmul_kernel,
        grid=(m // tm, n // tn),
        in_specs=[pl.BlockSpec((tm, d), lambda i, j: (i, 0)),
                  pl.BlockSpec((1, d), lambda i, j: (0, 0)),
                  pl.BlockSpec((None, d, tn), lambda i, j: (layer, 0, j))],
        out_specs=pl.BlockSpec((tm, tn), lambda i, j: (i, j)),
        out_shape=jax.ShapeDtypeStruct((m, n), F32),
        scratch_shapes=[pltpu.VMEM((tm, d), BF16)],
        compiler_params=_cparams(2),
        name="norm_matmul",
    )(x, g.reshape(1, d), w)


def _merge_kernel(p_ref, s_ref, a_ref, g0_ref, g1_ref, g2_ref, wp_ref, ws_ref, wa_ref, o_ref):
    merged = (_sigmoid(g0_ref[...]) * _dot(p_ref[...], wp_ref[...].astype(BF16))
              + _sigmoid(g1_ref[...]) * _dot(s_ref[...], ws_ref[...].astype(BF16))
              + _sigmoid(g2_ref[...]) * _dot(a_ref[...], wa_ref[...].astype(BF16)))
    o_ref[...] = merged.astype(o_ref.dtype)


def gated_merge(pool_out, ssm_out, att, proj, gate_col, w_pool, w_ssm, w_attn, layer):
    m, kb = pool_out.shape
    d = w_pool.shape[2]
    tm = _pick(m, ROW_TILES)
    tn = _pick(d, (512, 256, 128))
    g_blk = gate_col // tn
    nj = d // tn
    branch = pl.BlockSpec((tm, kb), lambda i, j: (i, 0))
    weight = pl.BlockSpec((None, kb, tn), lambda i, j: (layer, 0, j))

    def gate(r):
        return pl.BlockSpec((tm, tn), lambda i, j: (i, g_blk + r * nj + j))

    return pl.pallas_call(
        _merge_kernel,
        grid=(m // tm, nj),
        in_specs=[branch, branch, branch, gate(0), gate(1), gate(2), weight, weight, weight],
        out_specs=pl.BlockSpec((tm, tn), lambda i, j: (i, j)),
        out_shape=jax.ShapeDtypeStruct((m, d), BF16),
        compiler_params=_cparams(2),
        name="gated_merge",
    )(pool_out, ssm_out, att, proj, proj, proj, w_pool, w_ssm, w_attn)


def _prep_kernel(xp_ref, sm_ref, hist_ref, pw_ref, ps_ref, fb_ref, *refs, rows, pos0, n_valid, with_cum):
    if with_cum:
        pool_ref, logf_ref, ccol_ref, crow_ref, full_ref, carry_ref = refs
    else:
        pool_ref, logf_ref, full_ref = refs
    t = pl.program_id(1)
    hr = POOL_HIST_ROWS

    @pl.when(t == 0)
    def _():
        full_ref[0:hr, :] = hist_ref[...]
        if with_cum:
            carry_ref[...] = jnp.zeros_like(carry_ref)

    x = xp_ref[...]
    full_ref[hr:hr + rows, :] = x
    row = t * rows + lax.broadcasted_iota(jnp.int32, (rows, 1), 0)
    gd = x.shape[1] // len(POOL_WINDOWS)
    for g, w in enumerate(POOL_WINDOWS):
        cols = slice(g * gd, (g + 1) * gd)
        s = full_ref[hr:hr + rows, cols]
        for k in range(1, w):
            s = s + full_ref[hr - k:hr - k + rows, cols]
        cnt = jnp.minimum(w, pos0 + row + 1).astype(F32)
        pooled = s / cnt - x[:, cols]
        mixed = _dot(pooled.astype(BF16), pw_ref[g])
        pool_ref[:, cols] = (mixed * ps_ref[:, cols]).astype(pool_ref.dtype)
    full_ref[0:hr, :] = full_ref[rows:rows + hr, :]

    z = sm_ref[:, F_LANE:F_LANE + logf_ref.shape[1]] + fb_ref[...]
    logf = -_softplus(-z)
    logf_ref[...] = logf
    if with_cum:
        lf = jnp.where(row < n_valid, logf, 0.0)
        r_i = lax.broadcasted_iota(jnp.int32, (rows, rows), 0)
        c_i = lax.broadcasted_iota(jnp.int32, (rows, rows), 1)
        cs = _select_dot(r_i >= c_i, lf) + carry_ref[...]
        ccol_ref[...] = cs
        carry_ref[...] = cs[rows - 1:rows, :]
        nh = cs.shape[1]
        wide = jnp.concatenate([cs, jnp.zeros((rows, LANES - nh), F32)], axis=1)
        crow_ref[...] = wide.T[0:nh, :]


def pool_and_forget(proj, cols, hist, pool_w, pool_scale, forget_bias, *, nb, nt, rows, row0, pos0, n_valid,
                    with_cum, out_rows=None):
    mtot = nb * nt * rows
    blk0 = row0 // rows
    pd = pool_w.shape[0] * pool_w.shape[1]
    nh = forget_bias.shape[0]
    xp_blk = cols["pool"] // pd
    sm_blk = cols["small"] // SMALL_W
    in_specs = [pl.BlockSpec((rows, pd), lambda b, t: (blk0 + b * nt + t, xp_blk)),
                pl.BlockSpec((rows, SMALL_W), lambda b, t: (blk0 + b * nt + t, sm_blk)),
                pl.BlockSpec((None, POOL_HIST_ROWS, pd), lambda b, t: (b, 0, 0)),
                pl.BlockSpec(pool_w.shape, lambda b, t: (0, 0, 0)),
                pl.BlockSpec((1, pd), lambda b, t: (0, 0)),
                pl.BlockSpec((1, nh), lambda b, t: (0, 0))]
    out_specs = [pl.BlockSpec((rows, pd), lambda b, t: (b * nt + t, 0)),
                 pl.BlockSpec((rows, nh), lambda b, t: (b * nt + t, 0))]
    out_shape = [jax.ShapeDtypeStruct((out_rows or mtot, pd), BF16), jax.ShapeDtypeStruct((mtot, nh), F32)]
    scratch = [pltpu.VMEM((POOL_HIST_ROWS + rows, pd), F32)]
    if with_cum:
        out_specs += [pl.BlockSpec((rows, nh), lambda b, t: (b * nt + t, 0)),
                      pl.BlockSpec((None, nh, rows), lambda b, t: (b, 0, t))]
        out_shape += [jax.ShapeDtypeStruct((mtot, nh), F32), jax.ShapeDtypeStruct((nb, nh, nt * rows), F32)]
        scratch += [pltpu.VMEM((1, nh), F32)]
    return pl.pallas_call(
        functools.partial(_prep_kernel, rows=rows, pos0=pos0, n_valid=n_valid, with_cum=with_cum),
        grid=(nb, nt),
        in_specs=in_specs, out_specs=out_specs, out_shape=out_shape, scratch_shapes=scratch,
        compiler_params=_cparams(2),
        name="pool_and_forget",
    )(proj, proj, hist, pool_w, pool_scale.reshape(1, pd), forget_bias.reshape(1, nh))


def _ssd_kernel(xz_ref, z_ref, hist_ref, h0_ref, cw_ref, cb_ref, dtb_ref, alog_ref, dskip_ref, nw_ref,
                ex_ref, ext_ref, o_ref, h_ref, full_ref, y_ref, *, rows, n_valid, d_inner, n_state, head_dim):
    t = SEQ_TILE
    c = pl.program_id(1)
    hr = CONV_HIST_ROWS
    cdim = full_ref.shape[1]
    nheads = d_inner // head_dim
    ngroups = (cdim - d_inner) // (2 * n_state)
    pair_w = 2 * head_dim
    pairs_per_group = d_inner // pair_w // ngroups

    @pl.when(c == 0)
    def _():
        full_ref[0:hr, :] = hist_ref[...]
        h_ref[...] = h0_ref[...]

    full_ref[hr:hr + rows, :] = xz_ref[:, 0:cdim]
    if rows < t:
        full_ref[hr + rows:hr + t, :] = jnp.zeros((t - rows, cdim), F32)
    conv = cb_ref[...]
    for k in range(CONV_TAPS):
        off = hr - (CONV_TAPS - 1) + k
        conv = conv + full_ref[off:off + t, :] * cw_ref[k:k + 1, :]
    act = conv * _sigmoid(conv)
    full_ref[0:hr, :] = full_ref[t:t + hr, :]

    dt_raw = xz_ref[:, cdim + DT_LANE:cdim + DT_LANE + nheads]
    if rows < t:
        dt_raw = jnp.concatenate([dt_raw, jnp.zeros((t - rows, nheads), F32)], axis=0)
    row = c * t + lax.broadcasted_iota(jnp.int32, (t, 1), 0)
    dt = jnp.where(row < n_valid, _softplus(dt_raw + dtb_ref[...]), 0.0)
    a = -jnp.exp(alog_ref[...])
    r_i = lax.broadcasted_iota(jnp.int32, (t, t), 0)
    c_i = lax.broadcasted_iota(jnp.int32, (t, t), 1)
    causal = r_i >= c_i
    acum = _select_dot(causal, dt * a)
    lane_pad = jnp.zeros((t, LANES - nheads), F32)
    acum_row = jnp.concatenate([acum, lane_pad], axis=1).T[0:nheads, :]
    dt_row = jnp.concatenate([dt, lane_pad], axis=1).T[0:nheads, :]
    a_last = acum[t - 1:t, :]
    decay_in = jnp.exp(acum)
    to_end = jnp.exp(a_last - acum) * dt
    wide = _dot_select(jnp.concatenate([decay_in, to_end], axis=0), ex_ref[...])
    decay_in_x = wide[0:t, :]
    to_end_x = wide[t:2 * t, :]
    chunk_decay = jnp.broadcast_to(jnp.exp(acum_row[:, t - 1:t]), (nheads, n_state))
    chunk_decay_x = _select_dot(ext_ref[...], chunk_decay)

    xs = act[:, 0:d_inner]
    xs_b = xs.astype(BF16)
    lane = lax.broadcasted_iota(jnp.int32, (t, pair_w), 1)
    first = lane < head_dim
    for g in range(ngroups):
        bm = act[:, d_inner + g * n_state:d_inner + (g + 1) * n_state].astype(BF16)
        cm = act[:, d_inner + (ngroups + g) * n_state:d_inner + (ngroups + g + 1) * n_state].astype(BF16)
        cb = _dot_nt(cm, bm)
        for j in range(pairs_per_group):
            p = g * pairs_per_group + j
            cols = slice(p * pair_w, (p + 1) * pair_w)
            wts = []
            for h in (2 * p, 2 * p + 1):
                seg = acum[:, h:h + 1] - acum_row[h:h + 1, :]
                decay = jnp.exp(jnp.where(causal, seg, -jnp.inf))
                wts.append((cb * decay) * dt_row[h:h + 1, :])
            lhs = jnp.concatenate(wts, axis=1).astype(BF16)
            xp = xs_b[:, cols]
            zero = jnp.zeros_like(xp)
            rhs = jnp.concatenate([jnp.where(first, xp, zero), jnp.where(first, zero, xp)], axis=0)
            y = _dot(lhs, rhs)
            hp = h_ref[p * pair_w:(p + 1) * pair_w, :]
            y = y + _dot_nt(cm, hp.astype(BF16)) * decay_in_x[:, cols]
            y_ref[:, cols] = y + dskip_ref[:, cols] * xs[:, cols]
            upd = (xs[:, cols] * to_end_x[:, cols]).T.astype(BF16)
            h_ref[p * pair_w:(p + 1) * pair_w, :] = chunk_decay_x[p * pair_w:(p + 1) * pair_w, :] * hp + _dot(upd, bm)

    zz = z_ref[...]
    if rows < t:
        zz = jnp.concatenate([zz, jnp.zeros((t - rows, d_inner), F32)], axis=0)
    gated = y_ref[...] * (zz * _sigmoid(zz))
    ms = jnp.mean(gated * gated, axis=-1, keepdims=True)
    out = (gated * lax.rsqrt(ms + NORM_EPS)) * nw_ref[...]
    o_ref[...] = out[0:rows, :].astype(o_ref.dtype)


def ssd_mixer(proj, cols, hist, h0, conv_w, conv_b, dt_bias, a_log, d_skip, ssm_norm, *, nb, nt, rows, row0,
              n_valid, out_rows=None):
    mtot = nb * nt * rows
    blk0 = row0 // rows
    cdim = conv_w.shape[1]
    nheads = dt_bias.shape[0]
    d_inner, n_state = h0.shape[1], h0.shape[2]
    head_dim = d_inner // nheads
    xz_w = cdim + SMALL_W
    expand = jnp.repeat(jnp.eye(nheads, dtype=F32), head_dim, axis=1)
    const2 = lambda b, t: (0, 0)
    return pl.pallas_call(
        functools.partial(_ssd_kernel, rows=rows, n_valid=n_valid, d_inner=d_inner, n_state=n_state,
                          head_dim=head_dim),
        grid=(nb, nt),
        in_specs=[pl.BlockSpec((rows, xz_w), lambda b, t: (blk0 + b * nt + t, cols["xbc"] // xz_w)),
                  pl.BlockSpec((rows, d_inner), lambda b, t: (blk0 + b * nt + t, cols["z"] // d_inner)),
                  pl.BlockSpec((None, CONV_HIST_ROWS, cdim), lambda b, t: (b, 0, 0)),
                  pl.BlockSpec((None, d_inner, n_state), lambda b, t: (b, 0, 0)),
                  pl.BlockSpec((CONV_TAPS, cdim), const2),
                  pl.BlockSpec((1, cdim), const2),
                  pl.BlockSpec((1, nheads), const2),
                  pl.BlockSpec((1, nheads), const2),
                  pl.BlockSpec((1, d_inner), const2),
                  pl.BlockSpec((1, d_inner), const2),
                  pl.BlockSpec((nheads, d_inner), const2),
                  pl.BlockSpec((d_inner, nheads), const2)],
        out_specs=[pl.BlockSpec((rows, d_inner), lambda b, t: (b * nt + t, 0)),
                   pl.BlockSpec((None, d_inner, n_state), lambda b, t: (b, 0, 0))],
        out_shape=[jax.ShapeDtypeStruct((out_rows or mtot, d_inner), BF16),
                   jax.ShapeDtypeStruct((nb, d_inner, n_state), F32)],
        scratch_shapes=[pltpu.VMEM((CONV_HIST_ROWS + SEQ_TILE, cdim), F32),
                        pltpu.VMEM((SEQ_TILE, d_inner), F32)],
        compiler_params=_cparams(2),
        name="ssd_mixer",
    )(proj, proj, hist, h0, conv_w, conv_b.reshape(1, cdim), dt_bias.reshape(1, nheads),
      a_log.reshape(1, nheads), jnp.repeat(d_skip, head_dim).reshape(1, d_inner),
      ssm_norm.reshape(1, d_inner), expand, expand.T)


def _attn_kernel(q_ref, k_ref, v_ref, ccol_ref, crow_ref, o_ref, kaug_ref, vt_ref, acc_ref, sa_ref, sb_ref, p_ref, *, tq,
                 head_dim, lp):
    pair = pl.program_id(1)
    qi = pl.program_id(2)
    nh = ccol_ref.shape[1]
    pw = 2 * head_dim
    tk = tq
    n_parts = 3
    one_lane = 2 * n_parts

    @pl.when(qi == 0)
    def _():
        hid = lax.broadcasted_iota(jnp.int32, (nh, pw), 0)
        lid = lax.broadcasted_iota(jnp.int32, (nh, pw), 1)
        sel = jnp.where((hid == 2 * pair) & (lid < n_parts), 1.0,
                        jnp.where((hid == 2 * pair + 1) & (lid >= n_parts) & (lid < one_lane), 1.0, 0.0))
        lane = lax.broadcasted_iota(jnp.int32, (SEQ_TILE, pw), 1)
        part = lane - jnp.where(lane >= n_parts, n_parts, 0)

        def body(i, carry):
            r = pl.multiple_of(i * SEQ_TILE, SEQ_TILE)
            c2 = _dot_select(ccol_ref[pl.ds(r, SEQ_TILE), :], sel) * (-LOG2E)
            hi, mid, lo = _split3(c2)
            bias = jnp.where(part == 0, hi, jnp.where(part == 1, mid, lo))
            bias = jnp.where(lane < one_lane, bias, jnp.where(lane < one_lane + n_parts, 1.0, 0.0))
            kaug_ref[pl.ds(r, SEQ_TILE), 0:pw] = k_ref[pl.ds(r, SEQ_TILE), :].astype(BF16)
            kaug_ref[pl.ds(r, SEQ_TILE), pw:2 * pw] = bias.astype(BF16)
            vt = v_ref[pl.ds(r, SEQ_TILE), :].T
            d_row = lax.broadcasted_iota(jnp.int32, vt.shape, 0)
            vt_ref[0, :, pl.ds(r, SEQ_TILE)] = jnp.where(d_row < head_dim, vt, 0.0).astype(BF16)
            vt_ref[1, :, pl.ds(r, SEQ_TILE)] = jnp.where(d_row < head_dim, 0.0, vt).astype(BF16)
            return carry

        lax.fori_loop(0, lp // SEQ_TILE, body, 0)

    qs = pl.multiple_of(qi * tq, tq)
    qt = (q_ref[...] * (head_dim ** -0.5 * LOG2E)).T
    row = lax.broadcasted_iota(jnp.int32, (pw, tq), 0)
    q_aug = []
    for e in range(2):
        head_rows = (row < head_dim) if e == 0 else (row >= head_dim)
        c_q = crow_ref[pl.ds(2 * pair + e, 1), pl.ds(qs, tq)] * LOG2E
        hi, mid, lo = _split3(c_q)
        ones_rows = (row >= n_parts * e) & (row < n_parts * (e + 1))
        qb = jnp.where(ones_rows, 1.0,
                       jnp.where(row == one_lane, hi, jnp.where(row == one_lane + 1, mid,
                                                                jnp.where(row == one_lane + 2, lo, 0.0))))
        q_aug.append(jnp.concatenate([jnp.where(head_rows, qt, 0.0), qb], axis=0).astype(BF16))

    acc_ref[...] = jnp.zeros_like(acc_ref)
    q_pos = qs + lax.broadcasted_iota(jnp.int32, (1, tq), 1)

    def scores(kt):
        ks = pl.multiple_of(kt * tk, tk)
        ka = kaug_ref[pl.ds(ks, tk), :]
        return [_dot(ka, q_aug[e]) for e in range(2)]

    def softmax(kt, s_ref, carry, masked):
        ks = pl.multiple_of(kt * tk, tk)
        new = []
        for e in range(2):
            m_old, l_old = carry[3 * e], carry[3 * e + 1]
            s = s_ref[e]
            if masked:
                k_pos = ks + lax.broadcasted_iota(jnp.int32, (tk, 1), 0)
                s = jnp.where(k_pos <= q_pos, s, NEG_INF)
            m_new = jnp.maximum(m_old, jnp.max(s, axis=0, keepdims=True))
            alpha = jnp.exp2(m_old - m_new)
            p = jnp.exp2(s - m_new)
            p_ref[e * tk:(e + 1) * tk, :] = p.astype(BF16)
            new += [m_new, alpha * l_old + jnp.sum(p, axis=0, keepdims=True), alpha]
        return tuple(new)

    def accumulate(kt, carry):
        ks = pl.multiple_of(kt * tk, tk)
        vt = jnp.concatenate([vt_ref[0, :, pl.ds(ks, tk)], vt_ref[1, :, pl.ds(ks, tk)]], axis=1)
        alpha = jnp.where(row < head_dim, carry[2], carry[5])
        acc_ref[...] = alpha * acc_ref[...] + _dot(vt, p_ref[...])

    def stage(kt, cur_ref, nxt_ref, carry):
        s_next = scores(kt + 1)
        accumulate(jnp.maximum(kt - 1, 0), carry)
        carry = softmax(kt, cur_ref, carry, False)
        nxt_ref[0] = s_next[0]
        nxt_ref[1] = s_next[1]
        return carry

    def pair_body(j, carry):
        carry = stage(2 * j, sa_ref, sb_ref, carry)
        return stage(2 * j + 1, sb_ref, sa_ref, carry)

    def finish(diag_ref, carry):
        accumulate(jnp.maximum(qi - 1, 0), carry)
        carry = softmax(qi, diag_ref, carry, True)
        accumulate(qi, carry)
        out_t = acc_ref[...] / jnp.where(row < head_dim, carry[1], carry[4])
        o_ref[...] = out_t.T.astype(o_ref.dtype)

    s_first = scores(0)
    sa_ref[0] = s_first[0]
    sa_ref[1] = s_first[1]
    p_ref[...] = jnp.zeros_like(p_ref)
    init = (jnp.full((1, tq), NEG_INF, F32), jnp.zeros((1, tq), F32), jnp.ones((1, tq), F32)) * 2
    n_pairs = lax.shift_right_logical(qi, 1)
    carry = lax.fori_loop(0, n_pairs, pair_body, init)
    odd = jnp.bitwise_and(qi, 1)

    @pl.when(odd == 0)
    def _():
        finish(sa_ref, carry)

    @pl.when(odd == 1)
    def _():
        finish(sb_ref, stage(qi - 1, sa_ref, sb_ref, carry))


def prompt_attention(proj, cols, ccol, crow, *, nb, lp, head_dim, out_rows):
    mtot = out_rows
    nh = ccol.shape[1]
    pw = 2 * head_dim
    tq = _pick(lp, (384, 256, 128))
    nq = lp // tq
    return pl.pallas_call(
        functools.partial(_attn_kernel, tq=tq, head_dim=head_dim, lp=lp),
        grid=(nb, nh // 2, nq),
        in_specs=[pl.BlockSpec((tq, pw), lambda b, p, i: (b * nq + i, cols["q"] // pw + p)),
                  pl.BlockSpec((lp, pw), lambda b, p, i: (b, cols["k"] // pw + p)),
                  pl.BlockSpec((lp, pw), lambda b, p, i: (b, cols["v"] // pw + p)),
                  pl.BlockSpec((lp, nh), lambda b, p, i: (b, 0)),
                  pl.BlockSpec((None, nh, lp), lambda b, p, i: (b, 0, 0))],
        out_specs=pl.BlockSpec((tq, pw), lambda b, p, i: (b * nq + i, p)),
        out_shape=jax.ShapeDtypeStruct((mtot, nh * head_dim), BF16),
        scratch_shapes=[pltpu.VMEM((lp, 2 * pw), BF16), pltpu.VMEM((2, pw, lp), BF16), pltpu.VMEM((pw, tq), F32),
                        pltpu.VMEM((2, tq, tq), F32), pltpu.VMEM((2, tq, tq), F32), pltpu.VMEM((2 * tq, tq), BF16)],
        compiler_params=_cparams(3),
        name="prompt_attention",
    )(proj, proj, proj, ccol, crow)


def _decode_attn_kernel(pt_ref, q_ref, kn_ref, vn_ref, lfn_ref, *refs, pages):
    k_refs = refs[0:pages]
    v_refs = refs[pages:2 * pages]
    lf_refs = refs[2 * pages:3 * pages]
    o_ref, qb_ref, s_ref, p_ref, a_ref, m_ref, l_ref, carry_ref, acc_ref = refs[3 * pages:]
    s_id = pl.program_id(1)
    nh, hd, psz = k_refs[0].shape

    @pl.when(s_id == 0)
    def _():
        qb_ref[...] = q_ref[...] * hd ** -0.5
        lane = lax.broadcasted_iota(jnp.int32, (1, psz), 1)
        for h in range(nh):
            s_ref[h:h + 1, :] = jnp.sum(kn_ref[h] * qb_ref[h], axis=0, keepdims=True)
            acc_ref[h] = jnp.where(lane == 0, vn_ref[h], 0.0)
        m_ref[...] = s_ref[:, 0:1]
        l_ref[...] = jnp.ones_like(l_ref)
        carry_ref[...] = lfn_ref[:, 0:1]

    r_i = lax.broadcasted_iota(jnp.int32, (psz, psz), 0)
    c_i = lax.broadcasted_iota(jnp.int32, (psz, psz), 1)
    later = (r_i > c_i).astype(BF16)
    for i in range(pages):
        for h in range(nh):
            s_ref[h:h + 1, :] = jnp.sum(k_refs[i][h] * qb_ref[h], axis=0, keepdims=True)
        lf = lf_refs[i][...]
        hi, mid, lo = _split3(lf)
        within = (_dot(hi.astype(BF16), later) + _dot(mid.astype(BF16), later)) + _dot(lo.astype(BF16), later)
        carry = carry_ref[...]
        s = s_ref[...] + (within + carry)
        carry_ref[...] = carry + jnp.sum(lf, axis=1, keepdims=True)
        m_old = m_ref[...]
        m_new = jnp.maximum(m_old, jnp.max(s, axis=1, keepdims=True))
        alpha = jnp.exp(m_old - m_new)
        p = jnp.exp(s - m_new)
        l_ref[...] = alpha * l_ref[...] + jnp.sum(p, axis=1, keepdims=True)
        m_ref[...] = m_new
        p_ref[...] = p
        a_ref[...] = jnp.broadcast_to(alpha, (nh, psz))
        for h in range(nh):
            acc_ref[h] = a_ref[h:h + 1, :] * acc_ref[h] + v_refs[i][h] * p_ref[h:h + 1, :]

    @pl.when(s_id == pl.num_programs(1) - 1)
    def _():
        for h in range(nh):
            o_h = jnp.sum(acc_ref[h], axis=1, keepdims=True) / l_ref[h:h + 1, :]
            o_ref[h] = jnp.broadcast_to(o_h, (hd, psz))


def decode_attention(q, k_new, v_new, logf_new, cache_k, cache_v, cache_logf, page_table, layer):
    nb, n_pages = page_table.shape
    _, _, psz, nh, hd = cache_k.shape
    pages = _pick(n_pages, (8, 4, 2, 1))
    kt = jnp.transpose(cache_k, (0, 1, 3, 4, 2))
    vt = jnp.transpose(cache_v, (0, 1, 3, 4, 2))
    lft = jnp.transpose(cache_logf, (0, 1, 3, 2))
    rep = lambda x: jnp.broadcast_to(x[..., None], x.shape + (psz,))

    def page(i, dims):
        def index(b, s, pt):
            return (pt[b, n_pages - 1 - (s * pages + i)], layer) + (0,) * len(dims)
        return pl.BlockSpec((None, None) + dims, index)

    tok = pl.BlockSpec((None, nh, hd, psz), lambda b, s, pt: (b, 0, 0, 0))
    grid_spec = pltpu.PrefetchScalarGridSpec(
        num_scalar_prefetch=1,
        grid=(nb, n_pages // pages),
        in_specs=([tok, tok, tok, pl.BlockSpec((None, nh, psz), lambda b, s, pt: (b, 0, 0))]
                  + [page(i, (nh, hd, psz)) for i in range(pages)]
                  + [page(i, (nh, hd, psz)) for i in range(pages)]
                  + [page(i, (nh, psz)) for i in range(pages)]),
        out_specs=tok,
        scratch_shapes=[pltpu.VMEM((nh, hd, psz), F32), pltpu.VMEM((nh, psz), F32), pltpu.VMEM((nh, psz), F32),
                        pltpu.VMEM((nh, psz), F32), pltpu.VMEM((nh, 1), F32), pltpu.VMEM((nh, 1), F32),
                        pltpu.VMEM((nh, 1), F32), pltpu.VMEM((nh, hd, psz), F32)],
    )
    out = pl.pallas_call(
        functools.partial(_decode_attn_kernel, pages=pages),
        grid_spec=grid_spec,
        out_shape=jax.ShapeDtypeStruct((nb, nh, hd, psz), F32),
        compiler_params=_cparams(2),
        name="decode_attention",
    )(page_table, rep(q), rep(k_new), rep(v_new), rep(logf_new),
      *([kt] * pages), *([vt] * pages), *([lft] * pages))
    return out[..., 0]


def _final_norm_shift_kernel(a_ref, b_ref, g_ref, o_ref, *, shift):
    x = jnp.concatenate([a_ref[shift:, :], b_ref[0:shift, :]], axis=0)
    ms = jnp.mean(x * x, axis=-1, keepdims=True)
    o_ref[...] = (x * lax.rsqrt(ms + NORM_EPS)) * g_ref[...]


def final_norm_prompt(x, g, *, nb, lp, seq, shift):
    d = x.shape[1]
    tr = SEQ_TILE
    nt = seq // tr
    nl = lp // tr
    return pl.pallas_call(
        functools.partial(_final_norm_shift_kernel, shift=shift),
        grid=(nb, nt),
        in_specs=[pl.BlockSpec((tr, d), lambda b, t: (b * nl + t, 0)),
                  pl.BlockSpec((tr, d), lambda b, t: (b * nl + t + 1, 0)),
                  pl.BlockSpec((1, d), lambda b, t: (0, 0))],
        out_specs=pl.BlockSpec((None, tr, d), lambda b, t: (b, t, 0)),
        out_shape=jax.ShapeDtypeStruct((nb, seq, d), F32),
        compiler_params=_cparams(2),
        name="final_norm_prompt",
    )(x, x, g.reshape(1, d))


def _final_norm_kernel(x_ref, g_ref, o_ref):
    x = x_ref[...]
    ms = jnp.mean(x * x, axis=-1, keepdims=True)
    o_ref[...] = (x * lax.rsqrt(ms + NORM_EPS)) * g_ref[...]


def final_norm(x, g):
    m, d = x.shape
    return pl.pallas_call(
        _final_norm_kernel,
        grid=(1,),
        in_specs=[pl.BlockSpec((m, d), lambda i: (0, 0)), pl.BlockSpec((1, d), lambda i: (0, 0))],
        out_specs=pl.BlockSpec((m, d), lambda i: (0, 0)),
        out_shape=jax.ShapeDtypeStruct((m, d), F32),
        compiler_params=_cparams(1),
        name="final_norm",
    )(x, g.reshape(1, d))


def _pack_kernel(offs_ref, w_ref, ws_ref, o_ref, *, small_tile):
    w = jnp.where(pl.program_id(1) == small_tile, ws_ref[...], w_ref[0])
    o_ref[...] = w.T.astype(o_ref.dtype)


def pack_w_in(w_in_t, dims):
    depth, n_out, d = w_in_t.shape
    pd, di, cd, nh_s, ad, nh_a, dm = dims
    tile = SMALL_W
    src = {}
    o = 0
    for name, width in (("pool", pd), ("z", di), ("xbc", cd), ("dt", nh_s), ("q", ad), ("k", ad), ("v", ad),
                        ("f", nh_a), ("gate", n_out - (pd + di + cd + nh_s + 3 * ad + nh_a))):
        src[name] = (o, width)
        o += width
    assert DT_LANE == 0 and nh_s <= F_LANE
    rows = lambda name: w_in_t[:, src[name][0]:src[name][0] + src[name][1]]
    zeros = lambda w: jnp.zeros((depth, w, d), w_in_t.dtype)
    small = jnp.concatenate([rows("dt"), zeros(F_LANE - nh_s), rows("f"), zeros(tile - F_LANE - nh_a)], axis=1)
    cols, offs = {}, []
    for name in ("xbc", "small", "pool", "z", "q", "k", "v", "gate"):
        cols[name] = len(offs) * tile
        if name == "small":
            small_tile = len(offs)
            offs.append(0)
        else:
            start, width = src[name]
            assert width % tile == 0 and start % SUBLANES == 0 and tile % SUBLANES == 0
            offs += [(start + t * tile) // SUBLANES for t in range(width // tile)]
    grid_spec = pltpu.PrefetchScalarGridSpec(
        num_scalar_prefetch=1,
        grid=(depth, len(offs)),
        in_specs=[pl.BlockSpec((pl.Element(1), pl.Element(tile), pl.Element(d)), lambda l, j, o: (l, o[j] * SUBLANES, 0)),
                  pl.BlockSpec((None, tile, d), lambda l, j, o: (l, 0, 0))],
        out_specs=pl.BlockSpec((None, d, tile), lambda l, j, o: (l, 0, j)),
    )
    packed = pl.pallas_call(
        functools.partial(_pack_kernel, small_tile=small_tile),
        grid_spec=grid_spec,
        out_shape=jax.ShapeDtypeStruct((depth, d, len(offs) * tile), BF16),
        compiler_params=_cparams(2),
        name="pack_w_in",
    )(jnp.asarray(offs, jnp.int32), w_in_t, small)
    return packed, cols


def kernel(x_prompt, x_sample, cache_k, cache_v, cache_logf, page_table, state_ssm, state_conv, state_pool,
           meta_tokens, norm_ffn1, ffn1_w_gu, ffn1_w_down, norm_mix, w_in, pool_w, pool_scale, conv_w, conv_b,
           dt_bias, a_log, d_skip, ssm_norm, forget_bias, w_br_pool, w_br_ssm, w_br_attn, w_out, norm_ffn2,
           ffn2_w_gu, ffn2_w_down, final_norm_w):
    nb, seq, dm = x_prompt.shape
    ns = x_sample.shape[0]
    depth = w_in.shape[0]
    n_meta = meta_tokens.shape[0]
    nh_a, hd_a = cache_k.shape[3], cache_k.shape[4]
    ad = nh_a * hd_a
    psz = cache_k.shape[2]
    past_len = page_table.shape[1] * psz
    nh_s, hd_s, n_state = state_ssm.shape[2], state_ssm.shape[3], state_ssm.shape[4]
    di = nh_s * hd_s
    cd = state_conv.shape[3]
    conv_hist = state_conv.shape[2]
    pd = state_pool.shape[3]
    pool_hist = state_pool.shape[2]
    l = seq + n_meta
    lp = -(-l // SEQ_TILE) * SEQ_TILE
    assert x_sample.shape[1] == 1 and seq % SEQ_TILE == 0 and n_meta % 8 == 0 and n_meta < SEQ_TILE
    assert 2 * hd_a == LANES and 2 * hd_s == LANES and n_state == LANES
    assert pool_hist < POOL_HIST_ROWS and conv_hist < CONV_HIST_ROWS

    mp = nb * lp
    m_all = mp + ns * DEC_ROWS
    x = jnp.concatenate(
        [jnp.concatenate([jnp.broadcast_to(meta_tokens[None], (nb, n_meta, dm)), x_prompt,
                          jnp.zeros((nb, lp - l, dm), F32)], axis=1).reshape(mp, dm),
         jnp.pad(x_sample, ((0, 0), (0, DEC_ROWS - 1), (0, 0))).reshape(ns * DEC_ROWS, dm)], axis=0)

    zero_pool_hist = jnp.zeros((nb, POOL_HIST_ROWS, pd), F32)
    zero_conv_hist = jnp.zeros((nb, CONV_HIST_ROWS, cd), F32)
    zero_state = jnp.zeros((nb, di, n_state), F32)
    pool_w_b = pool_w.astype(BF16)
    pool_rows = _pick(lp, (3 * SEQ_TILE, 2 * SEQ_TILE, SEQ_TILE))
    wd1_b, wd2_b = ffn1_w_down.astype(BF16), ffn2_w_down.astype(BF16)
    w_in_p, cols = pack_w_in(jnp.swapaxes(w_in, 1, 2), (pd, di, cd, nh_s, ad, nh_a, dm))

    outs = {k: [] for k in ("kp", "vp", "lfp", "hp", "cp", "pp", "ks", "vs", "lfs", "hs", "cs", "ps")}
    for i in range(depth):
        pw = pool_w_b[i]

        x = matmul_residual(norm_swiglu(x, norm_ffn1[i], ffn1_w_gu, i), wd1_b, i, x, 0.5)
        proj = norm_matmul(x, norm_mix[i], w_in_p, i)

        pool_p, lf_p, ccol, crow = pool_and_forget(
            proj, cols, zero_pool_hist, pw, pool_scale[i], forget_bias[i],
            nb=nb, nt=lp // pool_rows, rows=pool_rows, row0=0, pos0=0, n_valid=l, with_cum=True, out_rows=m_all)
        ssm_p, h_p = ssd_mixer(proj, cols, zero_conv_hist, zero_state, conv_w[i], conv_b[i], dt_bias[i], a_log[i],
                               d_skip[i], ssm_norm[i], nb=nb, nt=lp // SEQ_TILE, rows=SEQ_TILE, row0=0, n_valid=l,
                               out_rows=m_all)
        att_p = prompt_attention(proj, cols, ccol, crow, nb=nb, lp=lp, head_dim=hd_a, out_rows=m_all)

        hist_p = jnp.pad(state_pool[:, i], ((0, 0), (POOL_HIST_ROWS - pool_hist, 0), (0, 0)))
        hist_c = jnp.pad(state_conv[:, i], ((0, 0), (CONV_HIST_ROWS - conv_hist, 0), (0, 0)))
        pool_s, lf_s = pool_and_forget(
            proj, cols, hist_p, pw, pool_scale[i], forget_bias[i],
            nb=ns, nt=1, rows=DEC_ROWS, row0=mp, pos0=past_len, n_valid=1, with_cum=False)
        ssm_s, h_s = ssd_mixer(proj, cols, hist_c, state_ssm[:, i].reshape(ns, di, n_state), conv_w[i], conv_b[i],
                               dt_bias[i], a_log[i], d_skip[i], ssm_norm[i],
                               nb=ns, nt=1, rows=DEC_ROWS, row0=mp, n_valid=1)
        ps0 = proj[mp:].reshape(ns, DEC_ROWS, -1)[:, 0:1]
        heads = lambda name: ps0[:, 0, cols[name]:cols[name] + ad].reshape(ns, nh_a, hd_a)
        att_s = decode_attention(heads("q"), heads("k"), heads("v"), lf_s.reshape(ns, DEC_ROWS, nh_a)[:, 0],
                                 cache_k, cache_v, cache_logf, page_table, i)
        att_s = jnp.pad(att_s.reshape(ns, 1, ad), ((0, 0), (0, DEC_ROWS - 1), (0, 0)))
        att_s = att_s.reshape(ns * DEC_ROWS, ad).astype(BF16)

        place = lambda full, part: lax.dynamic_update_slice(full, part, (mp, 0))
        merged = gated_merge(place(pool_p, pool_s), place(ssm_p, ssm_s), place(att_p, att_s), proj, cols["gate"],
                             w_br_pool, w_br_ssm, w_br_attn, i)
        x = matmul_residual(merged, w_out, i, x, 1.0)
        x = matmul_residual(norm_swiglu(x, norm_ffn2[i], ffn2_w_gu, i), wd2_b, i, x, 0.5)

        def prompt_rows(first, last, name, width):
            return jnp.stack([proj[b * lp + first:b * lp + last, cols[name]:cols[name] + width] for b in range(nb)])

        outs["kp"].append(prompt_rows(0, l, "k", ad).reshape(nb, l, nh_a, hd_a))
        outs["vp"].append(prompt_rows(0, l, "v", ad).reshape(nb, l, nh_a, hd_a))
        outs["lfp"].append(lf_p.reshape(nb, lp, nh_a)[:, :l])
        outs["hp"].append(h_p.reshape(nb, nh_s, hd_s, n_state))
        outs["cp"].append(prompt_rows(l - conv_hist, l, "xbc", cd))
        outs["pp"].append(prompt_rows(l - pool_hist, l, "pool", pd))
        outs["ks"].append(ps0[:, :, cols["k"]:cols["k"] + ad].reshape(ns, 1, nh_a, hd_a))
        outs["vs"].append(ps0[:, :, cols["v"]:cols["v"] + ad].reshape(ns, 1, nh_a, hd_a))
        outs["lfs"].append(lf_s.reshape(ns, DEC_ROWS, nh_a)[:, 0:1])
        outs["hs"].append(h_s.reshape(ns, nh_s, hd_s, n_state))
        outs["cs"].append(jnp.concatenate([state_conv[:, i, 1:], ps0[:, :, cols["xbc"]:cols["xbc"] + cd]], axis=1))
        outs["ps"].append(jnp.concatenate([state_pool[:, i, 1:], ps0[:, :, cols["pool"]:cols["pool"] + pd]], axis=1))

    y_prompt = final_norm_prompt(x, final_norm_w, nb=nb, lp=lp, seq=seq, shift=n_meta)
    y_sample = final_norm(x[mp:], final_norm_w).reshape(ns, DEC_ROWS, dm)[:, 0:1]
    st = lambda k: jnp.stack(outs[k], axis=1)
    return (y_prompt, y_sample, st("kp"), st("vp"), st("lfp"), st("ks"), st("vs"), st("lfs"),
            st("hp"), st("hs"), st("cp"), st("cs"), st("pp"), st("ps"))
```

```python
import functools

import jax
import jax.numpy as jnp
from jax import lax
from jax.experimental import pallas as pl
from jax.experimental.pallas import tpu as pltpu

F32 = jnp.float32
BF16 = jnp.bfloat16

NORM_EPS = 1e-6
NEG_INF = -1e30
LOG2E = 1.4426950408889634
POOL_WINDOWS = (2, 4, 8, 16)
POOL_HIST_ROWS = 16
CONV_TAPS = 4
CONV_HIST_ROWS = 8
SEQ_TILE = 128
DEC_ROWS = 16
LANES = 128
SUBLANES = 8
SMALL_W = 512
DT_LANE = 0
F_LANE = 128
VMEM_LIMIT_BYTES = 56 * 1024 * 1024
MAX_WEIGHT_BLOCK_BYTES = 8 * 1024 * 1024
ROW_TILES = (1072, 1056, 768, 704, 528, 512, 384, 256, 128)


def _cparams(n_axes):
    return pltpu.CompilerParams(dimension_semantics=("arbitrary",) * n_axes,
                                vmem_limit_bytes=VMEM_LIMIT_BYTES)


def _pick(n, cands):
    for c in cands:
        if n % c == 0:
            return c
    return n


def _sigmoid(x):
    return 1.0 / (1.0 + jnp.exp(-x))


def _softplus(x):
    return jnp.maximum(x, 0.0) + jnp.log(1.0 + jnp.exp(-jnp.abs(x)))


def _dot(a, b):
    return jnp.dot(a, b, preferred_element_type=F32)


def _dot_nt(a, b):
    return lax.dot_general(a, b, (((1,), (1,)), ((), ())), preferred_element_type=F32)


def _split3(x):
    hi = x.astype(BF16).astype(F32)
    r = x - hi
    mid = r.astype(BF16).astype(F32)
    lo = (r - mid).astype(BF16).astype(F32)
    return hi, mid, lo


def _select_dot(sel, x):
    s = sel.astype(BF16)
    hi, mid, lo = _split3(x)
    return (_dot(s, hi.astype(BF16)) + _dot(s, mid.astype(BF16))) + _dot(s, lo.astype(BF16))


def _dot_select(x, sel):
    s = sel.astype(BF16)
    hi, mid, lo = _split3(x)
    return (_dot(hi.astype(BF16), s) + _dot(mid.astype(BF16), s)) + _dot(lo.astype(BF16), s)


def _rms_to_scratch(x_ref, g_ref, xn_ref):
    x = x_ref[...]
    ms = jnp.mean(x * x, axis=-1, keepdims=True)
    xn_ref[...] = ((x * lax.rsqrt(ms + NORM_EPS)) * g_ref[...]).astype(BF16)


def _norm_swiglu_kernel(x_ref, g_ref, wg_ref, wu_ref, o_ref, xn_ref):
    @pl.when(pl.program_id(1) == 0)
    def _():
        _rms_to_scratch(x_ref, g_ref, xn_ref)

    xn = xn_ref[...]
    gate = _dot(xn, wg_ref[...].astype(BF16))
    up = _dot(xn, wu_ref[...].astype(BF16))
    o_ref[...] = ((gate * _sigmoid(gate)) * up).astype(o_ref.dtype)


def norm_swiglu(x, g, w_gu, layer):
    m, d = x.shape
    f = w_gu.shape[2] // 2
    tm = _pick(m, ROW_TILES)
    tn = _pick(f, (512, 256, 128))
    nj = f // tn
    return pl.pallas_call(
        _norm_swiglu_kernel,
        grid=(m // tm, nj),
        in_specs=[pl.BlockSpec((tm, d), lambda i, j: (i, 0)),
                  pl.BlockSpec((1, d), lambda i, j: (0, 0)),
                  pl.BlockSpec((None, d, tn), lambda i, j: (layer, 0, j)),
                  pl.BlockSpec((None, d, tn), lambda i, j: (layer, 0, j + nj))],
        out_specs=pl.BlockSpec((tm, tn), lambda i, j: (i, j)),
        out_shape=jax.ShapeDtypeStruct((m, f), BF16),
        scratch_shapes=[pltpu.VMEM((tm, d), BF16)],
        compiler_params=_cparams(2),
        name="norm_swiglu",
    )(x, g.reshape(1, d), w_gu, w_gu)


def _matmul_residual_kernel(h_ref, w_ref, x_ref, o_ref, *, scale):
    o_ref[...] = x_ref[...] + scale * _dot(h_ref[...], w_ref[...].astype(BF16))


def matmul_residual(h, w, layer, x, scale):
    m, k = h.shape
    n = w.shape[2]
    tm = _pick(m, ROW_TILES)
    wide = k * 512 * w.dtype.itemsize <= MAX_WEIGHT_BLOCK_BYTES
    tn = _pick(n, (512, 256, 128) if wide else (256, 128))
    return pl.pallas_call(
        functools.partial(_matmul_residual_kernel, scale=scale),
        grid=(m // tm, n // tn),
        in_specs=[pl.BlockSpec((tm, k), lambda i, j: (i, 0)),
                  pl.BlockSpec((None, k, tn), lambda i, j: (layer, 0, j)),
                  pl.BlockSpec((tm, tn), lambda i, j: (i, j))],
        out_specs=pl.BlockSpec((tm, tn), lambda i, j: (i, j)),
        out_shape=jax.ShapeDtypeStruct((m, n), F32),
        compiler_params=_cparams(2),
        name="matmul_residual",
    )(h, w, x)


def _norm_matmul_kernel(x_ref, g_ref, w_ref, o_ref, xn_ref):
    @pl.when(pl.program_id(1) == 0)
    def _():
        _rms_to_scratch(x_ref, g_ref, xn_ref)

    o_ref[...] = _dot(xn_ref[...], w_ref[...])


def norm_matmul(x, g, w, layer):
    m, d = x.shape
    n = w.shape[2]
    tm = _pick(m, ROW_TILES)
    tn = _pick(n, (1024, 512, 256, 128))
    return pl.pallas_call(
        _norm_matmul_kernel,
        grid=(m // tm, n // tn),
        in_specs=[pl.BlockSpec((tm, d), lambda i, j: (i, 0)),
                  pl.BlockSpec((1, d), lambda i, j: (0, 0)),
                  pl.BlockSpec((None, d, tn), lambda i, j: (layer, 0, j))],
        out_specs=pl.BlockSpec((tm, tn), lambda i, j: (i, j)),
        out_shape=jax.ShapeDtypeStruct((m, n), F32),
        scratch_shapes=[pltpu.VMEM((tm, d), BF16)],
        compiler_params=_cparams(2),
        name="norm_matmul",
    )(x, g.reshape(1, d), w)


def _merge_kernel(p_ref, s_ref, a_ref, g0_ref, g1_ref, g2_ref, wp_ref, ws_ref, wa_ref, o_ref):
    merged = (_sigmoid(g0_ref[...]) * _dot(p_ref[...], wp_ref[...].astype(BF16))
              + _sigmoid(g1_ref[...]) * _dot(s_ref[...], ws_ref[...].astype(BF16))
              + _sigmoid(g2_ref[...]) * _dot(a_ref[...], wa_ref[...].astype(BF16)))
    o_ref[...] = merged.astype(o_ref.dtype)


def gated_merge(pool_out, ssm_out, att, proj, gate_col, w_pool, w_ssm, w_attn, layer):
    m, kb = pool_out.shape
    d = w_pool.shape[2]
    tm = _pick(m, ROW_TILES)
    tn = _pick(d, (512, 256, 128))
    g_blk = gate_col // tn
    nj = d // tn
    branch = pl.BlockSpec((tm, kb), lambda i, j: (i, 0))
    weight = pl.BlockSpec((None, kb, tn), lambda i, j: (layer, 0, j))

    def gate(r):
        return pl.BlockSpec((tm, tn), lambda i, j: (i, g_blk + r * nj + j))

    return pl.pallas_call(
        _merge_kernel,
        grid=(m // tm, nj),
        in_specs=[branch, branch, branch, gate(0), gate(1), gate(2), weight, weight, weight],
        out_specs=pl.BlockSpec((tm, tn), lambda i, j: (i, j)),
        out_shape=jax.ShapeDtypeStruct((m, d), BF16),
        compiler_params=_cparams(2),
        name="gated_merge",
    )(pool_out, ssm_out, att, proj, proj, proj, w_pool, w_ssm, w_attn)


def _prep_kernel(xp_ref, sm_ref, hist_ref, pw_ref, ps_ref, fb_ref, *refs, rows, pos0, n_valid, with_cum):
    if with_cum:
        pool_ref, logf_ref, ccol_ref, crow_ref, full_ref, carry_ref = refs
    else:
        pool_ref, logf_ref, full_ref = refs
    t = pl.program_id(1)
    hr = POOL_HIST_ROWS

    @pl.when(t == 0)
    def _():
        full_ref[0:hr, :] = hist_ref[...]
        if with_cum:
            carry_ref[...] = jnp.zeros_like(carry_ref)

    x = xp_ref[...]
    full_ref[hr:hr + rows, :] = x
    row = t * rows + lax.broadcasted_iota(jnp.int32, (rows, 1), 0)
    gd = x.shape[1] // len(POOL_WINDOWS)
    for g, w in enumerate(POOL_WINDOWS):
        cols = slice(g * gd, (g + 1) * gd)
        s = full_ref[hr:hr + rows, cols]
        for k in range(1, w):
            s = s + full_ref[hr - k:hr - k + rows, cols]
        cnt = jnp.minimum(w, pos0 + row + 1).astype(F32)
        pooled = s / cnt - x[:, cols]
        mixed = _dot(pooled.astype(BF16), pw_ref[g])
        pool_ref[:, cols] = (mixed * ps_ref[:, cols]).astype(pool_ref.dtype)
    full_ref[0:hr, :] = full_ref[rows:rows + hr, :]

    z = sm_ref[:, F_LANE:F_LANE + logf_ref.shape[1]] + fb_ref[...]
    logf = -_softplus(-z)
    logf_ref[...] = logf
    if with_cum:
        lf = jnp.where(row < n_valid, logf, 0.0)
        r_i = lax.broadcasted_iota(jnp.int32, (rows, rows), 0)
        c_i = lax.broadcasted_iota(jnp.int32, (rows, rows), 1)
        cs = _select_dot(r_i >= c_i, lf) + carry_ref[...]
        ccol_ref[...] = cs
        carry_ref[...] = cs[rows - 1:rows, :]
        nh = cs.shape[1]
        wide = jnp.concatenate([cs, jnp.zeros((rows, LANES - nh), F32)], axis=1)
        crow_ref[...] = wide.T[0:nh, :]


def pool_and_forget(proj, cols, hist, pool_w, pool_scale, forget_bias, *, nb, nt, rows, row0, pos0, n_valid,
                    with_cum, out_rows=None):
    mtot = nb * nt * rows
    blk0 = row0 // rows
    pd = pool_w.shape[0] * pool_w.shape[1]
    nh = forget_bias.shape[0]
    xp_blk = cols["pool"] // pd
    sm_blk = cols["small"] // SMALL_W
    in_specs = [pl.BlockSpec((rows, pd), lambda b, t: (blk0 + b * nt + t, xp_blk)),
                pl.BlockSpec((rows, SMALL_W), lambda b, t: (blk0 + b * nt + t, sm_blk)),
                pl.BlockSpec((None, POOL_HIST_ROWS, pd), lambda b, t: (b, 0, 0)),
                pl.BlockSpec(pool_w.shape, lambda b, t: (0, 0, 0)),
                pl.BlockSpec((1, pd), lambda b, t: (0, 0)),
                pl.BlockSpec((1, nh), lambda b, t: (0, 0))]
    out_specs = [pl.BlockSpec((rows, pd), lambda b, t: (b * nt + t, 0)),
                 pl.BlockSpec((rows, nh), lambda b, t: (b * nt + t, 0))]
    out_shape = [jax.ShapeDtypeStruct((out_rows or mtot, pd), BF16), jax.ShapeDtypeStruct((mtot, nh), F32)]
    scratch = [pltpu.VMEM((POOL_HIST_ROWS + rows, pd), F32)]
    if with_cum:
        out_specs += [pl.BlockSpec((rows, nh), lambda b, t: (b * nt + t, 0)),
                      pl.BlockSpec((None, nh, rows), lambda b, t: (b, 0, t))]
        out_shape += [jax.ShapeDtypeStruct((mtot, nh), F32), jax.ShapeDtypeStruct((nb, nh, nt * rows), F32)]
        scratch += [pltpu.VMEM((1, nh), F32)]
    return pl.pallas_call(
        functools.partial(_prep_kernel, rows=rows, pos0=pos0, n_valid=n_valid, with_cum=with_cum),
        grid=(nb, nt),
        in_specs=in_specs, out_specs=out_specs, out_shape=out_shape, scratch_shapes=scratch,
        compiler_params=_cparams(2),
        name="pool_and_forget",
    )(proj, proj, hist, pool_w, pool_scale.reshape(1, pd), forget_bias.reshape(1, nh))


def _ssd_kernel(xz_ref, z_ref, hist_ref, h0_ref, cw_ref, cb_ref, dtb_ref, alog_ref, dskip_ref, nw_ref,
                ex_ref, ext_ref, o_ref, h_ref, full_ref, y_ref, *, rows, n_valid, d_inner, n_state, head_dim):
    t = SEQ_TILE
    c = pl.program_id(1)
    hr = CONV_HIST_ROWS
    cdim = full_ref.shape[1]
    nheads = d_inner // head_dim
    ngroups = (cdim - d_inner) // (2 * n_state)
    pair_w = 2 * head_dim
    pairs_per_group = d_inner // pair_w // ngroups

    @pl.when(c == 0)
    def _():
        full_ref[0:hr, :] = hist_ref[...]
        h_ref[...] = h0_ref[...]

    full_ref[hr:hr + rows, :] = xz_ref[:, 0:cdim]
    if rows < t:
        full_ref[hr + rows:hr + t, :] = jnp.zeros((t - rows, cdim), F32)
    conv = cb_ref[...]
    for k in range(CONV_TAPS):
        off = hr - (CONV_TAPS - 1) + k
        conv = conv + full_ref[off:off + t, :] * cw_ref[k:k + 1, :]
    act = conv * _sigmoid(conv)
    full_ref[0:hr, :] = full_ref[t:t + hr, :]

    dt_raw = xz_ref[:, cdim + DT_LANE:cdim + DT_LANE + nheads]
    if rows < t:
        dt_raw = jnp.concatenate([dt_raw, jnp.zeros((t - rows, nheads), F32)], axis=0)
    row = c * t + lax.broadcasted_iota(jnp.int32, (t, 1), 0)
    dt = jnp.where(row < n_valid, _softplus(dt_raw + dtb_ref[...]), 0.0)
    a = -jnp.exp(alog_ref[...])
    r_i = lax.broadcasted_iota(jnp.int32, (t, t), 0)
    c_i = lax.broadcasted_iota(jnp.int32, (t, t), 1)
    causal = r_i >= c_i
    acum = _select_dot(causal, dt * a)
    lane_pad = jnp.zeros((t, LANES - nheads), F32)
    acum_row = jnp.concatenate([acum, lane_pad], axis=1).T[0:nheads, :]
    dt_row = jnp.concatenate([dt, lane_pad], axis=1).T[0:nheads, :]
    a_last = acum[t - 1:t, :]
    decay_in = jnp.exp(acum)
    to_end = jnp.exp(a_last - acum) * dt
    wide = _dot_select(jnp.concatenate([decay_in, to_end], axis=0), ex_ref[...])
    decay_in_x = wide[0:t, :]
    to_end_x = wide[t:2 * t, :]
    chunk_decay = jnp.broadcast_to(jnp.exp(acum_row[:, t - 1:t]), (nheads, n_state))
    chunk_decay_x = _select_dot(ext_ref[...], chunk_decay)

    xs = act[:, 0:d_inner]
    xs_b = xs.astype(BF16)
    lane = lax.broadcasted_iota(jnp.int32, (t, pair_w), 1)
    first = lane < head_dim
    for g in range(ngroups):
        bm = act[:, d_inner + g * n_state:d_inner + (g + 1) * n_state].astype(BF16)
        cm = act[:, d_inner + (ngroups + g) * n_state:d_inner + (ngroups + g + 1) * n_state].astype(BF16)
        cb = _dot_nt(cm, bm)
        for j in range(pairs_per_group):
            p = g * pairs_per_group + j
            cols = slice(p * pair_w, (p + 1) * pair_w)
            wts = []
            for h in (2 * p, 2 * p + 1):
                seg = acum[:, h:h + 1] - acum_row[h:h + 1, :]
                decay = jnp.exp(jnp.where(causal, seg, -jnp.inf))
                wts.append((cb * decay) * dt_row[h:h + 1, :])
            lhs = jnp.concatenate(wts, axis=1).astype(BF16)
            xp = xs_b[:, cols]
            zero = jnp.zeros_like(xp)
            rhs = jnp.concatenate([jnp.where(first, xp, zero), jnp.where(first, zero, xp)], axis=0)
            y = _dot(lhs, rhs)
            hp = h_ref[p * pair_w:(p + 1) * pair_w, :]
            y = y + _dot_nt(cm, hp.astype(BF16)) * decay_in_x[:, cols]
            y_ref[:, cols] = y + dskip_ref[:, cols] * xs[:, cols]
            upd = (xs[:, cols] * to_end_x[:, cols]).T.astype(BF16)
            h_ref[p * pair_w:(p + 1) * pair_w, :] = chunk_decay_x[p * pair_w:(p + 1) * pair_w, :] * hp + _dot(upd, bm)

    zz = z_ref[...]
    if rows < t:
        zz = jnp.concatenate([zz, jnp.zeros((t - rows, d_inner), F32)], axis=0)
    gated = y_ref[...] * (zz * _sigmoid(zz))
    ms = jnp.mean(gated * gated, axis=-1, keepdims=True)
    out = (gated * lax.rsqrt(ms + NORM_EPS)) * nw_ref[...]
    o_ref[...] = out[0:rows, :].astype(o_ref.dtype)


def ssd_mixer(proj, cols, hist, h0, conv_w, conv_b, dt_bias, a_log, d_skip, ssm_norm, *, nb, nt, rows, row0,
              n_valid, out_rows=None):
    mtot = nb * nt * rows
    blk0 = row0 // rows
    cdim = conv_w.shape[1]
    nheads = dt_bias.shape[0]
    d_inner, n_state = h0.shape[1], h0.shape[2]
    head_dim = d_inner // nheads
    xz_w = cdim + SMALL_W
    expand = jnp.repeat(jnp.eye(nheads, dtype=F32), head_dim, axis=1)
    const2 = lambda b, t: (0, 0)
    return pl.pallas_call(
        functools.partial(_ssd_kernel, rows=rows, n_valid=n_valid, d_inner=d_inner, n_state=n_state,
                          head_dim=head_dim),
        grid=(nb, nt),
        in_specs=[pl.BlockSpec((rows, xz_w), lambda b, t: (blk0 + b * nt + t, cols["xbc"] // xz_w)),
                  pl.BlockSpec((rows, d_inner), lambda b, t: (blk0 + b * nt + t, cols["z"] // d_inner)),
                  pl.BlockSpec((None, CONV_HIST_ROWS, cdim), lambda b, t: (b, 0, 0)),
                  pl.BlockSpec((None, d_inner, n_state), lambda b, t: (b, 0, 0)),
                  pl.BlockSpec((CONV_TAPS, cdim), const2),
                  pl.BlockSpec((1, cdim), const2),
                  pl.BlockSpec((1, nheads), const2),
                  pl.BlockSpec((1, nheads), const2),
                  pl.BlockSpec((1, d_inner), const2),
                  pl.BlockSpec((1, d_inner), const2),
                  pl.BlockSpec((nheads, d_inner), const2),
                  pl.BlockSpec((d_inner, nheads), const2)],
        out_specs=[pl.BlockSpec((rows, d_inner), lambda b, t: (b * nt + t, 0)),
                   pl.BlockSpec((None, d_inner, n_state), lambda b, t: (b, 0, 0))],
        out_shape=[jax.ShapeDtypeStruct((out_rows or mtot, d_inner), BF16),
                   jax.ShapeDtypeStruct((nb, d_inner, n_state), F32)],
        scratch_shapes=[pltpu.VMEM((CONV_HIST_ROWS + SEQ_TILE, cdim), F32),
                        pltpu.VMEM((SEQ_TILE, d_inner), F32)],
        compiler_params=_cparams(2),
        name="ssd_mixer",
    )(proj, proj, hist, h0, conv_w, conv_b.reshape(1, cdim), dt_bias.reshape(1, nheads),
      a_log.reshape(1, nheads), jnp.repeat(d_skip, head_dim).reshape(1, d_inner),
      ssm_norm.reshape(1, d_inner), expand, expand.T)


def _attn_kernel(q_ref, k_ref, v_ref, ccol_ref, crow_ref, o_ref, kaug_ref, vt_ref, acc_ref, sa_ref, sb_ref, p_ref, *, tq,
                 head_dim, lp):
    pair = pl.program_id(1)
    qi = pl.program_id(2)
    nh = ccol_ref.shape[1]
    pw = 2 * head_dim
    tk = tq
    n_parts = 3
    one_lane = 2 * n_parts

    @pl.when(qi == 0)
    def _():
        hid = lax.broadcasted_iota(jnp.int32, (nh, pw), 0)
        lid = lax.broadcasted_iota(jnp.int32, (nh, pw), 1)
        sel = jnp.where((hid == 2 * pair) & (lid < n_parts), 1.0,
                        jnp.where((hid == 2 * pair + 1) & (lid >= n_parts) & (lid < one_lane), 1.0, 0.0))
        lane = lax.broadcasted_iota(jnp.int32, (SEQ_TILE, pw), 1)
        part = lane - jnp.where(lane >= n_parts, n_parts, 0)

        def body(i, carry):
            r = pl.multiple_of(i * SEQ_TILE, SEQ_TILE)
            c2 = _dot_select(ccol_ref[pl.ds(r, SEQ_TILE), :], sel) * (-LOG2E)
            hi, mid, lo = _split3(c2)
            bias = jnp.where(part == 0, hi, jnp.where(part == 1, mid, lo))
            bias = jnp.where(lane < one_lane, bias, jnp.where(lane < one_lane + n_parts, 1.0, 0.0))
            kaug_ref[pl.ds(r, SEQ_TILE), 0:pw] = k_ref[pl.ds(r, SEQ_TILE), :].astype(BF16)
            kaug_ref[pl.ds(r, SEQ_TILE), pw:2 * pw] = bias.astype(BF16)
            vt = v_ref[pl.ds(r, SEQ_TILE), :].T
            d_row = lax.broadcasted_iota(jnp.int32, vt.shape, 0)
            vt_ref[0, :, pl.ds(r, SEQ_TILE)] = jnp.where(d_row < head_dim, vt, 0.0).astype(BF16)
            vt_ref[1, :, pl.ds(r, SEQ_TILE)] = jnp.where(d_row < head_dim, 0.0, vt).astype(BF16)
            return carry

        lax.fori_loop(0, lp // SEQ_TILE, body, 0)

    qs = pl.multiple_of(qi * tq, tq)
    qt = (q_ref[...] * (head_dim ** -0.5 * LOG2E)).T
    row = lax.broadcasted_iota(jnp.int32, (pw, tq), 0)
    q_aug = []
    for e in range(2):
        head_rows = (row < head_dim) if e == 0 else (row >= head_dim)
        c_q = crow_ref[pl.ds(2 * pair + e, 1), pl.ds(qs, tq)] * LOG2E
        hi, mid, lo = _split3(c_q)
        ones_rows = (row >= n_parts * e) & (row < n_parts * (e + 1))
        qb = jnp.where(ones_rows, 1.0,
                       jnp.where(row == one_lane, hi, jnp.where(row == one_lane + 1, mid,
                                                                jnp.where(row == one_lane + 2, lo, 0.0))))
        q_aug.append(jnp.concatenate([jnp.where(head_rows, qt, 0.0), qb], axis=0).astype(BF16))

    acc_ref[...] = jnp.zeros_like(acc_ref)
    q_pos = qs + lax.broadcasted_iota(jnp.int32, (1, tq), 1)

    def scores(kt):
        ks = pl.multiple_of(kt * tk, tk)
        ka = kaug_ref[pl.ds(ks, tk), :]
        return [_dot(ka, q_aug[e]) for e in range(2)]

    def softmax(kt, s_ref, carry, masked):
        ks = pl.multiple_of(kt * tk, tk)
        new = []
        for e in range(2):
            m_old, l_old = carry[3 * e], carry[3 * e + 1]
            s = s_ref[e]
            if masked:
                k_pos = ks + lax.broadcasted_iota(jnp.int32, (tk, 1), 0)
                s = jnp.where(k_pos <= q_pos, s, NEG_INF)
            m_new = jnp.maximum(m_old, jnp.max(s, axis=0, keepdims=True))
            alpha = jnp.exp2(m_old - m_new)
            p = jnp.exp2(s - m_new)
            p_ref[e * tk:(e + 1) * tk, :] = p.astype(BF16)
            new += [m_new, alpha * l_old + jnp.sum(p, axis=0, keepdims=True), alpha]
        return tuple(new)

    def accumulate(kt, carry):
        ks = pl.multiple_of(kt * tk, tk)
        vt = jnp.concatenate([vt_ref[0, :, pl.ds(ks, tk)], vt_ref[1, :, pl.ds(ks, tk)]], axis=1)
        alpha = jnp.where(row < head_dim, carry[2], carry[5])
        acc_ref[...] = alpha * acc_ref[...] + _dot(vt, p_ref[...])

    def stage(kt, cur_ref, nxt_ref, carry):
        s_next = scores(kt + 1)
        accumulate(jnp.maximum(kt - 1, 0), carry)
        carry = softmax(kt, cur_ref, carry, False)
        nxt_ref[0] = s_next[0]
        nxt_ref[1] = s_next[1]
        return carry

    def pair_body(j, carry):
        carry = stage(2 * j, sa_ref, sb_ref, carry)
        return stage(2 * j + 1, sb_ref, sa_ref, carry)

    def finish(diag_ref, carry):
        accumulate(jnp.maximum(qi - 1, 0), carry)
        carry = softmax(qi, diag_ref, carry, True)
        accumulate(qi, carry)
        out_t = acc_ref[...] / jnp.where(row < head_dim, carry[1], carry[4])
        o_ref[...] = out_t.T.astype(o_ref.dtype)

    s_first = scores(0)
    sa_ref[0] = s_first[0]
    sa_ref[1] = s_first[1]
    p_ref[...] = jnp.zeros_like(p_ref)
    init = (jnp.full((1, tq), NEG_INF, F32), jnp.zeros((1, tq), F32), jnp.ones((1, tq), F32)) * 2
    n_pairs = lax.shift_right_logical(qi, 1)
    carry = lax.fori_loop(0, n_pairs, pair_body, init)
    odd = jnp.bitwise_and(qi, 1)

    @pl.when(odd == 0)
    def _():
        finish(sa_ref, carry)

    @pl.when(odd == 1)
    def _():
        finish(sb_ref, stage(qi - 1, sa_ref, sb_ref, carry))


def prompt_attention(proj, cols, ccol, crow, *, nb, lp, head_dim, out_rows):
    mtot = out_rows
    nh = ccol.shape[1]
    pw = 2 * head_dim
    tq = _pick(lp, (384, 256, 128))
    nq = lp // tq
    return pl.pallas_call(
        functools.partial(_attn_kernel, tq=tq, head_dim=head_dim, lp=lp),
        grid=(nb, nh // 2, nq),
        in_specs=[pl.BlockSpec((tq, pw), lambda b, p, i: (b * nq + i, cols["q"] // pw + p)),
                  pl.BlockSpec((lp, pw), lambda b, p, i: (b, cols["k"] // pw + p)),
                  pl.BlockSpec((lp, pw), lambda b, p, i: (b, cols["v"] // pw + p)),
                  pl.BlockSpec((lp, nh), lambda b, p, i: (b, 0)),
                  pl.BlockSpec((None, nh, lp), lambda b, p, i: (b, 0, 0))],
        out_specs=pl.BlockSpec((tq, pw), lambda b, p, i: (b * nq + i, p)),
        out_shape=jax.ShapeDtypeStruct((mtot, nh * head_dim), BF16),
        scratch_shapes=[pltpu.VMEM((lp, 2 * pw), BF16), pltpu.VMEM((2, pw, lp), BF16), pltpu.VMEM((pw, tq), F32),
                        pltpu.VMEM((2, tq, tq), F32), pltpu.VMEM((2, tq, tq), F32), pltpu.VMEM((2 * tq, tq), BF16)],
        compiler_params=_cparams(3),
        name="prompt_attention",
    )(proj, proj, proj, ccol, crow)


def _decode_attn_kernel(pt_ref, q_ref, kn_ref, vn_ref, lfn_ref, *refs, pages):
    k_refs = refs[0:pages]
    v_refs = refs[pages:2 * pages]
    lf_refs = refs[2 * pages:3 * pages]
    o_ref, qb_ref, s_ref, p_ref, a_ref, m_ref, l_ref, carry_ref, acc_ref = refs[3 * pages:]
    s_id = pl.program_id(1)
    nh, hd, psz = k_refs[0].shape

    @pl.when(s_id == 0)
    def _():
        qb_ref[...] = q_ref[...] * hd ** -0.5
        lane = lax.broadcasted_iota(jnp.int32, (1, psz), 1)
        for h in range(nh):
            s_ref[h:h + 1, :] = jnp.sum(kn_ref[h] * qb_ref[h], axis=0, keepdims=True)
            acc_ref[h] = jnp.where(lane == 0, vn_ref[h], 0.0)
        m_ref[...] = s_ref[:, 0:1]
        l_ref[...] = jnp.ones_like(l_ref)
        carry_ref[...] = lfn_ref[:, 0:1]

    r_i = lax.broadcasted_iota(jnp.int32, (psz, psz), 0)
    c_i = lax.broadcasted_iota(jnp.int32, (psz, psz), 1)
    later = (r_i > c_i).astype(BF16)
    for i in range(pages):
        for h in range(nh):
            s_ref[h:h + 1, :] = jnp.sum(k_refs[i][h] * qb_ref[h], axis=0, keepdims=True)
        lf = lf_refs[i][...]
        hi, mid, lo = _split3(lf)
        within = (_dot(hi.astype(BF16), later) + _dot(mid.astype(BF16), later)) + _dot(lo.astype(BF16), later)
        carry = carry_ref[...]
        s = s_ref[...] + (within + carry)
        carry_ref[...] = carry + jnp.sum(lf, axis=1, keepdims=True)
        m_old = m_ref[...]
        m_new = jnp.maximum(m_old, jnp.max(s, axis=1, keepdims=True))
        alpha = jnp.exp(m_old - m_new)
        p = jnp.exp(s - m_new)
        l_ref[...] = alpha * l_ref[...] + jnp.sum(p, axis=1, keepdims=True)
        m_ref[...] = m_new
        p_ref[...] = p
        a_ref[...] = jnp.broadcast_to(alpha, (nh, psz))
        for h in range(nh):
            acc_ref[h] = a_ref[h:h + 1, :] * acc_ref[h] + v_refs[i][h] * p_ref[h:h + 1, :]

    @pl.when(s_id == pl.num_programs(1) - 1)
    def _():
        for h in range(nh):
            o_h = jnp.sum(acc_ref[h], axis=1, keepdims=True) / l_ref[h:h + 1, :]
            o_ref[h] = jnp.broadcast_to(o_h, (hd, psz))


def decode_attention(q, k_new, v_new, logf_new, cache_k, cache_v, cache_logf, page_table, layer):
    nb, n_pages = page_table.shape
    _, _, psz, nh, hd = cache_k.shape
    pages = _pick(n_pages, (8, 4, 2, 1))
    kt = jnp.transpose(cache_k, (0, 1, 3, 4, 2))
    vt = jnp.transpose(cache_v, (0, 1, 3, 4, 2))
    lft = jnp.transpose(cache_logf, (0, 1, 3, 2))
    rep = lambda x: jnp.broadcast_to(x[..., None], x.shape + (psz,))

    def page(i, dims):
        def index(b, s, pt):
            return (pt[b, n_pages - 1 - (s * pages + i)], layer) + (0,) * len(dims)
        return pl.BlockSpec((None, None) + dims, index)

    tok = pl.BlockSpec((None, nh, hd, psz), lambda b, s, pt: (b, 0, 0, 0))
    grid_spec = pltpu.PrefetchScalarGridSpec(
        num_scalar_prefetch=1,
        grid=(nb, n_pages // pages),
        in_specs=([tok, tok, tok, pl.BlockSpec((None, nh, psz), lambda b, s, pt: (b, 0, 0))]
                  + [page(i, (nh, hd, psz)) for i in range(pages)]
                  + [page(i, (nh, hd, psz)) for i in range(pages)]
                  + [page(i, (nh, psz)) for i in range(pages)]),
        out_specs=tok,
        scratch_shapes=[pltpu.VMEM((nh, hd, psz), F32), pltpu.VMEM((nh, psz), F32), pltpu.VMEM((nh, psz), F32),
                        pltpu.VMEM((nh, psz), F32), pltpu.VMEM((nh, 1), F32), pltpu.VMEM((nh, 1), F32),
                        pltpu.VMEM((nh, 1), F32), pltpu.VMEM((nh, hd, psz), F32)],
    )
    out = pl.pallas_call(
        functools.partial(_decode_attn_kernel, pages=pages),
        grid_spec=grid_spec,
        out_shape=jax.ShapeDtypeStruct((nb, nh, hd, psz), F32),
        compiler_params=_cparams(2),
        name="decode_attention",
    )(page_table, rep(q), rep(k_new), rep(v_new), rep(logf_new),
      *([kt] * pages), *([vt] * pages), *([lft] * pages))
    return out[..., 0]


def _final_norm_shift_kernel(a_ref, b_ref, g_ref, o_ref, *, shift):
    x = jnp.concatenate([a_ref[shift:, :], b_ref[0:shift, :]], axis=0)
    ms = jnp.mean(x * x, axis=-1, keepdims=True)
    o_ref[...] = (x * lax.rsqrt(ms + NORM_EPS)) * g_ref[...]


def final_norm_prompt(x, g, *, nb, lp, seq, shift):
    d = x.shape[1]
    tr = SEQ_TILE
    nt = seq // tr
    nl = lp // tr
    return pl.pallas_call(
        functools.partial(_final_norm_shift_kernel, shift=shift),
        grid=(nb, nt),
        in_specs=[pl.BlockSpec((tr, d), lambda b, t: (b * nl + t, 0)),
                  pl.BlockSpec((tr, d), lambda b, t: (b * nl + t + 1, 0)),
                  pl.BlockSpec((1, d), lambda b, t: (0, 0))],
        out_specs=pl.BlockSpec((None, tr, d), lambda b, t: (b, t, 0)),
        out_shape=jax.ShapeDtypeStruct((nb, seq, d), F32),
        compiler_params=_cparams(2),
        name="final_norm_prompt",
    )(x, x, g.reshape(1, d))


def _final_norm_kernel(x_ref, g_ref, o_ref):
    x = x_ref[...]
    ms = jnp.mean(x * x, axis=-1, keepdims=True)
    o_ref[...] = (x * lax.rsqrt(ms + NORM_EPS)) * g_ref[...]


def final_norm(x, g):
    m, d = x.shape
    return pl.pallas_call(
        _final_norm_kernel,
        grid=(1,),
        in_specs=[pl.BlockSpec((m, d), lambda i: (0, 0)), pl.BlockSpec((1, d), lambda i: (0, 0))],
        out_specs=pl.BlockSpec((m, d), lambda i: (0, 0)),
        out_shape=jax.ShapeDtypeStruct((m, d), F32),
        compiler_params=_cparams(1),
        name="final_norm",
    )(x, g.reshape(1, d))


def _pack_kernel(offs_ref, w_ref, ws_ref, o_ref, *, small_tile):
    w = jnp.where(pl.program_id(1) == small_tile, ws_ref[...], w_ref[0])
    o_ref[...] = w.T.astype(o_ref.dtype)


def pack_w_in(w_in_t, dims):
    depth, n_out, d = w_in_t.shape
    pd, di, cd, nh_s, ad, nh_a, dm = dims
    tile = SMALL_W
    src = {}
    o = 0
    for name, width in (("pool", pd), ("z", di), ("xbc", cd), ("dt", nh_s), ("q", ad), ("k", ad), ("v", ad),
                        ("f", nh_a), ("gate", n_out - (pd + di + cd + nh_s + 3 * ad + nh_a))):
        src[name] = (o, width)
        o += width
    assert DT_LANE == 0 and nh_s <= F_LANE
    rows = lambda name: w_in_t[:, src[name][0]:src[name][0] + src[name][1]]
    zeros = lambda w: jnp.zeros((depth, w, d), w_in_t.dtype)
    small = jnp.concatenate([rows("dt"), zeros(F_LANE - nh_s), rows("f"), zeros(tile - F_LANE - nh_a)], axis=1)
    cols, offs = {}, []
    for name in ("xbc", "small", "pool", "z", "q", "k", "v", "gate"):
        cols[name] = len(offs) * tile
        if name == "small":
            small_tile = len(offs)
            offs.append(0)
        else:
            start, width = src[name]
            assert width % tile == 0 and start % SUBLANES == 0 and tile % SUBLANES == 0
            offs += [(start + t * tile) // SUBLANES for t in range(width // tile)]
    grid_spec = pltpu.PrefetchScalarGridSpec(
        num_scalar_prefetch=1,
        grid=(depth, len(offs)),
        in_specs=[pl.BlockSpec((pl.Element(1), pl.Element(tile), pl.Element(d)), lambda l, j, o: (l, o[j] * SUBLANES, 0)),
                  pl.BlockSpec((None, tile, d), lambda l, j, o: (l, 0, 0))],
        out_specs=pl.BlockSpec((None, d, tile), lambda l, j, o: (l, 0, j)),
    )
    packed = pl.pallas_call(
        functools.partial(_pack_kernel, small_tile=small_tile),
        grid_spec=grid_spec,
        out_shape=jax.ShapeDtypeStruct((depth, d, len(offs) * tile), BF16),
        compiler_params=_cparams(2),
        name="pack_w_in",
    )(jnp.asarray(offs, jnp.int32), w_in_t, small)
    return packed, cols


def kernel(x_prompt, x_sample, cache_k, cache_v, cache_logf, page_table, state_ssm, state_conv, state_pool,
           meta_tokens, norm_ffn1, ffn1_w_gu, ffn1_w_down, norm_mix, w_in, pool_w, pool_scale, conv_w, conv_b,
           dt_bias, a_log, d_skip, ssm_norm, forget_bias, w_br_pool, w_br_ssm, w_br_attn, w_out, norm_ffn2,
           ffn2_w_gu, ffn2_w_down, final_norm_w):
    nb, seq, dm = x_prompt.shape
    ns = x_sample.shape[0]
    depth = w_in.shape[0]
    n_meta = meta_tokens.shape[0]
    nh_a, hd_a = cache_k.shape[3], cache_k.shape[4]
    ad = nh_a * hd_a
    psz = cache_k.shape[2]
    past_len = page_table.shape[1] * psz
    nh_s, hd_s, n_state = state_ssm.shape[2], state_ssm.shape[3], state_ssm.shape[4]
    di = nh_s * hd_s
    cd = state_conv.shape[3]
    conv_hist = state_conv.shape[2]
    pd = state_pool.shape[3]
    pool_hist = state_pool.shape[2]
    l = seq + n_meta
    lp = -(-l // SEQ_TILE) * SEQ_TILE
    assert x_sample.shape[1] == 1 and seq % SEQ_TILE == 0 and n_meta % 8 == 0 and n_meta < SEQ_TILE
    assert 2 * hd_a == LANES and 2 * hd_s == LANES and n_state == LANES
    assert pool_hist < POOL_HIST_ROWS and conv_hist < CONV_HIST_ROWS

    mp = nb * lp
    m_all = mp + ns * DEC_ROWS
    x = jnp.concatenate(
        [jnp.concatenate([jnp.broadcast_to(meta_tokens[None], (nb, n_meta, dm)), x_prompt,
                          jnp.zeros((nb, lp - l, dm), F32)], axis=1).reshape(mp, dm),
         jnp.pad(x_sample, ((0, 0), (0, DEC_ROWS - 1), (0, 0))).reshape(ns * DEC_ROWS, dm)], axis=0)

    zero_pool_hist = jnp.zeros((nb, POOL_HIST_ROWS, pd), F32)
    zero_conv_hist = jnp.zeros((nb, CONV_HIST_ROWS, cd), F32)
    zero_state = jnp.zeros((nb, di, n_state), F32)
    pool_w_b = pool_w.astype(BF16)
    pool_rows = _pick(lp, (3 * SEQ_TILE, 2 * SEQ_TILE, SEQ_TILE))
    wd1_b, wd2_b = ffn1_w_down.astype(BF16), ffn2_w_down.astype(BF16)
    wbp_b, wbs_b, wba_b, wo_b = (w.astype(BF16) for w in (w_br_pool, w_br_ssm, w_br_attn, w_out))
    w_in_p, cols = pack_w_in(jnp.swapaxes(w_in, 1, 2), (pd, di, cd, nh_s, ad, nh_a, dm))

    outs = {k: [] for k in ("kp", "vp", "lfp", "hp", "cp", "pp", "ks", "vs", "lfs", "hs", "cs", "ps")}
    for i in range(depth):
        pw = pool_w_b[i]

        x = matmul_residual(norm_swiglu(x, norm_ffn1[i], ffn1_w_gu, i), wd1_b, i, x, 0.5)
        proj = norm_matmul(x, norm_mix[i], w_in_p, i)

        pool_p, lf_p, ccol, crow = pool_and_forget(
            proj, cols, zero_pool_hist, pw, pool_scale[i], forget_bias[i],
            nb=nb, nt=lp // pool_rows, rows=pool_rows, row0=0, pos0=0, n_valid=l, with_cum=True, out_rows=m_all)
        ssm_p, h_p = ssd_mixer(proj, cols, zero_conv_hist, zero_state, conv_w[i], conv_b[i], dt_bias[i], a_log[i],
                               d_skip[i], ssm_norm[i], nb=nb, nt=lp // SEQ_TILE, rows=SEQ_TILE, row0=0, n_valid=l,
                               out_rows=m_all)
        att_p = prompt_attention(proj, cols, ccol, crow, nb=nb, lp=lp, head_dim=hd_a, out_rows=m_all)

        hist_p = jnp.pad(state_pool[:, i], ((0, 0), (POOL_HIST_ROWS - pool_hist, 0), (0, 0)))
        hist_c = jnp.pad(state_conv[:, i], ((0, 0), (CONV_HIST_ROWS - conv_hist, 0), (0, 0)))
        pool_s, lf_s = pool_and_forget(
            proj, cols, hist_p, pw, pool_scale[i], forget_bias[i],
            nb=ns, nt=1, rows=DEC_ROWS, row0=mp, pos0=past_len, n_valid=1, with_cum=False)
        ssm_s, h_s = ssd_mixer(proj, cols, hist_c, state_ssm[:, i].reshape(ns, di, n_state), conv_w[i], conv_b[i],
                               dt_bias[i], a_log[i], d_skip[i], ssm_norm[i],
                               nb=ns, nt=1, rows=DEC_ROWS, row0=mp, n_valid=1)
        ps0 = proj[mp:].reshape(ns, DEC_ROWS, -1)[:, 0:1]
        heads = lambda name: ps0[:, 0, cols[name]:cols[name] + ad].reshape(ns, nh_a, hd_a)
        att_s = decode_attention(heads("q"), heads("k"), heads("v"), lf_s.reshape(ns, DEC_ROWS, nh_a)[:, 0],
                                 cache_k, cache_v, cache_logf, page_table, i)
        att_s = jnp.pad(att_s.reshape(ns, 1, ad), ((0, 0), (0, DEC_ROWS - 1), (0, 0)))
        att_s = att_s.reshape(ns * DEC_ROWS, ad).astype(BF16)

        place = lambda full, part: lax.dynamic_update_slice(full, part, (mp, 0))
        merged = gated_merge(place(pool_p, pool_s), place(ssm_p, ssm_s), place(att_p, att_s), proj, cols["gate"],
                             wbp_b, wbs_b, wba_b, i)
        x = matmul_residual(merged, wo_b, i, x, 1.0)
        x = matmul_residual(norm_swiglu(x, norm_ffn2[i], ffn2_w_gu, i), wd2_b, i, x, 0.5)

        def prompt_rows(first, last, name, width):
            return jnp.stack([proj[b * lp + first:b * lp + last, cols[name]:cols[name] + width] for b in range(nb)])

        outs["kp"].append(prompt_rows(0, l, "k", ad).reshape(nb, l, nh_a, hd_a))
        outs["vp"].append(prompt_rows(0, l, "v", ad).reshape(nb, l, nh_a, hd_a))
        outs["lfp"].append(lf_p.reshape(nb, lp, nh_a)[:, :l])
        outs["hp"].append(h_p.reshape(nb, nh_s, hd_s, n_state))
        outs["cp"].append(prompt_rows(l - conv_hist, l, "xbc", cd))
        outs["pp"].append(prompt_rows(l - pool_hist, l, "pool", pd))
        outs["ks"].append(ps0[:, :, cols["k"]:cols["k"] + ad].reshape(ns, 1, nh_a, hd_a))
        outs["vs"].append(ps0[:, :, cols["v"]:cols["v"] + ad].reshape(ns, 1, nh_a, hd_a))
        outs["lfs"].append(lf_s.reshape(ns, DEC_ROWS, nh_a)[:, 0:1])
        outs["hs"].append(h_s.reshape(ns, nh_s, hd_s, n_state))
        outs["cs"].append(jnp.concatenate([state_conv[:, i, 1:], ps0[:, :, cols["xbc"]:cols["xbc"] + cd]], axis=1))
        outs["ps"].append(jnp.concatenate([state_pool[:, i, 1:], ps0[:, :, cols["pool"]:cols["pool"] + pd]], axis=1))

    y_prompt = final_norm_prompt(x, final_norm_w, nb=nb, lp=lp, seq=seq, shift=n_meta)
    y_sample = final_norm(x[mp:], final_norm_w).reshape(ns, DEC_ROWS, dm)[:, 0:1]
    st = lambda k: jnp.stack(outs[k], axis=1)
    return (y_prompt, y_sample, st("kp"), st("vp"), st("lfp"), st("ks"), st("vs"), st("lfs"),
            st("hp"), st("hs"), st("cp"), st("cs"), st("pp"), st("ps"))
```

```python
import functools

import jax
import jax.numpy as jnp
from jax import lax
from jax.experimental import pallas as pl
from jax.experimental.pallas import tpu as pltpu

F32 = jnp.float32
BF16 = jnp.bfloat16

NORM_EPS = 1e-6
NEG_INF = -1e30
LOG2E = 1.4426950408889634
POOL_WINDOWS = (2, 4, 8, 16)
POOL_HIST_ROWS = 16
CONV_TAPS = 4
CONV_HIST_ROWS = 8
SEQ_TILE = 128
DEC_ROWS = 16
LANES = 128
MXU_COLS = 256
SUBLANES = 8
SMALL_W = 512
DT_LANE = 0
F_LANE = 128
VMEM_LIMIT_BYTES = 56 * 1024 * 1024
MAX_WEIGHT_BLOCK_BYTES = 8 * 1024 * 1024
ROW_TILES = (1072, 1056, 768, 704, 528, 512, 384, 256, 128)


def _cparams(n_axes):
    return pltpu.CompilerParams(dimension_semantics=("arbitrary",) * n_axes,
                                vmem_limit_bytes=VMEM_LIMIT_BYTES)


def _pick(n, cands):
    for c in cands:
        if n % c == 0:
            return c
    return n


def _sigmoid(x):
    return 1.0 / (1.0 + jnp.exp(-x))


def _softplus(x):
    return jnp.maximum(x, 0.0) + jnp.log(1.0 + jnp.exp(-jnp.abs(x)))


def _dot(a, b):
    return jnp.dot(a, b, preferred_element_type=F32)


def _dot_nt(a, b):
    return lax.dot_general(a, b, (((1,), (1,)), ((), ())), preferred_element_type=F32)


def _split3(x):
    hi = x.astype(BF16).astype(F32)
    r = x - hi
    mid = r.astype(BF16).astype(F32)
    lo = (r - mid).astype(BF16).astype(F32)
    return hi, mid, lo


def _select_dot(sel, x):
    s = sel.astype(BF16)
    hi, mid, lo = _split3(x)
    return (_dot(s, hi.astype(BF16)) + _dot(s, mid.astype(BF16))) + _dot(s, lo.astype(BF16))


def _dot_select(x, sel):
    s = sel.astype(BF16)
    hi, mid, lo = _split3(x)
    return (_dot(hi.astype(BF16), s) + _dot(mid.astype(BF16), s)) + _dot(lo.astype(BF16), s)


def _rms_to_scratch(x_ref, g_ref, xn_ref):
    x = x_ref[...]
    ms = jnp.mean(x * x, axis=-1, keepdims=True)
    xn_ref[...] = ((x * lax.rsqrt(ms + NORM_EPS)) * g_ref[...]).astype(BF16)


def _norm_swiglu_kernel(x_ref, g_ref, wg_ref, wu_ref, o_ref, xn_ref):
    @pl.when(pl.program_id(1) == 0)
    def _():
        _rms_to_scratch(x_ref, g_ref, xn_ref)

    xn = xn_ref[...]
    tn = o_ref.shape[1]
    half = tn // 2 if tn % (2 * MXU_COLS) == 0 else tn
    for c in range(0, tn, half):
        gate = _dot(xn, wg_ref[:, c:c + half].astype(BF16))
        up = _dot(xn, wu_ref[:, c:c + half].astype(BF16))
        o_ref[:, c:c + half] = ((gate * _sigmoid(gate)) * up).astype(o_ref.dtype)


def norm_swiglu(x, g, w_gu, layer):
    m, d = x.shape
    f = w_gu.shape[2] // 2
    tm = _pick(m, ROW_TILES)
    tn = _pick(f, (512, 256, 128))
    nj = f // tn
    return pl.pallas_call(
        _norm_swiglu_kernel,
        grid=(m // tm, nj),
        in_specs=[pl.BlockSpec((tm, d), lambda i, j: (i, 0)),
                  pl.BlockSpec((1, d), lambda i, j: (0, 0)),
                  pl.BlockSpec((None, d, tn), lambda i, j: (layer, 0, j)),
                  pl.BlockSpec((None, d, tn), lambda i, j: (layer, 0, j + nj))],
        out_specs=pl.BlockSpec((tm, tn), lambda i, j: (i, j)),
        out_shape=jax.ShapeDtypeStruct((m, f), BF16),
        scratch_shapes=[pltpu.VMEM((tm, d), BF16)],
        compiler_params=_cparams(2),
        name="norm_swiglu",
    )(x, g.reshape(1, d), w_gu, w_gu)


def _matmul_residual_kernel(h_ref, w_ref, x_ref, o_ref, *, scale):
    o_ref[...] = x_ref[...] + scale * _dot(h_ref[...], w_ref[...].astype(BF16))


def matmul_residual(h, w, layer, x, scale):
    m, k = h.shape
    n = w.shape[2]
    tm = _pick(m, ROW_TILES)
    wide = k * 512 * w.dtype.itemsize <= MAX_WEIGHT_BLOCK_BYTES
    tn = _pick(n, (512, 256, 128) if wide else (256, 128))
    return pl.pallas_call(
        functools.partial(_matmul_residual_kernel, scale=scale),
        grid=(m // tm, n // tn),
        in_specs=[pl.BlockSpec((tm, k), lambda i, j: (i, 0)),
                  pl.BlockSpec((None, k, tn), lambda i, j: (layer, 0, j)),
                  pl.BlockSpec((tm, tn), lambda i, j: (i, j))],
        out_specs=pl.BlockSpec((tm, tn), lambda i, j: (i, j)),
        out_shape=jax.ShapeDtypeStruct((m, n), F32),
        compiler_params=_cparams(2),
        name="matmul_residual",
    )(h, w, x)


def _norm_matmul_kernel(x_ref, g_ref, w_ref, o_ref, xn_ref):
    @pl.when(pl.program_id(1) == 0)
    def _():
        _rms_to_scratch(x_ref, g_ref, xn_ref)

    o_ref[...] = _dot(xn_ref[...], w_ref[...])


def norm_matmul(x, g, w, layer):
    m, d = x.shape
    n = w.shape[2]
    tm = _pick(m, ROW_TILES)
    tn = _pick(n, (1024, 512, 256, 128))
    return pl.pallas_call(
        _norm_matmul_kernel,
        grid=(m // tm, n // tn),
        in_specs=[pl.BlockSpec((tm, d), lambda i, j: (i, 0)),
                  pl.BlockSpec((1, d), lambda i, j: (0, 0)),
                  pl.BlockSpec((None, d, tn), lambda i, j: (layer, 0, j))],
        out_specs=pl.BlockSpec((tm, tn), lambda i, j: (i, j)),
        out_shape=jax.ShapeDtypeStruct((m, n), F32),
        scratch_shapes=[pltpu.VMEM((tm, d), BF16)],
        compiler_params=_cparams(2),
        name="norm_matmul",
    )(x, g.reshape(1, d), w)


def _merge_kernel(p_ref, s_ref, a_ref, g0_ref, g1_ref, g2_ref, wp_ref, ws_ref, wa_ref, o_ref):
    merged = (_sigmoid(g0_ref[...]) * _dot(p_ref[...], wp_ref[...].astype(BF16))
              + _sigmoid(g1_ref[...]) * _dot(s_ref[...], ws_ref[...].astype(BF16))
              + _sigmoid(g2_ref[...]) * _dot(a_ref[...], wa_ref[...].astype(BF16)))
    o_ref[...] = merged.astype(o_ref.dtype)


def gated_merge(pool_out, ssm_out, att, proj, gate_col, w_pool, w_ssm, w_attn, layer):
    m, kb = pool_out.shape
    d = w_pool.shape[2]
    tm = _pick(m, ROW_TILES)
    tn = _pick(d, (512, 256, 128))
    g_blk = gate_col // tn
    nj = d // tn
    branch = pl.BlockSpec((tm, kb), lambda i, j: (i, 0))
    weight = pl.BlockSpec((None, kb, tn), lambda i, j: (layer, 0, j))

    def gate(r):
        return pl.BlockSpec((tm, tn), lambda i, j: (i, g_blk + r * nj + j))

    return pl.pallas_call(
        _merge_kernel,
        grid=(m // tm, nj),
        in_specs=[branch, branch, branch, gate(0), gate(1), gate(2), weight, weight, weight],
        out_specs=pl.BlockSpec((tm, tn), lambda i, j: (i, j)),
        out_shape=jax.ShapeDtypeStruct((m, d), BF16),
        compiler_params=_cparams(2),
        name="gated_merge",
    )(pool_out, ssm_out, att, proj, proj, proj, w_pool, w_ssm, w_attn)


def _prep_kernel(xp_ref, sm_ref, hist_ref, pw_ref, ps_ref, fb_ref, *refs, rows, pos0, n_valid, with_cum):
    if with_cum:
        pool_ref, logf_ref, ccol_ref, crow_ref, full_ref, carry_ref = refs
    else:
        pool_ref, logf_ref, full_ref = refs
    t = pl.program_id(1)
    hr = POOL_HIST_ROWS

    @pl.when(t == 0)
    def _():
        full_ref[0:hr, :] = hist_ref[...]
        if with_cum:
            carry_ref[...] = jnp.zeros_like(carry_ref)

    x = xp_ref[...]
    full_ref[hr:hr + rows, :] = x
    row = t * rows + lax.broadcasted_iota(jnp.int32, (rows, 1), 0)
    gd = x.shape[1] // len(POOL_WINDOWS)
    for g, w in enumerate(POOL_WINDOWS):
        cols = slice(g * gd, (g + 1) * gd)
        s = full_ref[hr:hr + rows, cols]
        for k in range(1, w):
            s = s + full_ref[hr - k:hr - k + rows, cols]
        cnt = jnp.minimum(w, pos0 + row + 1).astype(F32)
        pooled = s / cnt - x[:, cols]
        mixed = _dot(pooled.astype(BF16), pw_ref[g])
        pool_ref[:, cols] = (mixed * ps_ref[:, cols]).astype(pool_ref.dtype)
    full_ref[0:hr, :] = full_ref[rows:rows + hr, :]

    z = sm_ref[:, F_LANE:F_LANE + logf_ref.shape[1]] + fb_ref[...]
    logf = -_softplus(-z)
    logf_ref[...] = logf
    if with_cum:
        lf = jnp.where(row < n_valid, logf, 0.0)
        r_i = lax.broadcasted_iota(jnp.int32, (rows, rows), 0)
        c_i = lax.broadcasted_iota(jnp.int32, (rows, rows), 1)
        cs = _select_dot(r_i >= c_i, lf) + carry_ref[...]
        ccol_ref[...] = cs
        carry_ref[...] = cs[rows - 1:rows, :]
        nh = cs.shape[1]
        wide = jnp.concatenate([cs, jnp.zeros((rows, LANES - nh), F32)], axis=1)
        crow_ref[...] = wide.T[0:nh, :]


def pool_and_forget(proj, cols, hist, pool_w, pool_scale, forget_bias, *, nb, nt, rows, row0, pos0, n_valid,
                    with_cum, out_rows=None):
    mtot = nb * nt * rows
    blk0 = row0 // rows
    pd = pool_w.shape[0] * pool_w.shape[1]
    nh = forget_bias.shape[0]
    xp_blk = cols["pool"] // pd
    sm_blk = cols["small"] // SMALL_W
    in_specs = [pl.BlockSpec((rows, pd), lambda b, t: (blk0 + b * nt + t, xp_blk)),
                pl.BlockSpec((rows, SMALL_W), lambda b, t: (blk0 + b * nt + t, sm_blk)),
                pl.BlockSpec((None, POOL_HIST_ROWS, pd), lambda b, t: (b, 0, 0)),
                pl.BlockSpec(pool_w.shape, lambda b, t: (0, 0, 0)),
                pl.BlockSpec((1, pd), lambda b, t: (0, 0)),
                pl.BlockSpec((1, nh), lambda b, t: (0, 0))]
    out_specs = [pl.BlockSpec((rows, pd), lambda b, t: (b * nt + t, 0)),
                 pl.BlockSpec((rows, nh), lambda b, t: (b * nt + t, 0))]
    out_shape = [jax.ShapeDtypeStruct((out_rows or mtot, pd), BF16), jax.ShapeDtypeStruct((mtot, nh), F32)]
    scratch = [pltpu.VMEM((POOL_HIST_ROWS + rows, pd), F32)]
    if with_cum:
        out_specs += [pl.BlockSpec((rows, nh), lambda b, t: (b * nt + t, 0)),
                      pl.BlockSpec((None, nh, rows), lambda b, t: (b, 0, t))]
        out_shape += [jax.ShapeDtypeStruct((mtot, nh), F32), jax.ShapeDtypeStruct((nb, nh, nt * rows), F32)]
        scratch += [pltpu.VMEM((1, nh), F32)]
    return pl.pallas_call(
        functools.partial(_prep_kernel, rows=rows, pos0=pos0, n_valid=n_valid, with_cum=with_cum),
        grid=(nb, nt),
        in_specs=in_specs, out_specs=out_specs, out_shape=out_shape, scratch_shapes=scratch,
        compiler_params=_cparams(2),
        name="pool_and_forget",
    )(proj, proj, hist, pool_w, pool_scale.reshape(1, pd), forget_bias.reshape(1, nh))


def _ssd_kernel(xz_ref, z_ref, hist_ref, h0_ref, cw_ref, cb_ref, dtb_ref, alog_ref, dskip_ref, nw_ref,
                ex_ref, ext_ref, o_ref, h_ref, full_ref, y_ref, *, rows, n_valid, d_inner, n_state, head_dim):
    t = SEQ_TILE
    c = pl.program_id(1)
    hr = CONV_HIST_ROWS
    cdim = full_ref.shape[1]
    nheads = d_inner // head_dim
    ngroups = (cdim - d_inner) // (2 * n_state)
    pair_w = 2 * head_dim
    pairs_per_group = d_inner // pair_w // ngroups

    @pl.when(c == 0)
    def _():
        full_ref[0:hr, :] = hist_ref[...]
        h_ref[...] = h0_ref[...]

    full_ref[hr:hr + rows, :] = xz_ref[:, 0:cdim]
    if rows < t:
        full_ref[hr + rows:hr + t, :] = jnp.zeros((t - rows, cdim), F32)
    conv = cb_ref[...]
    for k in range(CONV_TAPS):
        off = hr - (CONV_TAPS - 1) + k
        conv = conv + full_ref[off:off + t, :] * cw_ref[k:k + 1, :]
    act = conv * _sigmoid(conv)
    full_ref[0:hr, :] = full_ref[t:t + hr, :]

    dt_raw = xz_ref[:, cdim + DT_LANE:cdim + DT_LANE + nheads]
    if rows < t:
        dt_raw = jnp.concatenate([dt_raw, jnp.zeros((t - rows, nheads), F32)], axis=0)
    row = c * t + lax.broadcasted_iota(jnp.int32, (t, 1), 0)
    dt = jnp.where(row < n_valid, _softplus(dt_raw + dtb_ref[...]), 0.0)
    a = -jnp.exp(alog_ref[...])
    r_i = lax.broadcasted_iota(jnp.int32, (t, t), 0)
    c_i = lax.broadcasted_iota(jnp.int32, (t, t), 1)
    causal = r_i >= c_i
    acum = _select_dot(causal, dt * a)
    lane_pad = jnp.zeros((t, LANES - nheads), F32)
    acum_row = jnp.concatenate([acum, lane_pad], axis=1).T[0:nheads, :]
    dt_row = jnp.concatenate([dt, lane_pad], axis=1).T[0:nheads, :]
    a_last = acum[t - 1:t, :]
    decay_in = jnp.exp(acum)
    to_end = jnp.exp(a_last - acum) * dt
    wide = _dot_select(jnp.concatenate([decay_in, to_end], axis=0), ex_ref[...])
    decay_in_x = wide[0:t, :]
    to_end_x = wide[t:2 * t, :]
    chunk_decay = jnp.broadcast_to(jnp.exp(acum_row[:, t - 1:t]), (nheads, n_state))
    chunk_decay_x = _select_dot(ext_ref[...], chunk_decay)

    xs = act[:, 0:d_inner]
    xs_b = xs.astype(BF16)
    lane = lax.broadcasted_iota(jnp.int32, (t, pair_w), 1)
    first = lane < head_dim
    for g in range(ngroups):
        bm = act[:, d_inner + g * n_state:d_inner + (g + 1) * n_state].astype(BF16)
        cm = act[:, d_inner + (ngroups + g) * n_state:d_inner + (ngroups + g + 1) * n_state].astype(BF16)
        cb = _dot_nt(cm, bm)
        for j in range(pairs_per_group):
            p = g * pairs_per_group + j
            cols = slice(p * pair_w, (p + 1) * pair_w)
            wts = []
            for h in (2 * p, 2 * p + 1):
                seg = acum[:, h:h + 1] - acum_row[h:h + 1, :]
                decay = jnp.exp(jnp.where(causal, seg, -jnp.inf))
                wts.append((cb * decay) * dt_row[h:h + 1, :])
            lhs = jnp.concatenate(wts, axis=1).astype(BF16)
            xp = xs_b[:, cols]
            zero = jnp.zeros_like(xp)
            rhs = jnp.concatenate([jnp.where(first, xp, zero), jnp.where(first, zero, xp)], axis=0)
            y = _dot(lhs, rhs)
            hp = h_ref[p * pair_w:(p + 1) * pair_w, :]
            y = y + _dot_nt(cm, hp.astype(BF16)) * decay_in_x[:, cols]
            y_ref[:, cols] = y + dskip_ref[:, cols] * xs[:, cols]
            upd = (xs[:, cols] * to_end_x[:, cols]).T.astype(BF16)
            h_ref[p * pair_w:(p + 1) * pair_w, :] = chunk_decay_x[p * pair_w:(p + 1) * pair_w, :] * hp + _dot(upd, bm)

    zz = z_ref[...]
    if rows < t:
        zz = jnp.concatenate([zz, jnp.zeros((t - rows, d_inner), F32)], axis=0)
    gated = y_ref[...] * (zz * _sigmoid(zz))
    ms = jnp.mean(gated * gated, axis=-1, keepdims=True)
    out = (gated * lax.rsqrt(ms + NORM_EPS)) * nw_ref[...]
    o_ref[...] = out[0:rows, :].astype(o_ref.dtype)


def ssd_mixer(proj, cols, hist, h0, conv_w, conv_b, dt_bias, a_log, d_skip, ssm_norm, *, nb, nt, rows, row0,
              n_valid, out_rows=None):
    mtot = nb * nt * rows
    blk0 = row0 // rows
    cdim = conv_w.shape[1]
    nheads = dt_bias.shape[0]
    d_inner, n_state = h0.shape[1], h0.shape[2]
    head_dim = d_inner // nheads
    xz_w = cdim + SMALL_W
    expand = jnp.repeat(jnp.eye(nheads, dtype=F32), head_dim, axis=1)
    const2 = lambda b, t: (0, 0)
    return pl.pallas_call(
        functools.partial(_ssd_kernel, rows=rows, n_valid=n_valid, d_inner=d_inner, n_state=n_state,
                          head_dim=head_dim),
        grid=(nb, nt),
        in_specs=[pl.BlockSpec((rows, xz_w), lambda b, t: (blk0 + b * nt + t, cols["xbc"] // xz_w)),
                  pl.BlockSpec((rows, d_inner), lambda b, t: (blk0 + b * nt + t, cols["z"] // d_inner)),
                  pl.BlockSpec((None, CONV_HIST_ROWS, cdim), lambda b, t: (b, 0, 0)),
                  pl.BlockSpec((None, d_inner, n_state), lambda b, t: (b, 0, 0)),
                  pl.BlockSpec((CONV_TAPS, cdim), const2),
                  pl.BlockSpec((1, cdim), const2),
                  pl.BlockSpec((1, nheads), const2),
                  pl.BlockSpec((1, nheads), const2),
                  pl.BlockSpec((1, d_inner), const2),
                  pl.BlockSpec((1, d_inner), const2),
                  pl.BlockSpec((nheads, d_inner), const2),
                  pl.BlockSpec((d_inner, nheads), const2)],
        out_specs=[pl.BlockSpec((rows, d_inner), lambda b, t: (b * nt + t, 0)),
                   pl.BlockSpec((None, d_inner, n_state), lambda b, t: (b, 0, 0))],
        out_shape=[jax.ShapeDtypeStruct((out_rows or mtot, d_inner), BF16),
                   jax.ShapeDtypeStruct((nb, d_inner, n_state), F32)],
        scratch_shapes=[pltpu.VMEM((CONV_HIST_ROWS + SEQ_TILE, cdim), F32),
                        pltpu.VMEM((SEQ_TILE, d_inner), F32)],
        compiler_params=_cparams(2),
        name="ssd_mixer",
    )(proj, proj, hist, h0, conv_w, conv_b.reshape(1, cdim), dt_bias.reshape(1, nheads),
      a_log.reshape(1, nheads), jnp.repeat(d_skip, head_dim).reshape(1, d_inner),
      ssm_norm.reshape(1, d_inner), expand, expand.T)


def _attn_kernel(q_ref, k_ref, v_ref, ccol_ref, crow_ref, o_ref, kaug_ref, vt_ref, acc_ref, sa_ref, sb_ref, p_ref, *, tq,
                 head_dim, lp):
    pair = pl.program_id(1)
    qi = pl.program_id(2)
    nh = ccol_ref.shape[1]
    pw = 2 * head_dim
    tk = tq
    n_parts = 3
    one_lane = 2 * n_parts

    @pl.when(qi == 0)
    def _():
        hid = lax.broadcasted_iota(jnp.int32, (nh, pw), 0)
        lid = lax.broadcasted_iota(jnp.int32, (nh, pw), 1)
        sel = jnp.where((hid == 2 * pair) & (lid < n_parts), 1.0,
                        jnp.where((hid == 2 * pair + 1) & (lid >= n_parts) & (lid < one_lane), 1.0, 0.0))
        lane = lax.broadcasted_iota(jnp.int32, (SEQ_TILE, pw), 1)
        part = lane - jnp.where(lane >= n_parts, n_parts, 0)

        def body(i, carry):
            r = pl.multiple_of(i * SEQ_TILE, SEQ_TILE)
            c2 = _dot_select(ccol_ref[pl.ds(r, SEQ_TILE), :], sel) * (-LOG2E)
            hi, mid, lo = _split3(c2)
            bias = jnp.where(part == 0, hi, jnp.where(part == 1, mid, lo))
            bias = jnp.where(lane < one_lane, bias, jnp.where(lane < one_lane + n_parts, 1.0, 0.0))
            kaug_ref[pl.ds(r, SEQ_TILE), 0:pw] = k_ref[pl.ds(r, SEQ_TILE), :].astype(BF16)
            kaug_ref[pl.ds(r, SEQ_TILE), pw:2 * pw] = bias.astype(BF16)
            vt = v_ref[pl.ds(r, SEQ_TILE), :].T
            d_row = lax.broadcasted_iota(jnp.int32, vt.shape, 0)
            vt_ref[0, :, pl.ds(r, SEQ_TILE)] = jnp.where(d_row < head_dim, vt, 0.0).astype(BF16)
            vt_ref[1, :, pl.ds(r, SEQ_TILE)] = jnp.where(d_row < head_dim, 0.0, vt).astype(BF16)
            return carry

        lax.fori_loop(0, lp // SEQ_TILE, body, 0)

    qs = pl.multiple_of(qi * tq, tq)
    qt = (q_ref[...] * (head_dim ** -0.5 * LOG2E)).T
    row = lax.broadcasted_iota(jnp.int32, (pw, tq), 0)
    q_aug = []
    for e in range(2):
        head_rows = (row < head_dim) if e == 0 else (row >= head_dim)
        c_q = crow_ref[pl.ds(2 * pair + e, 1), pl.ds(qs, tq)] * LOG2E
        hi, mid, lo = _split3(c_q)
        ones_rows = (row >= n_parts * e) & (row < n_parts * (e + 1))
        qb = jnp.where(ones_rows, 1.0,
                       jnp.where(row == one_lane, hi, jnp.where(row == one_lane + 1, mid,
                                                                jnp.where(row == one_lane + 2, lo, 0.0))))
        q_aug.append(jnp.concatenate([jnp.where(head_rows, qt, 0.0), qb], axis=0).astype(BF16))

    acc_ref[...] = jnp.zeros_like(acc_ref)
    q_pos = qs + lax.broadcasted_iota(jnp.int32, (1, tq), 1)

    def scores(kt):
        ks = pl.multiple_of(kt * tk, tk)
        ka = kaug_ref[pl.ds(ks, tk), :]
        return [_dot(ka, q_aug[e]) for e in range(2)]

    def softmax(kt, s_ref, carry, masked):
        ks = pl.multiple_of(kt * tk, tk)
        new = []
        for e in range(2):
            m_old, l_old = carry[3 * e], carry[3 * e + 1]
            s = s_ref[e]
            if masked:
                k_pos = ks + lax.broadcasted_iota(jnp.int32, (tk, 1), 0)
                s = jnp.where(k_pos <= q_pos, s, NEG_INF)
            m_new = jnp.maximum(m_old, jnp.max(s, axis=0, keepdims=True))
            alpha = jnp.exp2(m_old - m_new)
            p = jnp.exp2(s - m_new)
            p_ref[e * tk:(e + 1) * tk, :] = p.astype(BF16)
            new += [m_new, alpha * l_old + jnp.sum(p, axis=0, keepdims=True), alpha]
        return tuple(new)

    def accumulate(kt, carry):
        ks = pl.multiple_of(kt * tk, tk)
        vt = jnp.concatenate([vt_ref[0, :, pl.ds(ks, tk)], vt_ref[1, :, pl.ds(ks, tk)]], axis=1)
        alpha = jnp.where(row < head_dim, carry[2], carry[5])
        acc_ref[...] = alpha * acc_ref[...] + _dot(vt, p_ref[...])

    def stage(kt, cur_ref, nxt_ref, carry):
        s_next = scores(kt + 1)
        accumulate(jnp.maximum(kt - 1, 0), carry)
        carry = softmax(kt, cur_ref, carry, False)
        nxt_ref[0] = s_next[0]
        nxt_ref[1] = s_next[1]
        return carry

    def pair_body(j, carry):
        carry = stage(2 * j, sa_ref, sb_ref, carry)
        return stage(2 * j + 1, sb_ref, sa_ref, carry)

    def finish(diag_ref, carry):
        accumulate(jnp.maximum(qi - 1, 0), carry)
        carry = softmax(qi, diag_ref, carry, True)
        accumulate(qi, carry)
        out_t = acc_ref[...] / jnp.where(row < head_dim, carry[1], carry[4])
        o_ref[...] = out_t.T.astype(o_ref.dtype)

    s_first = scores(0)
    sa_ref[0] = s_first[0]
    sa_ref[1] = s_first[1]
    p_ref[...] = jnp.zeros_like(p_ref)
    init = (jnp.full((1, tq), NEG_INF, F32), jnp.zeros((1, tq), F32), jnp.ones((1, tq), F32)) * 2
    n_pairs = lax.shift_right_logical(qi, 1)
    carry = lax.fori_loop(0, n_pairs, pair_body, init)
    odd = jnp.bitwise_and(qi, 1)

    @pl.when(odd == 0)
    def _():
        finish(sa_ref, carry)

    @pl.when(odd == 1)
    def _():
        finish(sb_ref, stage(qi - 1, sa_ref, sb_ref, carry))


def prompt_attention(proj, cols, ccol, crow, *, nb, lp, head_dim, out_rows):
    mtot = out_rows
    nh = ccol.shape[1]
    pw = 2 * head_dim
    tq = _pick(lp, (384, 256, 128))
    nq = lp // tq
    return pl.pallas_call(
        functools.partial(_attn_kernel, tq=tq, head_dim=head_dim, lp=lp),
        grid=(nb, nh // 2, nq),
        in_specs=[pl.BlockSpec((tq, pw), lambda b, p, i: (b * nq + i, cols["q"] // pw + p)),
                  pl.BlockSpec((lp, pw), lambda b, p, i: (b, cols["k"] // pw + p)),
                  pl.BlockSpec((lp, pw), lambda b, p, i: (b, cols["v"] // pw + p)),
                  pl.BlockSpec((lp, nh), lambda b, p, i: (b, 0)),
                  pl.BlockSpec((None, nh, lp), lambda b, p, i: (b, 0, 0))],
        out_specs=pl.BlockSpec((tq, pw), lambda b, p, i: (b * nq + i, p)),
        out_shape=jax.ShapeDtypeStruct((mtot, nh * head_dim), BF16),
        scratch_shapes=[pltpu.VMEM((lp, 2 * pw), BF16), pltpu.VMEM((2, pw, lp), BF16), pltpu.VMEM((pw, tq), F32),
                        pltpu.VMEM((2, tq, tq), F32), pltpu.VMEM((2, tq, tq), F32), pltpu.VMEM((2 * tq, tq), BF16)],
        compiler_params=_cparams(3),
        name="prompt_attention",
    )(proj, proj, proj, ccol, crow)


def _decode_attn_kernel(pt_ref, q_ref, kn_ref, vn_ref, lfn_ref, *refs, pages):
    k_refs = refs[0:pages]
    v_refs = refs[pages:2 * pages]
    lf_refs = refs[2 * pages:3 * pages]
    o_ref, qb_ref, s_ref, p_ref, a_ref, m_ref, l_ref, carry_ref, acc_ref = refs[3 * pages:]
    s_id = pl.program_id(1)
    nh, hd, psz = k_refs[0].shape

    @pl.when(s_id == 0)
    def _():
        qb_ref[...] = q_ref[...] * hd ** -0.5
        lane = lax.broadcasted_iota(jnp.int32, (1, psz), 1)
        for h in range(nh):
            s_ref[h:h + 1, :] = jnp.sum(kn_ref[h] * qb_ref[h], axis=0, keepdims=True)
            acc_ref[h] = jnp.where(lane == 0, vn_ref[h], 0.0)
        m_ref[...] = s_ref[:, 0:1]
        l_ref[...] = jnp.ones_like(l_ref)
        carry_ref[...] = lfn_ref[:, 0:1]

    r_i = lax.broadcasted_iota(jnp.int32, (psz, psz), 0)
    c_i = lax.broadcasted_iota(jnp.int32, (psz, psz), 1)
    later = (r_i > c_i).astype(BF16)
    for i in range(pages):
        for h in range(nh):
            s_ref[h:h + 1, :] = jnp.sum(k_refs[i][h] * qb_ref[h], axis=0, keepdims=True)
        lf = lf_refs[i][...]
        hi, mid, lo = _split3(lf)
        within = (_dot(hi.astype(BF16), later) + _dot(mid.astype(BF16), later)) + _dot(lo.astype(BF16), later)
        carry = carry_ref[...]
        s = s_ref[...] + (within + carry)
        carry_ref[...] = carry + jnp.sum(lf, axis=1, keepdims=True)
        m_old = m_ref[...]
        m_new = jnp.maximum(m_old, jnp.max(s, axis=1, keepdims=True))
        alpha = jnp.exp(m_old - m_new)
        p = jnp.exp(s - m_new)
        l_ref[...] = alpha * l_ref[...] + jnp.sum(p, axis=1, keepdims=True)
        m_ref[...] = m_new
        p_ref[...] = p
        a_ref[...] = jnp.broadcast_to(alpha, (nh, psz))
        for h in range(nh):
            acc_ref[h] = a_ref[h:h + 1, :] * acc_ref[h] + v_refs[i][h] * p_ref[h:h + 1, :]

    @pl.when(s_id == pl.num_programs(1) - 1)
    def _():
        for h in range(nh):
            o_h = jnp.sum(acc_ref[h], axis=1, keepdims=True) / l_ref[h:h + 1, :]
            o_ref[h] = jnp.broadcast_to(o_h, (hd, psz))


def decode_attention(q, k_new, v_new, logf_new, cache_k, cache_v, cache_logf, page_table, layer):
    nb, n_pages = page_table.shape
    _, _, psz, nh, hd = cache_k.shape
    pages = _pick(n_pages, (8, 4, 2, 1))
    kt = jnp.transpose(cache_k, (0, 1, 3, 4, 2))
    vt = jnp.transpose(cache_v, (0, 1, 3, 4, 2))
    lft = jnp.transpose(cache_logf, (0, 1, 3, 2))
    rep = lambda x: jnp.broadcast_to(x[..., None], x.shape + (psz,))

    def page(i, dims):
        def index(b, s, pt):
            return (pt[b, n_pages - 1 - (s * pages + i)], layer) + (0,) * len(dims)
        return pl.BlockSpec((None, None) + dims, index)

    tok = pl.BlockSpec((None, nh, hd, psz), lambda b, s, pt: (b, 0, 0, 0))
    grid_spec = pltpu.PrefetchScalarGridSpec(
        num_scalar_prefetch=1,
        grid=(nb, n_pages // pages),
        in_specs=([tok, tok, tok, pl.BlockSpec((None, nh, psz), lambda b, s, pt: (b, 0, 0))]
                  + [page(i, (nh, hd, psz)) for i in range(pages)]
                  + [page(i, (nh, hd, psz)) for i in range(pages)]
                  + [page(i, (nh, psz)) for i in range(pages)]),
        out_specs=tok,
        scratch_shapes=[pltpu.VMEM((nh, hd, psz), F32), pltpu.VMEM((nh, psz), F32), pltpu.VMEM((nh, psz), F32),
                        pltpu.VMEM((nh, psz), F32), pltpu.VMEM((nh, 1), F32), pltpu.VMEM((nh, 1), F32),
                        pltpu.VMEM((nh, 1), F32), pltpu.VMEM((nh, hd, psz), F32)],
    )
    out = pl.pallas_call(
        functools.partial(_decode_attn_kernel, pages=pages),
        grid_spec=grid_spec,
        out_shape=jax.ShapeDtypeStruct((nb, nh, hd, psz), F32),
        compiler_params=_cparams(2),
        name="decode_attention",
    )(page_table, rep(q), rep(k_new), rep(v_new), rep(logf_new),
      *([kt] * pages), *([vt] * pages), *([lft] * pages))
    return out[..., 0]


def _final_norm_shift_kernel(a_ref, b_ref, g_ref, o_ref, *, shift):
    x = jnp.concatenate([a_ref[shift:, :], b_ref[0:shift, :]], axis=0)
    ms = jnp.mean(x * x, axis=-1, keepdims=True)
    o_ref[...] = (x * lax.rsqrt(ms + NORM_EPS)) * g_ref[...]


def final_norm_prompt(x, g, *, nb, lp, seq, shift):
    d = x.shape[1]
    tr = SEQ_TILE
    nt = seq // tr
    nl = lp // tr
    return pl.pallas_call(
        functools.partial(_final_norm_shift_kernel, shift=shift),
        grid=(nb, nt),
        in_specs=[pl.BlockSpec((tr, d), lambda b, t: (b * nl + t, 0)),
                  pl.BlockSpec((tr, d), lambda b, t: (b * nl + t + 1, 0)),
                  pl.BlockSpec((1, d), lambda b, t: (0, 0))],
        out_specs=pl.BlockSpec((None, tr, d), lambda b, t: (b, t, 0)),
        out_shape=jax.ShapeDtypeStruct((nb, seq, d), F32),
        compiler_params=_cparams(2),
        name="final_norm_prompt",
    )(x, x, g.reshape(1, d))


def _final_norm_kernel(x_ref, g_ref, o_ref):
    x = x_ref[...]
    ms = jnp.mean(x * x, axis=-1, keepdims=True)
    o_ref[...] = (x * lax.rsqrt(ms + NORM_EPS)) * g_ref[...]


def final_norm(x, g):
    m, d = x.shape
    return pl.pallas_call(
        _final_norm_kernel,
        grid=(1,),
        in_specs=[pl.BlockSpec((m, d), lambda i: (0, 0)), pl.BlockSpec((1, d), lambda i: (0, 0))],
        out_specs=pl.BlockSpec((m, d), lambda i: (0, 0)),
        out_shape=jax.ShapeDtypeStruct((m, d), F32),
        compiler_params=_cparams(1),
        name="final_norm",
    )(x, g.reshape(1, d))


def _pack_kernel(offs_ref, w_ref, ws_ref, o_ref, *, small_tile):
    w = jnp.where(pl.program_id(1) == small_tile, ws_ref[...], w_ref[0])
    o_ref[...] = w.T.astype(o_ref.dtype)


def pack_w_in(w_in_t, dims):
    depth, n_out, d = w_in_t.shape
    pd, di, cd, nh_s, ad, nh_a, dm = dims
    tile = SMALL_W
    src = {}
    o = 0
    for name, width in (("pool", pd), ("z", di), ("xbc", cd), ("dt", nh_s), ("q", ad), ("k", ad), ("v", ad),
                        ("f", nh_a), ("gate", n_out - (pd + di + cd + nh_s + 3 * ad + nh_a))):
        src[name] = (o, width)
        o += width
    assert DT_LANE == 0 and nh_s <= F_LANE
    rows = lambda name: w_in_t[:, src[name][0]:src[name][0] + src[name][1]]
    zeros = lambda w: jnp.zeros((depth, w, d), w_in_t.dtype)
    small = jnp.concatenate([rows("dt"), zeros(F_LANE - nh_s), rows("f"), zeros(tile - F_LANE - nh_a)], axis=1)
    cols, offs = {}, []
    for name in ("xbc", "small", "pool", "z", "q", "k", "v", "gate"):
        cols[name] = len(offs) * tile
        if name == "small":
            small_tile = len(offs)
            offs.append(0)
        else:
            start, width = src[name]
            assert width % tile == 0 and start % SUBLANES == 0 and tile % SUBLANES == 0
            offs += [(start + t * tile) // SUBLANES for t in range(width // tile)]
    grid_spec = pltpu.PrefetchScalarGridSpec(
        num_scalar_prefetch=1,
        grid=(depth, len(offs)),
        in_specs=[pl.BlockSpec((pl.Element(1), pl.Element(tile), pl.Element(d)), lambda l, j, o: (l, o[j] * SUBLANES, 0)),
                  pl.BlockSpec((None, tile, d), lambda l, j, o: (l, 0, 0))],
        out_specs=pl.BlockSpec((None, d, tile), lambda l, j, o: (l, 0, j)),
    )
    packed = pl.pallas_call(
        functools.partial(_pack_kernel, small_tile=small_tile),
        grid_spec=grid_spec,
        out_shape=jax.ShapeDtypeStruct((depth, d, len(offs) * tile), BF16),
        compiler_params=_cparams(2),
        name="pack_w_in",
    )(jnp.asarray(offs, jnp.int32), w_in_t, small)
    return packed, cols


def kernel(x_prompt, x_sample, cache_k, cache_v, cache_logf, page_table, state_ssm, state_conv, state_pool,
           meta_tokens, norm_ffn1, ffn1_w_gu, ffn1_w_down, norm_mix, w_in, pool_w, pool_scale, conv_w, conv_b,
           dt_bias, a_log, d_skip, ssm_norm, forget_bias, w_br_pool, w_br_ssm, w_br_attn, w_out, norm_ffn2,
           ffn2_w_gu, ffn2_w_down, final_norm_w):
    nb, seq, dm = x_prompt.shape
    ns = x_sample.shape[0]
    depth = w_in.shape[0]
    n_meta = meta_tokens.shape[0]
    nh_a, hd_a = cache_k.shape[3], cache_k.shape[4]
    ad = nh_a * hd_a
    psz = cache_k.shape[2]
    past_len = page_table.shape[1] * psz
    nh_s, hd_s, n_state = state_ssm.shape[2], state_ssm.shape[3], state_ssm.shape[4]
    di = nh_s * hd_s
    cd = state_conv.shape[3]
    conv_hist = state_conv.shape[2]
    pd = state_pool.shape[3]
    pool_hist = state_pool.shape[2]
    l = seq + n_meta
    lp = -(-l // SEQ_TILE) * SEQ_TILE
    assert x_sample.shape[1] == 1 and seq % SEQ_TILE == 0 and n_meta % 8 == 0 and n_meta < SEQ_TILE
    assert 2 * hd_a == LANES and 2 * hd_s == LANES and n_state == LANES
    assert pool_hist < POOL_HIST_ROWS and conv_hist < CONV_HIST_ROWS

    mp = nb * lp
    m_all = mp + ns * DEC_ROWS
    x = jnp.concatenate(
        [jnp.concatenate([jnp.broadcast_to(meta_tokens[None], (nb, n_meta, dm)), x_prompt,
                          jnp.zeros((nb, lp - l, dm), F32)], axis=1).reshape(mp, dm),
         jnp.pad(x_sample, ((0, 0), (0, DEC_ROWS - 1), (0, 0))).reshape(ns * DEC_ROWS, dm)], axis=0)

    zero_pool_hist = jnp.zeros((nb, POOL_HIST_ROWS, pd), F32)
    zero_conv_hist = jnp.zeros((nb, CONV_HIST_ROWS, cd), F32)
    zero_state = jnp.zeros((nb, di, n_state), F32)
    pool_w_b = pool_w.astype(BF16)
    pool_rows = _pick(lp, (3 * SEQ_TILE, 2 * SEQ_TILE, SEQ_TILE))
    wd1_b, wd2_b = ffn1_w_down.astype(BF16), ffn2_w_down.astype(BF16)
    wbp_b, wbs_b, wba_b, wo_b = (w.astype(BF16) for w in (w_br_pool, w_br_ssm, w_br_attn, w_out))
    w_in_p, cols = pack_w_in(jnp.swapaxes(w_in, 1, 2), (pd, di, cd, nh_s, ad, nh_a, dm))

    outs = {k: [] for k in ("kp", "vp", "lfp", "hp", "cp", "pp", "ks", "vs", "lfs", "hs", "cs", "ps")}
    for i in range(depth):
        pw = pool_w_b[i]

        x = matmul_residual(norm_swiglu(x, norm_ffn1[i], ffn1_w_gu, i), wd1_b, i, x, 0.5)
        proj = norm_matmul(x, norm_mix[i], w_in_p, i)

        pool_p, lf_p, ccol, crow = pool_and_forget(
            proj, cols, zero_pool_hist, pw, pool_scale[i], forget_bias[i],
            nb=nb, nt=lp // pool_rows, rows=pool_rows, row0=0, pos0=0, n_valid=l, with_cum=True, out_rows=m_all)
        ssm_p, h_p = ssd_mixer(proj, cols, zero_conv_hist, zero_state, conv_w[i], conv_b[i], dt_bias[i], a_log[i],
                               d_skip[i], ssm_norm[i], nb=nb, nt=lp // SEQ_TILE, rows=SEQ_TILE, row0=0, n_valid=l,
                               out_rows=m_all)
        att_p = prompt_attention(proj, cols, ccol, crow, nb=nb, lp=lp, head_dim=hd_a, out_rows=m_all)

        hist_p = jnp.pad(state_pool[:, i], ((0, 0), (POOL_HIST_ROWS - pool_hist, 0), (0, 0)))
        hist_c = jnp.pad(state_conv[:, i], ((0, 0), (CONV_HIST_ROWS - conv_hist, 0), (0, 0)))
        pool_s, lf_s = pool_and_forget(
            proj, cols, hist_p, pw, pool_scale[i], forget_bias[i],
            nb=ns, nt=1, rows=DEC_ROWS, row0=mp, pos0=past_len, n_valid=1, with_cum=False)
        ssm_s, h_s = ssd_mixer(proj, cols, hist_c, state_ssm[:, i].reshape(ns, di, n_state), conv_w[i], conv_b[i],
                               dt_bias[i], a_log[i], d_skip[i], ssm_norm[i],
                               nb=ns, nt=1, rows=DEC_ROWS, row0=mp, n_valid=1)
        ps0 = proj[mp:].reshape(ns, DEC_ROWS, -1)[:, 0:1]
        heads = lambda name: ps0[:, 0, cols[name]:cols[name] + ad].reshape(ns, nh_a, hd_a)
        att_s = decode_attention(heads("q"), heads("k"), heads("v"), lf_s.reshape(ns, DEC_ROWS, nh_a)[:, 0],
                                 cache_k, cache_v, cache_logf, page_table, i)
        att_s = jnp.pad(att_s.reshape(ns, 1, ad), ((0, 0), (0, DEC_ROWS - 1), (0, 0)))
        att_s = att_s.reshape(ns * DEC_ROWS, ad).astype(BF16)

        place = lambda full, part: lax.dynamic_update_slice(full, part, (mp, 0))
        merged = gated_merge(place(pool_p, pool_s), place(ssm_p, ssm_s), place(att_p, att_s), proj, cols["gate"],
                             wbp_b, wbs_b, wba_b, i)
        x = matmul_residual(merged, wo_b, i, x, 1.0)
        x = matmul_residual(norm_swiglu(x, norm_ffn2[i], ffn2_w_gu, i), wd2_b, i, x, 0.5)

        def prompt_rows(first, last, name, width):
            return jnp.stack([proj[b * lp + first:b * lp + last, cols[name]:cols[name] + width] for b in range(nb)])

        outs["kp"].append(prompt_rows(0, l, "k", ad).reshape(nb, l, nh_a, hd_a))
        outs["vp"].append(prompt_rows(0, l, "v", ad).reshape(nb, l, nh_a, hd_a))
        outs["lfp"].append(lf_p.reshape(nb, lp, nh_a)[:, :l])
        outs["hp"].append(h_p.reshape(nb, nh_s, hd_s, n_state))
        outs["cp"].append(prompt_rows(l - conv_hist, l, "xbc", cd))
        outs["pp"].append(prompt_rows(l - pool_hist, l, "pool", pd))
        outs["ks"].append(ps0[:, :, cols["k"]:cols["k"] + ad].reshape(ns, 1, nh_a, hd_a))
        outs["vs"].append(ps0[:, :, cols["v"]:cols["v"] + ad].reshape(ns, 1, nh_a, hd_a))
        outs["lfs"].append(lf_s.reshape(ns, DEC_ROWS, nh_a)[:, 0:1])
        outs["hs"].append(h_s.reshape(ns, nh_s, hd_s, n_state))
        outs["cs"].append(jnp.concatenate([state_conv[:, i, 1:], ps0[:, :, cols["xbc"]:cols["xbc"] + cd]], axis=1))
        outs["ps"].append(jnp.concatenate([state_pool[:, i, 1:], ps0[:, :, cols["pool"]:cols["pool"] + pd]], axis=1))

    y_prompt = final_norm_prompt(x, final_norm_w, nb=nb, lp=lp, seq=seq, shift=n_meta)
    y_sample = final_norm(x[mp:], final_norm_w).reshape(ns, DEC_ROWS, dm)[:, 0:1]
    st = lambda k: jnp.stack(outs[k], axis=1)
    return (y_prompt, y_sample, st("kp"), st("vp"), st("lfp"), st("ks"), st("vs"), st("lfs"),
            st("hp"), st("hs"), st("cp"), st("cs"), st("pp"), st("ps"))
```
